```python
import jax, jax.numpy as jnp
from jax import lax
import numpy as np

D_MODEL = 2048
BATCH = 4
SEQ = 2048
DEPTH = 4
DEC_BATCH = 32
DEC_SEQ = 8
PAST_LEN = 16384
PAGE_SIZE = 128

N_MIXERS = 3
ALPHA = (2 * DEPTH) ** 0.25
BETA = (8 * DEPTH) ** -0.25
LN_EPS = 1e-5

GLA_HEADS = 4
GLA_DK = D_MODEL // 2 // GLA_HEADS
GLA_DV = D_MODEL // GLA_HEADS
GLA_RANK = 16
GLA_TAU = 16.0
GLA_CHUNK = 64

SWA_HEAD_DIM = 64
SWA_Q_HEADS = D_MODEL // SWA_HEAD_DIM
SWA_KV_HEADS = 8
SWA_GROUP = SWA_Q_HEADS // SWA_KV_HEADS
WINDOW = 128
ROT_DIM = SWA_HEAD_DIM // 4
ROPE_THETA = 500000.0

SG_WIDTH = D_MODEL
SG_GROUPS = 4
SG_CHUNK = 128

D_FF = 5632
CONV_W = 3

N_GLA_LAYERS = len(range(0, DEPTH, N_MIXERS))
N_SWA_LAYERS = len(range(1, DEPTH, N_MIXERS))
N_SG_LAYERS = len(range(2, DEPTH, N_MIXERS))

kernel_name = 'hybrid_gla_swa_gmlp_convffn_step'


def _layer_norm(x, g, b):
    xf = x.astype(jnp.float32)
    mu = jnp.mean(xf, -1, keepdims=True)
    var = jnp.mean(jnp.square(xf - mu), -1, keepdims=True)
    return ((xf - mu) * lax.rsqrt(var + LN_EPS) * g + b).astype(x.dtype)


def _gla_recurrence(q, k, v, log_a, s0):
    b, L = q.shape[:2]
    c = min(GLA_CHUNK, L)
    n = -(-L // c)
    pad = n * c - L

    def prep(t):
        t = jnp.pad(t.astype(jnp.float32), ((0, 0), (0, pad), (0, 0), (0, 0)))
        return t.reshape(b, n, c, *t.shape[2:]).swapaxes(0, 1)

    causal = jnp.tril(jnp.ones((c, c), dtype=bool))

    def step(s, xs):
        qc, kc, vc, gc = xs
        cum = jnp.cumsum(gc, axis=1)
        q_dec = qc * jnp.exp(cum)
        k_inv = kc * jnp.exp(-cum)
        attn = jnp.where(causal, jnp.einsum('bthd,bshd->bhts', q_dec, k_inv), 0.0)
        o = jnp.einsum('bhts,bshe->bthe', attn, vc) + jnp.einsum('bthd,bhde->bthe', q_dec, s)
        last = cum[:, -1]
        k_last = kc * jnp.exp(last[:, None] - cum)
        s = s * jnp.exp(last)[..., None] + jnp.einsum('bshd,bshe->bhde', k_last, vc)
        return s, o

    s, o = lax.scan(step, s0.astype(jnp.float32), (prep(q), prep(k), prep(v), prep(log_a)))
    o = o.swapaxes(0, 1).reshape(b, n * c, *o.shape[3:])[:, :L]
    return o, s


def _gla_mixer(x, s0, w_in, w_g2, b_g, norm_w, w_out):
    b, L, _ = x.shape
    qk = GLA_HEADS * GLA_DK
    vd = GLA_HEADS * GLA_DV
    q, k, v, r, g_low = jnp.split(x @ w_in, [qk, 2 * qk, 2 * qk + vd, 2 * qk + 2 * vd], axis=-1)
    q = q.reshape(b, L, GLA_HEADS, GLA_DK) * GLA_DK ** -0.5
    k = k.reshape(b, L, GLA_HEADS, GLA_DK)
    v = v.reshape(b, L, GLA_HEADS, GLA_DV)
    log_a = jax.nn.log_sigmoid((g_low @ w_g2 + b_g).astype(jnp.float32)) / GLA_TAU
    log_a = log_a.reshape(b, L, GLA_HEADS, GLA_DK)
    o, s = _gla_recurrence(q, k, v, log_a, s0)
    o = o * lax.rsqrt(jnp.mean(o * o, -1, keepdims=True) + LN_EPS) * norm_w
    o = o.reshape(b, L, vd) * jax.nn.silu(r.astype(jnp.float32))
    return o.astype(x.dtype) @ w_out, s.astype(s0.dtype)


def _rope(x, pos):
    half = ROT_DIM // 2
    inv = ROPE_THETA ** (-jnp.arange(half, dtype=jnp.float32) / half)
    ang = pos.astype(jnp.float32)[:, None] * inv
    bshape = (ang.shape[0],) + (1,) * (x.ndim - 3) + (half,)
    cos = jnp.cos(ang).reshape(bshape)
    sin = jnp.sin(ang).reshape(bshape)
    xf = x.astype(jnp.float32)
    x1 = xf[..., :half]
    x2 = xf[..., half:ROT_DIM]
    out = jnp.concatenate([x1 * cos - x2 * sin, x2 * cos + x1 * sin, xf[..., ROT_DIM:]], -1)
    return out.astype(x.dtype)


def _swa_qkv(x, pos, w_qkv, b_qkv):
    b, L, _ = x.shape
    qd = SWA_Q_HEADS * SWA_HEAD_DIM
    kd = SWA_KV_HEADS * SWA_HEAD_DIM
    q, k, v = jnp.split(x @ w_qkv + b_qkv, [qd, qd + kd], axis=-1)
    q = _rope(q.reshape(b, L, SWA_KV_HEADS, SWA_GROUP, SWA_HEAD_DIM), pos)
    k = _rope(k.reshape(b, L, SWA_KV_HEADS, SWA_HEAD_DIM), pos)
    v = v.reshape(b, L, SWA_KV_HEADS, SWA_HEAD_DIM)
    return q, k, v


def _sink_attention(q, k, v, mask, sinks):
    s = jnp.einsum('...qhgd,...khd->...hgqk', q, k).astype(jnp.float32) * SWA_HEAD_DIM ** -0.5
    s = jnp.where(mask, s, -jnp.inf)
    sink = jnp.broadcast_to(sinks.astype(jnp.float32)[..., None, None], s.shape[:-1] + (1,))
    p = jax.nn.softmax(jnp.concatenate([s, sink], axis=-1), axis=-1)[..., :-1]
    return jnp.einsum('...hgqk,...khd->...qhgd', p.astype(v.dtype), v)


def _swa_prompt(x, w_qkv, b_qkv, sinks, w_out, b_out):
    b, L, _ = x.shape
    pos = jnp.arange(L)
    q, k, v = _swa_qkv(x, pos, w_qkv, b_qkv)
    nb = L // WINDOW
    qb = q.reshape(b, nb, WINDOW, SWA_KV_HEADS, SWA_GROUP, SWA_HEAD_DIM)

    def band(t):
        tb = t.reshape(b, nb, WINDOW, SWA_KV_HEADS, SWA_HEAD_DIM)
        prev = jnp.concatenate([jnp.zeros_like(tb[:, :1]), tb[:, :-1]], axis=1)
        return jnp.concatenate([prev, tb], axis=2)

    qpos = pos.reshape(nb, WINDOW)
    kpos = jnp.concatenate([qpos - WINDOW, qpos], axis=-1)
    diff = qpos[:, :, None] - kpos[:, None, :]
    mask = (diff >= 0) & (diff < WINDOW) & (kpos[:, None, :] >= 0)
    o = _sink_attention(qb, band(k), band(v), mask[None, :, None, None], sinks)
    y = o.reshape(b, L, SWA_Q_HEADS * SWA_HEAD_DIM) @ w_out + b_out
    nbuf = min(WINDOW, L)
    return y, k[:, -nbuf:], v[:, -nbuf:]


def _swa_sample(x, k_buf, v_buf, w_qkv, b_qkv, sinks, w_out, b_out):
    b, L, _ = x.shape
    nbuf = k_buf.shape[1]
    pos = PAST_LEN + jnp.arange(L)
    q, k, v = _swa_qkv(x, pos, w_qkv, b_qkv)
    k_all = jnp.concatenate([k_buf.astype(k.dtype), k], axis=1)
    v_all = jnp.concatenate([v_buf.astype(v.dtype), v], axis=1)
    kpos = jnp.concatenate([PAST_LEN - nbuf + jnp.arange(nbuf), pos])
    diff = pos[:, None] - kpos[None, :]
    mask = (diff >= 0) & (diff < WINDOW)
    o = _sink_attention(q, k_all, v_all, mask, sinks)
    y = o.reshape(b, L, SWA_Q_HEADS * SWA_HEAD_DIM) @ w_out + b_out
    return y, k_all[:, -nbuf:], v_all[:, -nbuf:]


def _sg_mixer(x, w_in, b_in, ln_g, ln_b, w_s, b_s, w_out, b_out):
    b, L, _ = x.shape
    z = jax.nn.gelu(x @ w_in + b_in, approximate=False)
    u, v = jnp.split(z, 2, axis=-1)
    v = _layer_norm(v, ln_g, ln_b)
    c = min(SG_CHUNK, L)
    n = -(-L // c)
    pad = n * c - L
    vc = jnp.pad(v, ((0, 0), (0, pad), (0, 0))).reshape(b, n, c, SG_GROUPS, SG_WIDTH // SG_GROUPS)
    w = jnp.tril(w_s[:, :c, :c])
    mixed = jnp.einsum('gts,bnsgc->bntgc', w, vc) + b_s[:, :c].T[:, :, None]
    mixed = mixed.reshape(b, n * c, SG_WIDTH)[:, :L]
    return (u * mixed) @ w_out + b_out, v


def _conv_ffn(x, prev, w_in, conv_w, conv_b, w_out):
    L = x.shape[1]
    gate, val = jnp.split(x @ w_in, 2, axis=-1)
    ext = jnp.concatenate([prev.astype(gate.dtype), gate], axis=1)
    conv = conv_b
    for i in range(CONV_W):
        conv = conv + conv_w[i] * ext[:, i:i + L]
    y = (jax.nn.gelu(conv, approximate=False) * val) @ w_out
    return y, ext[:, -(CONV_W - 1):]


def setup_inputs(seed: int = 0) -> dict:
    key = jax.random.key(seed)
    ks = iter(jax.random.split(key, 40))

    def nrm(shape, scale=1.0):
        return jax.random.normal(next(ks), shape, jnp.float32) * scale

    qk = GLA_HEADS * GLA_DK
    vd = GLA_HEADS * GLA_DV
    swa_buf = min(WINDOW, PAST_LEN)
    return {
        'x_prompt': nrm((BATCH, SEQ, D_MODEL)),
        'x_sample': nrm((DEC_BATCH, DEC_SEQ, D_MODEL)),
        'state_gla': nrm((N_GLA_LAYERS, DEC_BATCH, GLA_HEADS, GLA_DK, GLA_DV)),
        'cache_swa_k': nrm((N_SWA_LAYERS, DEC_BATCH, swa_buf, SWA_KV_HEADS, SWA_HEAD_DIM)),
        'cache_swa_v': nrm((N_SWA_LAYERS, DEC_BATCH, swa_buf, SWA_KV_HEADS, SWA_HEAD_DIM)),
        'state_ffn_conv': nrm((DEPTH, DEC_BATCH, CONV_W - 1, D_FF)),
        'ln_mix_g': 1.0 + nrm((DEPTH, D_MODEL), 0.02),
        'ln_mix_b': nrm((DEPTH, D_MODEL), 0.02),
        'ln_ffn_g': 1.0 + nrm((DEPTH, D_MODEL), 0.02),
        'ln_ffn_b': nrm((DEPTH, D_MODEL), 0.02),
        'gla_w_in': nrm((N_GLA_LAYERS, D_MODEL, 2 * qk + 2 * vd + GLA_RANK), D_MODEL ** -0.5),
        'gla_w_g2': nrm((N_GLA_LAYERS, GLA_RANK, qk), GLA_RANK ** -0.5),
        'gla_b_g': nrm((N_GLA_LAYERS, qk), 0.1),
        'gla_norm_w': 1.0 + nrm((N_GLA_LAYERS, GLA_DV), 0.02),
        'gla_w_out': nrm((N_GLA_LAYERS, vd, D_MODEL), vd ** -0.5 * BETA),
        'swa_w_qkv': nrm((N_SWA_LAYERS, D_MODEL, (SWA_Q_HEADS + 2 * SWA_KV_HEADS) * SWA_HEAD_DIM), D_MODEL ** -0.5),
        'swa_b_qkv': nrm((N_SWA_LAYERS, (SWA_Q_HEADS + 2 * SWA_KV_HEADS) * SWA_HEAD_DIM), 0.02),
        'swa_sinks': nrm((N_SWA_LAYERS, SWA_KV_HEADS, SWA_GROUP), 1.0),
        'swa_w_out': nrm((N_SWA_LAYERS, SWA_Q_HEADS * SWA_HEAD_DIM, D_MODEL), (SWA_Q_HEADS * SWA_HEAD_DIM) ** -0.5 * BETA),
        'swa_b_out': nrm((N_SWA_LAYERS, D_MODEL), 0.02),
        'sg_w_in': nrm((N_SG_LAYERS, D_MODEL, 2 * SG_WIDTH), D_MODEL ** -0.5),
        'sg_b_in': nrm((N_SG_LAYERS, 2 * SG_WIDTH), 0.02),
        'sg_ln_g': 1.0 + nrm((N_SG_LAYERS, SG_WIDTH), 0.02),
        'sg_ln_b': nrm((N_SG_LAYERS, SG_WIDTH), 0.02),
        'sg_w_s': nrm((N_SG_LAYERS, SG_GROUPS, SG_CHUNK, SG_CHUNK), SG_CHUNK ** -0.5),
        'sg_b_s': 1.0 + nrm((N_SG_LAYERS, SG_GROUPS, SG_CHUNK), 0.02),
        'sg_w_out': nrm((N_SG_LAYERS, SG_WIDTH, D_MODEL), SG_WIDTH ** -0.5 * BETA),
        'sg_b_out': nrm((N_SG_LAYERS, D_MODEL), 0.02),
        'ffn_w_in': nrm((DEPTH, D_MODEL, 2 * D_FF), D_MODEL ** -0.5),
        'ffn_conv_w': nrm((DEPTH, CONV_W, D_FF), CONV_W ** -0.5),
        'ffn_conv_b': nrm((DEPTH, D_FF), 0.02),
        'ffn_w_out': nrm((DEPTH, D_FF, D_MODEL), D_FF ** -0.5 * BETA),
    }


def reference(x_prompt, x_sample, state_gla, cache_swa_k, cache_swa_v, state_ffn_conv,
              ln_mix_g, ln_mix_b, ln_ffn_g, ln_ffn_b,
              gla_w_in, gla_w_g2, gla_b_g, gla_norm_w, gla_w_out,
              swa_w_qkv, swa_b_qkv, swa_sinks, swa_w_out, swa_b_out,
              sg_w_in, sg_b_in, sg_ln_g, sg_ln_b, sg_w_s, sg_b_s, sg_w_out, sg_b_out,
              ffn_w_in, ffn_conv_w, ffn_conv_b, ffn_w_out):
    xp, xs = x_prompt, x_sample
    gla_p, gla_s = [], []
    swk_p, swv_p, swk_s, swv_s = [], [], [], []
    sgv_s = []
    conv_p, conv_s = [], []
    for i in range(DEPTH):
        j = i // N_MIXERS
        kind = i % N_MIXERS
        if kind == 0:
            w = (gla_w_in[j], gla_w_g2[j], gla_b_g[j], gla_norm_w[j], gla_w_out[j])
            s0 = jnp.zeros((xp.shape[0], GLA_HEADS, GLA_DK, GLA_DV), state_gla.dtype)
            mp, st_p = _gla_mixer(xp, s0, *w)
            ms, st_s = _gla_mixer(xs, state_gla[j], *w)
            gla_p.append(st_p)
            gla_s.append(st_s)
        elif kind == 1:
            w = (swa_w_qkv[j], swa_b_qkv[j], swa_sinks[j], swa_w_out[j], swa_b_out[j])
            mp, kp, vp = _swa_prompt(xp, *w)
            ms, ks_, vs_ = _swa_sample(xs, cache_swa_k[j], cache_swa_v[j], *w)
            swk_p.append(kp)
            swv_p.append(vp)
            swk_s.append(ks_)
            swv_s.append(vs_)
        else:
            w = (sg_w_in[j], sg_b_in[j], sg_ln_g[j], sg_ln_b[j], sg_w_s[j], sg_b_s[j], sg_w_out[j], sg_b_out[j])
            mp, _ = _sg_mixer(xp, *w)
            ms, v_rows = _sg_mixer(xs, *w)
            sgv_s.append(v_rows)
        xp = _layer_norm(ALPHA * xp + mp, ln_mix_g[i], ln_mix_b[i])
        xs = _layer_norm(ALPHA * xs + ms, ln_mix_g[i], ln_mix_b[i])
        wf = (ffn_w_in[i], ffn_conv_w[i], ffn_conv_b[i], ffn_w_out[i])
        fp, cp = _conv_ffn(xp, jnp.zeros((xp.shape[0], CONV_W - 1, D_FF), xp.dtype), *wf)
        fs, cs = _conv_ffn(xs, state_ffn_conv[i], *wf)
        conv_p.append(cp)
        conv_s.append(cs)
        xp = _layer_norm(ALPHA * xp + fp, ln_ffn_g[i], ln_ffn_b[i])
        xs = _layer_norm(ALPHA * xs + fs, ln_ffn_g[i], ln_ffn_b[i])
    return (xp, xs, jnp.stack(gla_p), jnp.stack(gla_s), jnp.stack(swk_p), jnp.stack(swv_p),
            jnp.stack(swk_s), jnp.stack(swv_s), jnp.stack(sgv_s), jnp.stack(conv_p), jnp.stack(conv_s))
```

```python
import functools

import jax
import jax.numpy as jnp
from jax import lax
from jax.experimental import pallas as pl
from jax.experimental.pallas import tpu as pltpu

F32 = jnp.float32
BF16 = jnp.bfloat16

D_MODEL = 2048
DEPTH = 4
PAST_LEN = 16384
N_MIXERS = 3
ALPHA = (2 * DEPTH) ** 0.25
LN_EPS = 1e-5

GLA_HEADS = 4
GLA_DK = 256
GLA_DV = 512
GLA_RANK = 16
GLA_TAU = 16.0
GLA_CHUNK = 64
GLA_QK = GLA_HEADS * GLA_DK
GLA_VD = GLA_HEADS * GLA_DV
GLA_PROJ = 2 * GLA_QK + 2 * GLA_VD
GLA_GLOW_PAD = 128
GLA_PROJ_PAD = 6400

SWA_HEAD_DIM = 64
SWA_Q_HEADS = 32
SWA_KV_HEADS = 8
SWA_GROUP = 4
WINDOW = 128
ROT_DIM = 16
ROPE_THETA = 500000.0
SWA_QD = SWA_Q_HEADS * SWA_HEAD_DIM
SWA_KD = SWA_KV_HEADS * SWA_HEAD_DIM

SG_WIDTH = 2048
SG_GROUPS = 4
SG_GW = SG_WIDTH // SG_GROUPS
SG_CHUNK = 128

D_FF = 5632
CONV_W = 3

VMEM_LIMIT_BYTES = 56 * 1024 * 1024


def _cparams(*sem):
    return pltpu.CompilerParams(dimension_semantics=sem, vmem_limit_bytes=VMEM_LIMIT_BYTES)


def _dot(a, b):
    return jnp.dot(a, b, preferred_element_type=F32)


def _dot_nt(a, b):
    return lax.dot_general(a, b, (((1,), (1,)), ((), ())), preferred_element_type=F32)


def _dot_tn(a, b):
    return lax.dot_general(a, b, (((0,), (0,)), ((), ())), preferred_element_type=F32)


def _layer_norm(x, g, b):
    mu = jnp.mean(x, -1, keepdims=True)
    xc = x - mu
    var = jnp.mean(xc * xc, -1, keepdims=True)
    return xc * lax.rsqrt(var + LN_EPS) * g + b


def _gelu(x):
    return 0.5 * x * (1.0 + lax.erf(x * (0.5 ** 0.5)))


def _mm_kernel(x_ref, w_ref, o_ref):
    o_ref[...] = _dot(x_ref[...], w_ref[...])


def _matmul(xb, w, tm, tn):
    m, k = xb.shape
    n = w.shape[1]
    return pl.pallas_call(
        _mm_kernel,
        grid=(n // tn, m // tm),
        in_specs=[pl.BlockSpec((tm, k), lambda j, i: (i, 0)),
                  pl.BlockSpec((k, tn), lambda j, i: (0, j))],
        out_specs=pl.BlockSpec((tm, tn), lambda j, i: (i, j)),
        out_shape=jax.ShapeDtypeStruct((m, n), F32),
        compiler_params=_cparams("arbitrary", "arbitrary"),
        name="gla_proj",
    )(xb, w)


def _gla_kernel(*refs, t_rows, chunk, has_init):
    if has_init:
        (q_ref, k_ref, v_ref, r_ref, gl_ref, wg2_ref, bg_ref, nw_ref, s0_ref,
         o_ref, sout_ref, st_ref) = refs
    else:
        (q_ref, k_ref, v_ref, r_ref, gl_ref, wg2_ref, bg_ref, nw_ref,
         o_ref, sout_ref, st_ref) = refs
    t = pl.program_id(2)

    @pl.when(t == 0)
    def _():
        if has_init:
            st_ref[...] = s0_ref[0, 0].T
        else:
            st_ref[...] = jnp.zeros_like(st_ref)

    ga = _dot(gl_ref[...].astype(BF16), wg2_ref[...]) + bg_ref[...]
    la = (jnp.minimum(ga, 0.0) - jnp.log1p(jnp.exp(-jnp.abs(ga)))) * (1.0 / GLA_TAU)

    rowmod = lax.broadcasted_iota(jnp.int32, la.shape, 0) % chunk
    cum = la
    s = 1
    while s < chunk:
        cum = cum + jnp.where(rowmod >= s, pltpu.roll(cum, s, axis=0), 0.0)
        s *= 2

    q = q_ref[...] * (GLA_DK ** -0.5)
    k = k_ref[...]
    q_dec = (q * jnp.exp(cum)).astype(BF16)
    k_inv = (k * jnp.exp(-cum)).astype(BF16)
    v = v_ref[...].astype(BF16)

    n_chunks = t_rows // chunk
    tri_r = lax.broadcasted_iota(jnp.int32, (chunk, chunk), 0)
    tri_c = lax.broadcasted_iota(jnp.int32, (chunk, chunk), 1)
    causal = tri_c <= tri_r
    outs = []
    for c in range(n_chunks):
        lo, hi = c * chunk, (c + 1) * chunk
        cum_c = cum[lo:hi]
        last = cum_c[chunk - 1:chunk, :]
        k_last = (k[lo:hi] * jnp.exp(last - cum_c)).astype(BF16)
        qd, ki, vc = q_dec[lo:hi], k_inv[lo:hi], v[lo:hi]
        attn = jnp.where(causal, _dot_nt(qd, ki), 0.0).astype(BF16)
        st = st_ref[...]
        outs.append(_dot(attn, vc) + _dot_nt(qd, st.astype(BF16)))
        st_ref[...] = st * jnp.exp(last) + _dot_tn(vc, k_last)
    o = outs[0] if n_chunks == 1 else jnp.concatenate(outs, axis=0)

    o = o * lax.rsqrt(jnp.mean(o * o, -1, keepdims=True) + LN_EPS) * nw_ref[...]
    r = r_ref[...]
    o = o * (r * jax.nn.sigmoid(r))
    o_ref[...] = o.astype(o_ref.dtype)

    @pl.when(t == pl.num_programs(2) - 1)
    def _():
        sout_ref[0, 0] = st_ref[...].T


def _gla_core(proj, wg2p, bg, norm_w, s0, batch, seq, t_rows, chunk, out_dtype):
    nt = seq // t_rows
    has_init = s0 is not None
    row = lambda b, h, t: b * nt + t
    in_specs = [
        pl.BlockSpec((t_rows, GLA_DK), lambda b, h, t: (row(b, h, t), h)),
        pl.BlockSpec((t_rows, GLA_DK), lambda b, h, t: (row(b, h, t), GLA_HEADS + h)),
        pl.BlockSpec((t_rows, GLA_DV), lambda b, h, t: (row(b, h, t), 2 * GLA_QK // GLA_DV + h)),
        pl.BlockSpec((t_rows, GLA_DV), lambda b, h, t: (row(b, h, t), (2 * GLA_QK + GLA_VD) // GLA_DV + h)),
        pl.BlockSpec((t_rows, GLA_GLOW_PAD), lambda b, h, t: (row(b, h, t), GLA_PROJ // GLA_GLOW_PAD)),
        pl.BlockSpec((GLA_GLOW_PAD, GLA_DK), lambda b, h, t: (0, h)),
        pl.BlockSpec((1, GLA_DK), lambda b, h, t: (0, h)),
        pl.BlockSpec((1, GLA_DV), lambda b, h, t: (0, 0)),
    ]
    args = [proj, proj, proj, proj, proj, wg2p, bg, norm_w]
    if has_init:
        in_specs.append(pl.BlockSpec((1, 1, GLA_DK, GLA_DV), lambda b, h, t: (b, h, 0, 0)))
        args.append(s0)
    return pl.pallas_call(
        functools.partial(_gla_kernel, t_rows=t_rows, chunk=chunk, has_init=has_init),
        grid=(batch, GLA_HEADS, nt),
        in_specs=in_specs,
        out_specs=[pl.BlockSpec((t_rows, GLA_DV), lambda b, h, t: (row(b, h, t), h)),
                   pl.BlockSpec((1, 1, GLA_DK, GLA_DV), lambda b, h, t: (b, h, 0, 0))],
        out_shape=[jax.ShapeDtypeStruct((batch * seq, GLA_VD), out_dtype),
                   jax.ShapeDtypeStruct((batch, GLA_HEADS, GLA_DK, GLA_DV), F32)],
        scratch_shapes=[pltpu.VMEM((GLA_DV, GLA_DK), F32)],
        compiler_params=_cparams("arbitrary", "arbitrary", "arbitrary"),
        name="gla_core",
    )(*args)


def _out_ln_kernel(*refs, has_bias):
    if has_bias:
        a_ref, w_ref, bias_ref, res_ref, g_ref, b_ref, of_ref, ob_ref = refs
    else:
        a_ref, w_ref, res_ref, g_ref, b_ref, of_ref, ob_ref = refs
    y = _dot(a_ref[...].astype(BF16), w_ref[...])
    if has_bias:
        y = y + bias_ref[...]
    o = _layer_norm(ALPHA * res_ref[...] + y, g_ref[...], b_ref[...])
    of_ref[...] = o
    ob_ref[...] = o.astype(BF16)


def _resident(shape, index_map):
    return pl.BlockSpec(shape, index_map, pipeline_mode=pl.Buffered(1))


def _out_ln(a, w, bias, res, g, b, tm, name):
    m, k = a.shape
    has_bias = bias is not None
    vec = pl.BlockSpec((1, D_MODEL), lambda i: (0, 0))
    in_specs = [pl.BlockSpec((tm, k), lambda i: (i, 0)),
                _resident((k, D_MODEL), lambda i: (0, 0))]
    args = [a, w]
    if has_bias:
        in_specs.append(vec)
        args.append(bias)
    in_specs += [pl.BlockSpec((tm, D_MODEL), lambda i: (i, 0)), vec, vec]
    args += [res, g, b]
    return pl.pallas_call(
        functools.partial(_out_ln_kernel, has_bias=has_bias),
        grid=(m // tm,),
        in_specs=in_specs,
        out_specs=[pl.BlockSpec((tm, D_MODEL), lambda i: (i, 0)),
                   pl.BlockSpec((tm, D_MODEL), lambda i: (i, 0))],
        out_shape=[jax.ShapeDtypeStruct((m, D_MODEL), F32),
                   jax.ShapeDtypeStruct((m, D_MODEL), BF16)],
        compiler_params=_cparams("arbitrary"),
        name=name,
    )(*args)


def _conv_glu(g, val, g1, g2, cw_ref, cb_ref):
    conv = cb_ref[...] + cw_ref[0:1, :] * g2
    conv = conv + cw_ref[1:2, :] * g1
    conv = conv + cw_ref[2:3, :] * g
    return (_gelu(conv) * val).astype(BF16)


def _ffn_a_prompt_kernel(x_ref, wg_ref, wv_ref, cw_ref, cb_ref, u_ref, gl_ref, carry_ref, *, tm, seq_tiles):
    i = pl.program_id(1)
    x = x_ref[...]
    g = _dot(x, wg_ref[...])
    val = _dot(x, wv_ref[...])

    @pl.when(i % seq_tiles == 0)
    def _():
        carry_ref[...] = jnp.zeros_like(carry_ref)

    c = carry_ref[...]
    row = lax.broadcasted_iota(jnp.int32, g.shape, 0)
    g1 = jnp.where(row >= 1, pltpu.roll(g, 1, axis=0), c[7:8, :])
    g2 = jnp.where(row >= 2, pltpu.roll(g, 2, axis=0), jnp.where(row == 0, c[6:7, :], c[7:8, :]))
    u_ref[...] = _conv_glu(g, val, g1, g2, cw_ref, cb_ref)
    carry_ref[...] = g[tm - 8:tm, :]

    @pl.when(i % seq_tiles == seq_tiles - 1)
    def _():
        gl_ref[...] = g[tm - 8:tm, :]


def _ffn_a_sample_kernel(x_ref, wg_ref, wv_ref, cw_ref, cb_ref, p1_ref, p2_ref, u_ref, gl_ref, *, seq):
    x = x_ref[...]
    g = _dot(x, wg_ref[...])
    val = _dot(x, wv_ref[...])
    pos = lax.broadcasted_iota(jnp.int32, g.shape, 0) % seq
    g1 = jnp.where(pos >= 1, pltpu.roll(g, 1, axis=0), p1_ref[...])
    g2 = jnp.where(pos >= 2, pltpu.roll(g, 2, axis=0), p2_ref[...])
    u_ref[...] = _conv_glu(g, val, g1, g2, cw_ref, cb_ref)
    gl_ref[...] = g


def _ffn_a_prompt(xb, w_in, conv_w, conv_b, batch, seq, tm, tn):
    m = batch * seq
    nj = D_FF // tn
    seq_tiles = seq // tm
    return pl.pallas_call(
        functools.partial(_ffn_a_prompt_kernel, tm=tm, seq_tiles=seq_tiles),
        grid=(nj, m // tm),
        in_specs=[pl.BlockSpec((tm, D_MODEL), lambda j, i: (i, 0)),
                  pl.BlockSpec((D_MODEL, tn), lambda j, i: (0, j)),
                  pl.BlockSpec((D_MODEL, tn), lambda j, i: (0, nj + j)),
                  pl.BlockSpec((CONV_W, tn), lambda j, i: (0, j)),
                  pl.BlockSpec((1, tn), lambda j, i: (0, j))],
        out_specs=[pl.BlockSpec((tm, tn), lambda j, i: (i, j)),
                   pl.BlockSpec((8, tn), lambda j, i: (i // seq_tiles, j))],
        out_shape=[jax.ShapeDtypeStruct((m, D_FF), BF16),
                   jax.ShapeDtypeStruct((batch * 8, D_FF), F32)],
        scratch_shapes=[pltpu.VMEM((8, tn), F32)],
        compiler_params=_cparams("arbitrary", "arbitrary"),
        name="ffn_a_prompt",
    )(xb, w_in, w_in, conv_w, conv_b)


def _ffn_a_sample(xb, w_in, conv_w, conv_b, prev1, prev2, seq, tn):
    m = xb.shape[0]
    nj = D_FF // tn
    return pl.pallas_call(
        functools.partial(_ffn_a_sample_kernel, seq=seq),
        grid=(nj,),
        in_specs=[pl.BlockSpec((m, D_MODEL), lambda j: (0, 0)),
                  pl.BlockSpec((D_MODEL, tn), lambda j: (0, j)),
                  pl.BlockSpec((D_MODEL, tn), lambda j: (0, nj + j)),
                  pl.BlockSpec((CONV_W, tn), lambda j: (0, j)),
                  pl.BlockSpec((1, tn), lambda j: (0, j)),
                  pl.BlockSpec((m, tn), lambda j: (0, j)),
                  pl.BlockSpec((m, tn), lambda j: (0, j))],
        out_specs=[pl.BlockSpec((m, tn), lambda j: (0, j)),
                   pl.BlockSpec((m, tn), lambda j: (0, j))],
        out_shape=[jax.ShapeDtypeStruct((m, D_FF), BF16),
                   jax.ShapeDtypeStruct((m, D_FF), F32)],
        compiler_params=_cparams("arbitrary"),
        name="ffn_a_sample",
    )(xb, w_in, w_in, conv_w, conv_b, prev1, prev2)


def _swa_qkv_kernel(x_ref, w_ref, b_ref, ca_ref, cm_ref, cp_ref, o_ref, *, tn, n_rot_tiles):
    j = pl.program_id(0)
    y = _dot(x_ref[...], w_ref[...]) + b_ref[...]

    @pl.when(j < n_rot_tiles)
    def _():
        reps = tn // ca_ref.shape[1]
        ca = jnp.concatenate([ca_ref[...]] * reps, axis=1)
        cm = jnp.concatenate([cm_ref[...]] * reps, axis=1)
        cp = jnp.concatenate([cp_ref[...]] * reps, axis=1)
        o_ref[...] = y * ca + pltpu.roll(y, tn - ROT_DIM // 2, axis=1) * cm + pltpu.roll(y, ROT_DIM // 2, axis=1) * cp

    @pl.when(j >= n_rot_tiles)
    def _():
        o_ref[...] = y


def _rope_tables(pos):
    half = ROT_DIM // 2
    inv = ROPE_THETA ** (-jnp.arange(half, dtype=F32) / half)
    ang = pos.astype(F32)[:, None] * inv
    cos, sin = jnp.cos(ang), jnp.sin(ang)
    n = pos.shape[0]
    rest = SWA_HEAD_DIM - ROT_DIM
    ca = jnp.concatenate([cos, cos, jnp.ones((n, rest), F32)], -1)
    cm = jnp.concatenate([-sin, jnp.zeros((n, half + rest), F32)], -1)
    cp = jnp.concatenate([jnp.zeros((n, half), F32), sin, jnp.zeros((n, rest), F32)], -1)
    return tuple(jnp.tile(t, (1, 128 // SWA_HEAD_DIM)) for t in (ca, cm, cp))


def _swa_qkv(xb, w, b, tables, seq, tm, tn):
    m = xb.shape[0]
    n = w.shape[1]
    pos_tiles = seq // tm if seq >= tm else 1
    tab = pl.BlockSpec((tm, 128), lambda j, i: (i % pos_tiles, 0))
    return pl.pallas_call(
        functools.partial(_swa_qkv_kernel, tn=tn, n_rot_tiles=(SWA_QD + SWA_KD) // tn),
        grid=(n // tn, m // tm),
        in_specs=[pl.BlockSpec((tm, D_MODEL), lambda j, i: (i, 0)),
                  pl.BlockSpec((D_MODEL, tn), lambda j, i: (0, j)),
                  pl.BlockSpec((1, tn), lambda j, i: (0, j)),
                  tab, tab, tab],
        out_specs=pl.BlockSpec((tm, tn), lambda j, i: (i, j)),
        out_shape=jax.ShapeDtypeStruct((m, n), F32),
        compiler_params=_cparams("arbitrary", "arbitrary"),
        name="swa_qkv",
    )(xb, w, b, *tables)


def _sink_softmax_pv(s, mask, sink, v):
    s = jnp.where(mask, s * (SWA_HEAD_DIM ** -0.5), -jnp.inf)
    mx = jnp.maximum(jnp.max(s, -1, keepdims=True), sink)
    p = jnp.exp(s - mx)
    denom = jnp.sum(p, -1, keepdims=True) + jnp.exp(sink - mx)
    return _dot((p / denom).astype(BF16), v)


def _swa_prompt_kernel(sink_ref, q_ref, kp_ref, kc_ref, vp_ref, vc_ref, o_ref):
    i = pl.program_id(1)
    q = q_ref[...].astype(BF16)
    kb = jnp.concatenate([kp_ref[...], kc_ref[...]], axis=0).astype(BF16)
    vb = jnp.concatenate([vp_ref[...], vc_ref[...]], axis=0).astype(BF16)
    r = lax.broadcasted_iota(jnp.int32, (WINDOW, 2 * WINDOW), 0)
    c = lax.broadcasted_iota(jnp.int32, (WINDOW, 2 * WINDOW), 1)
    mask = (c > r) & (c <= r + WINDOW) & (c >= jnp.where(i > 0, 0, WINDOW))
    pieces = []
    for h in range(SWA_KV_HEADS):
        kh = kb[:, h * SWA_HEAD_DIM:(h + 1) * SWA_HEAD_DIM]
        vh = vb[:, h * SWA_HEAD_DIM:(h + 1) * SWA_HEAD_DIM]
        for g in range(SWA_GROUP):
            hq = h * SWA_GROUP + g
            qg = q[:, hq * SWA_HEAD_DIM:(hq + 1) * SWA_HEAD_DIM]
            pieces.append(_sink_softmax_pv(_dot_nt(qg, kh), mask, sink_ref[hq], vh))
    o_ref[...] = jnp.concatenate(pieces, axis=-1).astype(o_ref.dtype)


def _swa_prompt_attn(qkv, sinks, batch, seq):
    nb = seq // WINDOW
    kcol = SWA_QD // SWA_KD
    cur = lambda b, i: b * nb + i
    prev = lambda b, i: b * nb + jnp.maximum(i - 1, 0)
    return pl.pallas_call(
        _swa_prompt_kernel,
        grid=(batch, nb),
        in_specs=[pl.BlockSpec(memory_space=pltpu.SMEM),
                  pl.BlockSpec((WINDOW, SWA_QD), lambda b, i: (cur(b, i), 0)),
                  pl.BlockSpec((WINDOW, SWA_KD), lambda b, i: (prev(b, i), kcol)),
                  pl.BlockSpec((WINDOW, SWA_KD), lambda b, i: (cur(b, i), kcol)),
                  pl.BlockSpec((WINDOW, SWA_KD), lambda b, i: (prev(b, i), kcol + 1)),
                  pl.BlockSpec((WINDOW, SWA_KD), lambda b, i: (cur(b, i), kcol + 1))],
        out_specs=pl.BlockSpec((WINDOW, SWA_QD), lambda b, i: (cur(b, i), 0)),
        out_shape=jax.ShapeDtypeStruct((batch * seq, SWA_QD), BF16),
        compiler_params=_cparams("arbitrary", "arbitrary"),
        name="swa_prompt_attn",
    )(sinks, qkv, qkv, qkv, qkv, qkv)


def _swa_sample_kernel(sink_ref, q_ref, kn_ref, vn_ref, kc_ref, vc_ref, o_ref, ko_ref, vo_ref, *, seq):
    nbuf = kc_ref.shape[1]
    k_all = jnp.concatenate([kc_ref[0], kn_ref[...]], axis=0)
    v_all = jnp.concatenate([vc_ref[0], vn_ref[...]], axis=0)
    ko_ref[0] = k_all[seq:, :]
    vo_ref[0] = v_all[seq:, :]
    kb = k_all.astype(BF16)
    vb = v_all.astype(BF16)
    q = q_ref[...]
    rows = SWA_GROUP * seq
    l = lax.broadcasted_iota(jnp.int32, (rows, nbuf + seq), 0) % seq
    c = lax.broadcasted_iota(jnp.int32, (rows, nbuf + seq), 1)
    diff = l + nbuf - c
    mask = (diff >= 0) & (diff < WINDOW)
    grp = lax.broadcasted_iota(jnp.int32, (rows, 1), 0) // seq
    pieces = []
    for h in range(SWA_KV_HEADS):
        kh = kb[:, h * SWA_HEAD_DIM:(h + 1) * SWA_HEAD_DIM]
        vh = vb[:, h * SWA_HEAD_DIM:(h + 1) * SWA_HEAD_DIM]
        qs = jnp.concatenate(
            [q[:, (h * SWA_GROUP + g) * SWA_HEAD_DIM:(h * SWA_GROUP + g + 1) * SWA_HEAD_DIM]
             for g in range(SWA_GROUP)], axis=0).astype(BF16)
        sink = jnp.zeros((rows, 1), F32)
        for g in range(SWA_GROUP):
            sink = jnp.where(grp == g, sink_ref[h * SWA_GROUP + g], sink)
        o = _sink_softmax_pv(_dot_nt(qs, kh), mask, sink, vh)
        pieces += [o[g * seq:(g + 1) * seq, :] for g in range(SWA_GROUP)]
    o_ref[...] = jnp.concatenate(pieces, axis=-1)


def _swa_sample_attn(qkv, sinks, k_cache, v_cache, batch, seq):
    nbuf = k_cache.shape[1]
    kcol = SWA_QD // SWA_KD
    cache = pl.BlockSpec((1, nbuf, SWA_KD), lambda b: (b, 0, 0))
    return pl.pallas_call(
        functools.partial(_swa_sample_kernel, seq=seq),
        grid=(batch,),
        in_specs=[pl.BlockSpec(memory_space=pltpu.SMEM),
                  pl.BlockSpec((seq, SWA_QD), lambda b: (b, 0)),
                  pl.BlockSpec((seq, SWA_KD), lambda b: (b, kcol)),
                  pl.BlockSpec((seq, SWA_KD), lambda b: (b, kcol + 1)),
                  cache, cache],
        out_specs=[pl.BlockSpec((seq, SWA_QD), lambda b: (b, 0)), cache, cache],
        out_shape=[jax.ShapeDtypeStruct((batch * seq, SWA_QD), F32),
                   jax.ShapeDtypeStruct((batch, nbuf, SWA_KD), F32),
                   jax.ShapeDtypeStruct((batch, nbuf, SWA_KD), F32)],
        compiler_params=_cparams("arbitrary"),
        name="swa_sample_attn",
    )(sinks, qkv, qkv, qkv, k_cache, v_cache)


def _sg_in_kernel(x_ref, w_ref, b_ref, g_ref, beta_ref, o_ref):
    j = pl.program_id(0)
    z = _gelu(_dot(x_ref[...], w_ref[...]) + b_ref[...])

    @pl.when(j == 0)
    def _():
        o_ref[0] = z

    @pl.when(j == 1)
    def _():
        o_ref[0] = _layer_norm(z, g_ref[...], beta_ref[...])


def _sg_in(xb, w, b, ln_g, ln_b, tm):
    m = xb.shape[0]
    vec = pl.BlockSpec((1, SG_WIDTH), lambda j, i: (0, 0))
    return pl.pallas_call(
        _sg_in_kernel,
        grid=(2, m // tm),
        in_specs=[pl.BlockSpec((tm, D_MODEL), lambda j, i: (i, 0)),
                  pl.BlockSpec((D_MODEL, SG_WIDTH), lambda j, i: (0, j)),
                  pl.BlockSpec((1, SG_WIDTH), lambda j, i: (0, j)),
                  vec, vec],
        out_specs=pl.BlockSpec((1, tm, SG_WIDTH), lambda j, i: (j, i, 0)),
        out_shape=jax.ShapeDtypeStruct((2, m, SG_WIDTH), F32),
        compiler_params=_cparams("arbitrary", "arbitrary"),
        name="sg_in",
    )(xb, w, b, ln_g, ln_b)


def _sg_out_kernel(u_ref, v_ref, ws_ref, bs_ref, w_ref, bias_ref, res_ref, g_ref, b_ref, of_ref, ob_ref,
                   *, chunk, period):
    tm = u_ref.shape[1]
    r = lax.broadcasted_iota(jnp.int32, (chunk, chunk), 0)
    c = lax.broadcasted_iota(jnp.int32, (chunk, chunk), 1)
    mask = (c <= r) & (r // period == c // period)
    ws = [jnp.where(mask, ws_ref[g], 0.0).astype(BF16) for g in range(SG_GROUPS)]
    rows = []
    for n in range(tm // chunk):
        lo, hi = n * chunk, (n + 1) * chunk
        cols = []
        for g in range(SG_GROUPS):
            vg = v_ref[0, lo:hi, g * SG_GW:(g + 1) * SG_GW].astype(BF16)
            mixed = _dot(ws[g], vg) + bs_ref[:, g:g + 1]
            cols.append((u_ref[0, lo:hi, g * SG_GW:(g + 1) * SG_GW] * mixed).astype(BF16))
        rows.append(jnp.concatenate(cols, axis=-1))
    a = rows[0] if len(rows) == 1 else jnp.concatenate(rows, axis=0)
    y = _dot(a, w_ref[...]) + bias_ref[...]
    o = _layer_norm(ALPHA * res_ref[...] + y, g_ref[...], b_ref[...])
    of_ref[...] = o
    ob_ref[...] = o.astype(BF16)


def _sg_out(uv, ws, bs_col, w_out, b_out, res, g, b, tm, chunk, period):
    m = res.shape[0]
    vec = pl.BlockSpec((1, D_MODEL), lambda i: (0, 0))
    return pl.pallas_call(
        functools.partial(_sg_out_kernel, chunk=chunk, period=period),
        grid=(m // tm,),
        in_specs=[pl.BlockSpec((1, tm, SG_WIDTH), lambda i: (0, i, 0)),
                  pl.BlockSpec((1, tm, SG_WIDTH), lambda i: (1, i, 0)),
                  pl.BlockSpec((SG_GROUPS, chunk, chunk), lambda i: (0, 0, 0)),
                  pl.BlockSpec((chunk, SG_GROUPS), lambda i: (0, 0)),
                  _resident((SG_WIDTH, D_MODEL), lambda i: (0, 0)),
                  vec,
                  pl.BlockSpec((tm, D_MODEL), lambda i: (i, 0)),
                  vec, vec],
        out_specs=[pl.BlockSpec((tm, D_MODEL), lambda i: (i, 0)),
                   pl.BlockSpec((tm, D_MODEL), lambda i: (i, 0))],
        out_shape=[jax.ShapeDtypeStruct((m, D_MODEL), F32),
                   jax.ShapeDtypeStruct((m, D_MODEL), BF16)],
        compiler_params=_cparams("arbitrary"),
        name="sg_out",
    )(uv, uv, ws, bs_col, w_out, b_out, res, g, b)


def _row(v):
    return v.reshape(1, -1)


def _trunk(x, batch, seq, is_prompt, state_gla, cache_k, cache_v, state_conv, p):
    m = batch * seq
    tm = 512 if is_prompt else m
    tm_small = 256 if is_prompt else m
    xf = x.reshape(m, D_MODEL)
    xb = xf.astype(BF16)
    gla_states, swa_k, swa_v, sg_v, conv_states = [], [], [], [], []
    for i in range(DEPTH):
        j = i // N_MIXERS
        kind = i % N_MIXERS
        ln_g, ln_b = _row(p['ln_mix_g'][i]), _row(p['ln_mix_b'][i])
        if kind == 0:
            proj = _matmul(xb, p['gla_w_proj'][j], tm, 1280)
            if is_prompt:
                og, st = _gla_core(proj, p['gla_wg2p'][j], _row(p['gla_b_g'][j]), _row(p['gla_norm_w'][j]),
                                   None, batch, seq, 256, GLA_CHUNK, BF16)
            else:
                og, st = _gla_core(proj, p['gla_wg2p'][j], _row(p['gla_b_g'][j]), _row(p['gla_norm_w'][j]),
                                   state_gla[j], batch, seq, seq, min(GLA_CHUNK, seq), F32)
            gla_states.append(st)
            xf, xb = _out_ln(og, p['gla_w_out'][j], None, xf, ln_g, ln_b, tm_small, "gla_out")
        elif kind == 1:
            if is_prompt:
                tables = _rope_tables(jnp.arange(seq))
            else:
                tables = tuple(jnp.tile(t, (batch, 1)) for t in _rope_tables(PAST_LEN + jnp.arange(seq)))
            qkv = _swa_qkv(xb, p['swa_w_qkv'][j], _row(p['swa_b_qkv'][j]), tables, seq, tm, 512)
            sinks = p['swa_sinks'][j].reshape(-1)
            if is_prompt:
                oa = _swa_prompt_attn(qkv, sinks, batch, seq)
                kv = qkv.reshape(batch, seq, -1)[:, seq - WINDOW:, SWA_QD:]
                swa_k.append(kv[..., :SWA_KD].reshape(batch, WINDOW, SWA_KV_HEADS, SWA_HEAD_DIM))
                swa_v.append(kv[..., SWA_KD:].reshape(batch, WINDOW, SWA_KV_HEADS, SWA_HEAD_DIM))
            else:
                nbuf = cache_k.shape[2]
                oa, ko, vo = _swa_sample_attn(qkv, sinks, cache_k[j].reshape(batch, nbuf, SWA_KD),
                                              cache_v[j].reshape(batch, nbuf, SWA_KD), batch, seq)
                swa_k.append(ko.reshape(batch, nbuf, SWA_KV_HEADS, SWA_HEAD_DIM))
                swa_v.append(vo.reshape(batch, nbuf, SWA_KV_HEADS, SWA_HEAD_DIM))
            xf, xb = _out_ln(oa, p['swa_w_out'][j], _row(p['swa_b_out'][j]), xf, ln_g, ln_b, tm_small, "swa_out")
        else:
            uv = _sg_in(xb, p['sg_w_in'][j], _row(p['sg_b_in'][j]), _row(p['sg_ln_g'][j]), _row(p['sg_ln_b'][j]), tm)
            if is_prompt:
                chunk = period = SG_CHUNK
                ws = p['sg_w_s'][j]
                bs_col = p['sg_b_s'][j].T
            else:
                chunk, period = m, seq
                ws = jnp.tile(p['sg_w_s'][j][:, :seq, :seq], (1, batch, batch))
                bs_col = jnp.tile(p['sg_b_s'][j][:, :seq].T, (batch, 1))
                sg_v.append(uv[1].reshape(batch, seq, SG_WIDTH))
            xf, xb = _sg_out(uv, ws, bs_col, p['sg_w_out'][j], _row(p['sg_b_out'][j]), xf, ln_g, ln_b,
                             tm_small, chunk, period)
        if is_prompt:
            u, gl = _ffn_a_prompt(xb, p['ffn_w_in'][i], p['ffn_conv_w'][i], _row(p['ffn_conv_b'][i]),
                                  batch, seq, tm, 512)
            conv_states.append(gl.reshape(batch, 8, D_FF)[:, 8 - (CONV_W - 1):])
        else:
            st = state_conv[i]
            prev1 = jnp.pad(st[:, 1:2], ((0, 0), (0, seq - 1), (0, 0))).reshape(m, D_FF)
            prev2 = jnp.pad(st, ((0, 0), (0, seq - 2), (0, 0))).reshape(m, D_FF)
            u, gl = _ffn_a_sample(xb, p['ffn_w_in'][i], p['ffn_conv_w'][i], _row(p['ffn_conv_b'][i]),
                                  prev1, prev2, seq, 512)
            conv_states.append(gl.reshape(batch, seq, D_FF)[:, seq - (CONV_W - 1):])
        xf, xb = _out_ln(u, p['ffn_w_out'][i], None, xf, _row(p['ln_ffn_g'][i]), _row(p['ln_ffn_b'][i]),
                         tm_small, "ffn_out")
    return xf.reshape(batch, seq, D_MODEL), gla_states, swa_k, swa_v, sg_v, conv_states


def kernel(x_prompt, x_sample, state_gla, cache_swa_k, cache_swa_v, state_ffn_conv, ln_mix_g, ln_mix_b, ln_ffn_g, ln_ffn_b, gla_w_in, gla_w_g2, gla_b_g, gla_norm_w, gla_w_out, swa_w_qkv, swa_b_qkv, swa_sinks, swa_w_out, swa_b_out, sg_w_in, sg_b_in, sg_ln_g, sg_ln_b, sg_w_s, sg_b_s, sg_w_out, sg_b_out, ffn_w_in, ffn_conv_w, ffn_conv_b, ffn_w_out):
    n_gla = gla_w_in.shape[0]
    gla_w_proj = jnp.pad(gla_w_in.astype(BF16), ((0, 0), (0, 0), (0, GLA_PROJ_PAD - gla_w_in.shape[2])))
    gla_wg2p = jnp.pad(gla_w_g2.astype(BF16), ((0, 0), (0, GLA_GLOW_PAD - GLA_RANK), (0, 0)))
    p = dict(
        ln_mix_g=ln_mix_g, ln_mix_b=ln_mix_b, ln_ffn_g=ln_ffn_g, ln_ffn_b=ln_ffn_b,
        gla_w_proj=gla_w_proj, gla_wg2p=gla_wg2p, gla_b_g=gla_b_g, gla_norm_w=gla_norm_w,
        gla_w_out=gla_w_out.astype(BF16),
        swa_w_qkv=swa_w_qkv.astype(BF16), swa_b_qkv=swa_b_qkv, swa_sinks=swa_sinks,
        swa_w_out=swa_w_out.astype(BF16), swa_b_out=swa_b_out,
        sg_w_in=sg_w_in.astype(BF16), sg_b_in=sg_b_in, sg_ln_g=sg_ln_g, sg_ln_b=sg_ln_b,
        sg_w_s=sg_w_s, sg_b_s=sg_b_s, sg_w_out=sg_w_out.astype(BF16), sg_b_out=sg_b_out,
        ffn_w_in=ffn_w_in.astype(BF16), ffn_conv_w=ffn_conv_w, ffn_conv_b=ffn_conv_b,
        ffn_w_out=ffn_w_out.astype(BF16),
    )
    bp, lp, _ = x_prompt.shape
    bs, ls, _ = x_sample.shape
    yp, gla_p, swk_p, swv_p, _, conv_p = _trunk(x_prompt, bp, lp, True, None, None, None, None, p)
    ys, gla_s, swk_s, swv_s, sgv_s, conv_s = _trunk(x_sample, bs, ls, False, state_gla, cache_swa_k,
                                                    cache_swa_v, state_ffn_conv, p)
    return (yp, ys, jnp.stack(gla_p), jnp.stack(gla_s), jnp.stack(swk_p), jnp.stack(swv_p),
            jnp.stack(swk_s), jnp.stack(swv_s), jnp.stack(sgv_s), jnp.stack(conv_p), jnp.stack(conv_s))
```

```python
import functools

import jax
import jax.numpy as jnp
from jax import lax
from jax.experimental import pallas as pl
from jax.experimental.pallas import tpu as pltpu

F32 = jnp.float32
BF16 = jnp.bfloat16

D_MODEL = 2048
DEPTH = 4
PAST_LEN = 16384
N_MIXERS = 3
ALPHA = (2 * DEPTH) ** 0.25
LN_EPS = 1e-5

GLA_HEADS = 4
GLA_DK = 256
GLA_DV = 512
GLA_RANK = 16
GLA_TAU = 16.0
GLA_CHUNK = 64
GLA_QK = GLA_HEADS * GLA_DK
GLA_VD = GLA_HEADS * GLA_DV
GLA_PROJ = 2 * GLA_QK + 2 * GLA_VD
GLA_IN = GLA_PROJ + GLA_RANK
GLA_GLOW_PAD = 128
GLA_PROJ_TN = 1280
GLA_PROJ_PAD = 5 * GLA_PROJ_TN

SWA_HEAD_DIM = 64
SWA_Q_HEADS = 32
SWA_KV_HEADS = 8
SWA_GROUP = 4
WINDOW = 128
ROT_DIM = 16
ROPE_THETA = 500000.0
SWA_QD = SWA_Q_HEADS * SWA_HEAD_DIM
SWA_KD = SWA_KV_HEADS * SWA_HEAD_DIM

SG_WIDTH = 2048
SG_GROUPS = 4
SG_GW = SG_WIDTH // SG_GROUPS
SG_CHUNK = 128

D_FF = 5632
CONV_W = 3

TM = 512
MXU_COLS = 256
ROW_PIECE = 512
VMEM_LIMIT_BYTES = 56 * 1024 * 1024


def _cparams(*sem):
    return pltpu.CompilerParams(dimension_semantics=sem, vmem_limit_bytes=VMEM_LIMIT_BYTES)


def _dot(a, b):
    return jnp.dot(a, b, preferred_element_type=F32)


def _dot_nt(a, b):
    return lax.dot_general(a, b, (((1,), (1,)), ((), ())), preferred_element_type=F32)


def _dot_tn(a, b):
    return lax.dot_general(a, b, (((0,), (0,)), ((), ())), preferred_element_type=F32)


def _layer_norm(x, g, b):
    mu = jnp.mean(x, -1, keepdims=True)
    xc = x - mu
    var = jnp.mean(xc * xc, -1, keepdims=True)
    return xc * lax.rsqrt(var + LN_EPS) * g + b


def _gelu(x):
    return 0.5 * x * (1.0 + lax.erf(x * (0.5 ** 0.5)))


def _log_sigmoid(x):
    return jnp.minimum(x, 0.0) - jnp.log1p(jnp.exp(-jnp.abs(x)))


def _cumsum_rows(x, period):
    rowmod = lax.broadcasted_iota(jnp.int32, x.shape, 0) % period
    s = 1
    while s < period:
        x = x + jnp.where(rowmod >= s, pltpu.roll(x, s, axis=0), 0.0)
        s *= 2
    return x


def _single_buffered(shape, index_map):
    return pl.BlockSpec(shape, index_map, pipeline_mode=pl.Buffered(1))


def _row(v):
    return v.reshape(1, -1)


def _gla_proj_kernel(x_ref, w_ref, o_ref, wb_ref, *, tn, n_valid):
    @pl.when(pl.program_id(1) == 0)
    def _():
        w = w_ref[...]
        col = lax.broadcasted_iota(jnp.int32, w.shape, 1) + pl.program_id(0) * tn
        wb_ref[...] = jnp.where(col < n_valid, w, 0.0).astype(BF16)

    o_ref[...] = _dot(x_ref[...], wb_ref[...])


def _gla_proj(xb, w_in, layer):
    m, k = xb.shape
    tn = GLA_PROJ_TN
    return pl.pallas_call(
        functools.partial(_gla_proj_kernel, tn=tn, n_valid=GLA_IN),
        grid=(GLA_PROJ_PAD // tn, m // TM),
        in_specs=[pl.BlockSpec((TM, k), lambda j, i: (i, 0)),
                  pl.BlockSpec((None, k, tn), lambda j, i: (layer, 0, j))],
        out_specs=pl.BlockSpec((TM, tn), lambda j, i: (i, j)),
        out_shape=jax.ShapeDtypeStruct((m, GLA_PROJ_PAD), F32),
        scratch_shapes=[pltpu.VMEM((k, tn), BF16)],
        compiler_params=_cparams("arbitrary", "arbitrary"),
        name="gla_proj",
    )(xb, w_in)


def _gla_gate_and_decay(q_ref, k_ref, gl_ref, wg2_ref, bg_ref, period):
    ga = _dot(gl_ref[...].astype(BF16), wg2_ref[...]) + bg_ref[...]
    cum = _cumsum_rows(_log_sigmoid(ga) * (1.0 / GLA_TAU), period)
    q = q_ref[...] * (GLA_DK ** -0.5)
    k = k_ref[...]
    q_dec = (q * jnp.exp(cum)).astype(BF16)
    k_inv = (k * jnp.exp(-cum)).astype(BF16)
    return cum, k, q_dec, k_inv


def _gla_finish(o, r, nw_ref):
    o = o * lax.rsqrt(jnp.mean(o * o, -1, keepdims=True) + LN_EPS) * nw_ref[...]
    return o * (r * jax.nn.sigmoid(r))


def _causal(n):
    r = lax.broadcasted_iota(jnp.int32, (n, n), 0)
    c = lax.broadcasted_iota(jnp.int32, (n, n), 1)
    return c <= r


def _gla_prompt_kernel(q_ref, k_ref, v_ref, r_ref, gl_ref, wg2_ref, bg_ref, nw_ref,
                       o_ref, sout_ref, st_ref, *, t_rows, chunk):
    t = pl.program_id(2)

    @pl.when(t == 0)
    def _():
        st_ref[...] = jnp.zeros_like(st_ref)

    cum, k, q_dec, k_inv = _gla_gate_and_decay(q_ref, k_ref, gl_ref, wg2_ref, bg_ref, chunk)
    v = v_ref[...].astype(BF16)
    causal = _causal(chunk)
    outs = []
    for c in range(t_rows // chunk):
        lo, hi = c * chunk, (c + 1) * chunk
        cum_c = cum[lo:hi]
        last = cum_c[chunk - 1:chunk, :]
        k_last = (k[lo:hi] * jnp.exp(last - cum_c)).astype(BF16)
        qd, ki, vc = q_dec[lo:hi], k_inv[lo:hi], v[lo:hi]
        attn = jnp.where(causal, _dot_nt(qd, ki), 0.0).astype(BF16)
        st = st_ref[...]
        outs.append(_dot(attn, vc) + _dot_nt(qd, st.astype(BF16)))
        st_ref[...] = st * jnp.exp(last) + _dot_tn(vc, k_last)
    o = jnp.concatenate(outs, axis=0)
    o_ref[...] = _gla_finish(o, r_ref[...], nw_ref).astype(o_ref.dtype)

    @pl.when(t == pl.num_programs(2) - 1)
    def _():
        sout_ref[0, 0] = st_ref[...].T


def _gla_prompt(proj, wg2p, bg, norm_w, batch, seq, t_rows):
    nt = seq // t_rows
    row = lambda b, h, t: b * nt + t
    in_specs = [
        pl.BlockSpec((t_rows, GLA_DK), lambda b, h, t: (row(b, h, t), h)),
        pl.BlockSpec((t_rows, GLA_DK), lambda b, h, t: (row(b, h, t), GLA_HEADS + h)),
        pl.BlockSpec((t_rows, GLA_DV), lambda b, h, t: (row(b, h, t), 2 * GLA_QK // GLA_DV + h)),
        pl.BlockSpec((t_rows, GLA_DV), lambda b, h, t: (row(b, h, t), (2 * GLA_QK + GLA_VD) // GLA_DV + h)),
        pl.BlockSpec((t_rows, GLA_GLOW_PAD), lambda b, h, t: (row(b, h, t), GLA_PROJ // GLA_GLOW_PAD)),
        pl.BlockSpec((GLA_GLOW_PAD, GLA_DK), lambda b, h, t: (0, h)),
        pl.BlockSpec((1, GLA_DK), lambda b, h, t: (0, h)),
        pl.BlockSpec((1, GLA_DV), lambda b, h, t: (0, 0)),
    ]
    return pl.pallas_call(
        functools.partial(_gla_prompt_kernel, t_rows=t_rows, chunk=GLA_CHUNK),
        grid=(batch, GLA_HEADS, nt),
        in_specs=in_specs,
        out_specs=[pl.BlockSpec((t_rows, GLA_DV), lambda b, h, t: (row(b, h, t), h)),
                   pl.BlockSpec((1, 1, GLA_DK, GLA_DV), lambda b, h, t: (b, h, 0, 0))],
        out_shape=[jax.ShapeDtypeStruct((batch * seq, GLA_VD), BF16),
                   jax.ShapeDtypeStruct((batch, GLA_HEADS, GLA_DK, GLA_DV), F32)],
        scratch_shapes=[pltpu.VMEM((GLA_DV, GLA_DK), F32)],
        compiler_params=_cparams("arbitrary", "arbitrary", "arbitrary"),
        name="gla_prompt",
    )(proj, proj, proj, proj, proj, wg2p, bg, norm_w)


def _gla_sample_kernel(q_ref, k_ref, v_ref, r_ref, gl_ref, wg2_ref, bg_ref, nw_ref, s0_ref,
                       o_ref, sout_ref, *, seq, nb, n_real):
    g = pl.program_id(1)

    @pl.when(g < n_real)
    def _():
        cum, k, q_dec, k_inv = _gla_gate_and_decay(q_ref, k_ref, gl_ref, wg2_ref, bg_ref, seq)
        v = v_ref[...].astype(BF16)
        causal = _causal(seq)
        outs = []
        for b in range(nb):
            lo, hi = b * seq, (b + 1) * seq
            cum_b = cum[lo:hi]
            last = cum_b[seq - 1:seq, :]
            k_last = (k[lo:hi] * jnp.exp(last - cum_b)).astype(BF16)
            qd, ki, vc = q_dec[lo:hi], k_inv[lo:hi], v[lo:hi]
            attn = jnp.where(causal, _dot_nt(qd, ki), 0.0).astype(BF16)
            s = s0_ref[b]
            outs.append(_dot(attn, vc) + _dot(qd, s.astype(BF16)))
            decay = jnp.transpose(jnp.broadcast_to(jnp.exp(last), (128, GLA_DK)))[:, 0:1]
            sout_ref[b] = s * decay + _dot_tn(k_last, vc)
        o = jnp.concatenate(outs, axis=0)
        o_ref[...] = _gla_finish(o, r_ref[...], nw_ref)

    @pl.when(g >= n_real)
    def _():
        o_ref[...] = jnp.zeros_like(o_ref)


def _gla_sample(proj, row0, wg2p, bg, norm_w, state, layer, batch, seq, nb):
    rows = nb * seq
    n_real = batch // nb
    blk0 = row0 // rows
    rb = lambda g: blk0 + jnp.minimum(g, n_real - 1)
    sb = lambda g: jnp.minimum(g, n_real - 1)
    in_specs = [
        pl.BlockSpec((rows, GLA_DK), lambda h, g: (rb(g), h)),
        pl.BlockSpec((rows, GLA_DK), lambda h, g: (rb(g), GLA_HEADS + h)),
        pl.BlockSpec((rows, GLA_DV), lambda h, g: (rb(g), 2 * GLA_QK // GLA_DV + h)),
        pl.BlockSpec((rows, GLA_DV), lambda h, g: (rb(g), (2 * GLA_QK + GLA_VD) // GLA_DV + h)),
        pl.BlockSpec((rows, GLA_GLOW_PAD), lambda h, g: (rb(g), GLA_PROJ // GLA_GLOW_PAD)),
        pl.BlockSpec((GLA_GLOW_PAD, GLA_DK), lambda h, g: (0, h)),
        pl.BlockSpec((1, GLA_DK), lambda h, g: (0, h)),
        pl.BlockSpec((1, GLA_DV), lambda h, g: (0, 0)),
        pl.BlockSpec((None, nb, None, GLA_DK, GLA_DV), lambda h, g: (layer, sb(g), h, 0, 0)),
    ]
    return pl.pallas_call(
        functools.partial(_gla_sample_kernel, seq=seq, nb=nb, n_real=n_real),
        grid=(GLA_HEADS, TM // rows),
        in_specs=in_specs,
        out_specs=[pl.BlockSpec((rows, GLA_DV), lambda h, g: (g, h)),
                   pl.BlockSpec((nb, None, GLA_DK, GLA_DV), lambda h, g: (sb(g), h, 0, 0))],
        out_shape=[jax.ShapeDtypeStruct((TM, GLA_VD), F32),
                   jax.ShapeDtypeStruct((batch, GLA_HEADS, GLA_DK, GLA_DV), F32)],
        compiler_params=_cparams("arbitrary", "arbitrary"),
        name="gla_sample",
    )(proj, proj, proj, proj, proj, wg2p, bg, norm_w, state)


def _residual_ln_store(y, res_ref, g_ref, b_ref, of_ref, ob_ref):
    o = _layer_norm(ALPHA * res_ref[...] + y, g_ref[...], b_ref[...])
    of_ref[...] = o
    ob_ref[...] = o.astype(BF16)


def _mixer_out_kernel(*refs, has_bias, n_prompt_tiles):
    if has_bias:
        ap_ref, as_ref, w_ref, bias_ref, res_ref, g_ref, b_ref, of_ref, ob_ref, wb_ref, a_ref = refs
    else:
        ap_ref, as_ref, w_ref, res_ref, g_ref, b_ref, of_ref, ob_ref, wb_ref, a_ref = refs
    i = pl.program_id(0)

    @pl.when(i == 0)
    def _():
        wb_ref[...] = w_ref[...].astype(BF16)

    @pl.when(i < n_prompt_tiles)
    def _():
        a_ref[...] = ap_ref[...]

    @pl.when(i >= n_prompt_tiles)
    def _():
        a_ref[...] = as_ref[...].astype(BF16)

    y = _dot(a_ref[...], wb_ref[...])
    if has_bias:
        y = y + bias_ref[...]
    _residual_ln_store(y, res_ref, g_ref, b_ref, of_ref, ob_ref)


def _mixer_out(a_prompt, a_sample, w, layer, bias, res, g, b, tm, name):
    m = res.shape[0]
    k = a_prompt.shape[1]
    n_prompt_tiles = a_prompt.shape[0] // tm
    sample_blocks = a_sample.shape[0] // tm
    has_bias = bias is not None
    vec = pl.BlockSpec((1, D_MODEL), lambda i: (0, 0))
    in_specs = [pl.BlockSpec((tm, k), lambda i: (jnp.minimum(i, n_prompt_tiles - 1), 0)),
                pl.BlockSpec((tm, k), lambda i: (jnp.clip(i - n_prompt_tiles, 0, sample_blocks - 1), 0)),
                _single_buffered((None, k, D_MODEL), lambda i: (layer, 0, 0))]
    args = [a_prompt, a_sample, w]
    if has_bias:
        in_specs.append(vec)
        args.append(bias)
    in_specs += [pl.BlockSpec((tm, D_MODEL), lambda i: (i, 0)), vec, vec]
    args += [res, g, b]
    return pl.pallas_call(
        functools.partial(_mixer_out_kernel, has_bias=has_bias, n_prompt_tiles=n_prompt_tiles),
        grid=(m // tm,),
        in_specs=in_specs,
        out_specs=[pl.BlockSpec((tm, D_MODEL), lambda i: (i, 0)),
                   pl.BlockSpec((tm, D_MODEL), lambda i: (i, 0))],
        out_shape=[jax.ShapeDtypeStruct((m, D_MODEL), F32),
                   jax.ShapeDtypeStruct((m, D_MODEL), BF16)],
        scratch_shapes=[pltpu.VMEM((k, D_MODEL), BF16), pltpu.VMEM((tm, k), BF16)],
        compiler_params=_cparams("arbitrary"),
        name=name,
    )(*args)


def _ffn_out_kernel(a_ref, w_ref, res_ref, g_ref, b_ref, of_ref, ob_ref):
    _residual_ln_store(_dot(a_ref[...], w_ref[...]), res_ref, g_ref, b_ref, of_ref, ob_ref)


def _ffn_out(a, w, res, g, b, tm):
    m, k = a.shape
    vec = pl.BlockSpec((1, D_MODEL), lambda i: (0, 0))
    return pl.pallas_call(
        _ffn_out_kernel,
        grid=(m // tm,),
        in_specs=[pl.BlockSpec((tm, k), lambda i: (i, 0)),
                  _single_buffered((k, D_MODEL), lambda i: (0, 0)),
                  pl.BlockSpec((tm, D_MODEL), lambda i: (i, 0)), vec, vec],
        out_specs=[pl.BlockSpec((tm, D_MODEL), lambda i: (i, 0)),
                   pl.BlockSpec((tm, D_MODEL), lambda i: (i, 0))],
        out_shape=[jax.ShapeDtypeStruct((m, D_MODEL), F32),
                   jax.ShapeDtypeStruct((m, D_MODEL), BF16)],
        compiler_params=_cparams("arbitrary"),
        name="ffn_out",
    )(a, w, res, g, b)


def _conv_glu(g, val, g1, g2, cw, cb):
    conv = cb + cw[0:1, :] * g2
    conv = conv + cw[1:2, :] * g1
    conv = conv + cw[2:3, :] * g
    return (_gelu(conv) * val).astype(BF16)


def _ffn_in_kernel(x_ref, wg_ref, wv_ref, cw_ref, cb_ref, p1_ref, p2_ref, u_ref, glp_ref, gls_ref,
                   wgb_ref, wvb_ref, carry_ref, *, seq_tiles, n_prompt_tiles, sample_seq, sample_rows):
    i = pl.program_id(1)
    tm, tn = u_ref.shape
    col_slices = [slice(c0, c0 + MXU_COLS) for c0 in range(0, tn, MXU_COLS)]
    row_slices = [slice(r0, r0 + ROW_PIECE) for r0 in range(0, tm, ROW_PIECE)]
    row = lax.broadcasted_iota(jnp.int32, (ROW_PIECE, MXU_COLS), 0)

    @pl.when(i == 0)
    def _():
        wgb_ref[...] = wg_ref[...].astype(BF16)
        wvb_ref[...] = wv_ref[...].astype(BF16)

    @pl.when((i % seq_tiles == 0) & (i < n_prompt_tiles))
    def _():
        carry_ref[...] = jnp.zeros_like(carry_ref)

    @pl.when(i < n_prompt_tiles)
    def _():
        for cs in col_slices:
            above = carry_ref[:, cs]
            for rs in row_slices:
                x = x_ref[rs, :]
                g = _dot(x, wgb_ref[:, cs])
                val = _dot(x, wvb_ref[:, cs])
                g1 = jnp.where(row >= 1, pltpu.roll(g, 1, axis=0), above[7:8, :])
                g2 = jnp.where(row >= 2, pltpu.roll(g, 2, axis=0),
                               jnp.where(row == 0, above[6:7, :], above[7:8, :]))
                u_ref[rs, cs] = _conv_glu(g, val, g1, g2, cw_ref[:, cs], cb_ref[:, cs])
                above = g[ROW_PIECE - 8:ROW_PIECE, :]
            carry_ref[:, cs] = above
            glp_ref[:, cs] = above

    @pl.when(i >= n_prompt_tiles)
    def _():
        pos = row % sample_seq
        for cs in col_slices:
            for rs in row_slices:
                if rs.start < sample_rows:
                    x = x_ref[rs, :]
                    g = _dot(x, wgb_ref[:, cs])
                    val = _dot(x, wvb_ref[:, cs])
                    g1 = jnp.where(pos >= 1, pltpu.roll(g, 1, axis=0), p1_ref[rs, cs])
                    g2 = jnp.where(pos >= 2, pltpu.roll(g, 2, axis=0), p2_ref[rs, cs])
                    u_ref[rs, cs] = _conv_glu(g, val, g1, g2, cw_ref[:, cs], cb_ref[:, cs])
                    gls_ref[rs, cs] = g
                else:
                    u_ref[rs, cs] = jnp.zeros((ROW_PIECE, MXU_COLS), BF16)
                    gls_ref[rs, cs] = jnp.zeros((ROW_PIECE, MXU_COLS), F32)


def _ffn_in(xb, w_in, layer, conv_w, conv_b, prev1, prev2, n_prompt_seqs, seq, sample_seq, sample_rows, tn):
    m = xb.shape[0]
    nj = D_FF // tn
    seq_tiles = seq // TM
    n_prompt_tiles = n_prompt_seqs * seq_tiles
    return pl.pallas_call(
        functools.partial(_ffn_in_kernel, seq_tiles=seq_tiles, n_prompt_tiles=n_prompt_tiles,
                          sample_seq=sample_seq, sample_rows=sample_rows),
        grid=(nj, m // TM),
        in_specs=[pl.BlockSpec((TM, D_MODEL), lambda j, i: (i, 0)),
                  pl.BlockSpec((None, D_MODEL, tn), lambda j, i: (layer, 0, j)),
                  pl.BlockSpec((None, D_MODEL, tn), lambda j, i: (layer, 0, nj + j)),
                  pl.BlockSpec((None, CONV_W, tn), lambda j, i: (layer, 0, j)),
                  pl.BlockSpec((None, 1, tn), lambda j, i: (layer, 0, j)),
                  pl.BlockSpec((TM, tn), lambda j, i: (0, j)),
                  pl.BlockSpec((TM, tn), lambda j, i: (0, j))],
        out_specs=[pl.BlockSpec((TM, tn), lambda j, i: (i, j)),
                   pl.BlockSpec((8, tn), lambda j, i: (jnp.minimum(i, n_prompt_tiles - 1) // seq_tiles, j)),
                   pl.BlockSpec((TM, tn), lambda j, i: (0, j))],
        out_shape=[jax.ShapeDtypeStruct((m, D_FF), BF16),
                   jax.ShapeDtypeStruct((n_prompt_seqs * 8, D_FF), F32),
                   jax.ShapeDtypeStruct((TM, D_FF), F32)],
        scratch_shapes=[pltpu.VMEM((D_MODEL, tn), BF16), pltpu.VMEM((D_MODEL, tn), BF16),
                        pltpu.VMEM((8, tn), F32)],
        compiler_params=_cparams("arbitrary", "arbitrary"),
        name="ffn_in",
    )(xb, w_in, w_in, conv_w, conv_b, prev1, prev2)


def _swa_qkv_kernel(x_ref, w_ref, b_ref, ca_ref, cm_ref, cp_ref, o_ref, wb_ref, *, n_rot_tiles):
    j = pl.program_id(0)
    tm, tn = o_ref.shape

    @pl.when(pl.program_id(1) == 0)
    def _():
        wb_ref[...] = w_ref[...].astype(BF16)

    x = x_ref[...]
    reps = MXU_COLS // ca_ref.shape[1]
    ca = jnp.concatenate([ca_ref[...]] * reps, axis=1)
    cm = jnp.concatenate([cm_ref[...]] * reps, axis=1)
    cp = jnp.concatenate([cp_ref[...]] * reps, axis=1)
    lane = lax.broadcasted_iota(jnp.int32, (tm, MXU_COLS), 1) % SWA_HEAD_DIM
    rotated = lane < jnp.where(j < n_rot_tiles, ROT_DIM, 0)
    half = ROT_DIM // 2
    for c0 in range(0, tn, MXU_COLS):
        sl = slice(c0, c0 + MXU_COLS)
        y = _dot(x, wb_ref[:, sl]) + b_ref[:, sl]
        rot = y * ca + pltpu.roll(y, MXU_COLS - half, axis=1) * cm + pltpu.roll(y, half, axis=1) * cp
        o_ref[:, sl] = jnp.where(rotated, rot, y)


def _rope_tables(pos):
    half = ROT_DIM // 2
    inv = ROPE_THETA ** (-jnp.arange(half, dtype=F32) / half)
    ang = pos.astype(F32)[:, None] * inv
    cos, sin = jnp.cos(ang), jnp.sin(ang)
    n = pos.shape[0]
    rest = SWA_HEAD_DIM - ROT_DIM
    ca = jnp.concatenate([cos, cos, jnp.ones((n, rest), F32)], -1)
    cm = jnp.concatenate([-sin, jnp.zeros((n, half + rest), F32)], -1)
    cp = jnp.concatenate([jnp.zeros((n, half), F32), sin, jnp.zeros((n, rest), F32)], -1)
    return tuple(jnp.tile(t, (1, 128 // SWA_HEAD_DIM)) for t in (ca, cm, cp))


def _swa_qkv(xb, w, layer, b, tables, n_prompt_tiles, seq, tn):
    m = xb.shape[0]
    n = w.shape[2]
    pos_tiles = seq // TM
    tab = pl.BlockSpec((TM, 128), lambda j, i: (jnp.where(i < n_prompt_tiles, i % pos_tiles, pos_tiles), 0))
    return pl.pallas_call(
        functools.partial(_swa_qkv_kernel, n_rot_tiles=(SWA_QD + SWA_KD) // tn),
        grid=(n // tn, m // TM),
        in_specs=[pl.BlockSpec((TM, D_MODEL), lambda j, i: (i, 0)),
                  pl.BlockSpec((None, D_MODEL, tn), lambda j, i: (layer, 0, j)),
                  pl.BlockSpec((1, tn), lambda j, i: (0, j)),
                  tab, tab, tab],
        out_specs=pl.BlockSpec((TM, tn), lambda j, i: (i, j)),
        out_shape=jax.ShapeDtypeStruct((m, n), F32),
        scratch_shapes=[pltpu.VMEM((D_MODEL, tn), BF16)],
        compiler_params=_cparams("arbitrary", "arbitrary"),
        name="swa_qkv",
    )(xb, w, b, *tables)


def _sink_softmax_pv(s, mask, sink, v):
    s = jnp.where(mask, s * (SWA_HEAD_DIM ** -0.5), -jnp.inf)
    mx = jnp.maximum(jnp.max(s, -1, keepdims=True), sink)
    p = jnp.exp(s - mx)
    denom = jnp.sum(p, -1, keepdims=True) + jnp.exp(sink - mx)
    return _dot((p / denom).astype(BF16), v)


def _swa_prompt_kernel(sink_ref, q_ref, kp_ref, kc_ref, vp_ref, vc_ref, o_ref):
    i = pl.program_id(1)
    q = q_ref[...].astype(BF16)
    kb = jnp.concatenate([kp_ref[...], kc_ref[...]], axis=0).astype(BF16)
    vb = jnp.concatenate([vp_ref[...], vc_ref[...]], axis=0).astype(BF16)
    r = lax.broadcasted_iota(jnp.int32, (WINDOW, 2 * WINDOW), 0)
    c = lax.broadcasted_iota(jnp.int32, (WINDOW, 2 * WINDOW), 1)
    mask = (c > r) & (c <= r + WINDOW) & (c >= jnp.where(i > 0, 0, WINDOW))
    pieces = []
    for h in range(SWA_KV_HEADS):
        kh = kb[:, h * SWA_HEAD_DIM:(h + 1) * SWA_HEAD_DIM]
        vh = vb[:, h * SWA_HEAD_DIM:(h + 1) * SWA_HEAD_DIM]
        for g in range(SWA_GROUP):
            hq = h * SWA_GROUP + g
            qg = q[:, hq * SWA_HEAD_DIM:(hq + 1) * SWA_HEAD_DIM]
            pieces.append(_sink_softmax_pv(_dot_nt(qg, kh), mask, sink_ref[hq], vh))
    o_ref[...] = jnp.concatenate(pieces, axis=-1).astype(o_ref.dtype)


def _swa_prompt_attn(qkv, sinks, batch, seq):
    nb = seq // WINDOW
    kcol = SWA_QD // SWA_KD
    cur = lambda b, i: b * nb + i
    prev = lambda b, i: b * nb + jnp.maximum(i - 1, 0)
    return pl.pallas_call(
        _swa_prompt_kernel,
        grid=(batch, nb),
        in_specs=[pl.BlockSpec(memory_space=pltpu.SMEM),
                  pl.BlockSpec((WINDOW, SWA_QD), lambda b, i: (cur(b, i), 0)),
                  pl.BlockSpec((WINDOW, SWA_KD), lambda b, i: (prev(b, i), kcol)),
                  pl.BlockSpec((WINDOW, SWA_KD), lambda b, i: (cur(b, i), kcol)),
                  pl.BlockSpec((WINDOW, SWA_KD), lambda b, i: (prev(b, i), kcol + 1)),
                  pl.BlockSpec((WINDOW, SWA_KD), lambda b, i: (cur(b, i), kcol + 1))],
        out_specs=pl.BlockSpec((WINDOW, SWA_QD), lambda b, i: (cur(b, i), 0)),
        out_shape=jax.ShapeDtypeStruct((batch * seq, SWA_QD), BF16),
        compiler_params=_cparams("arbitrary", "arbitrary"),
        name="swa_prompt_attn",
    )(sinks, qkv, qkv, qkv, qkv, qkv)


def _swa_sample_kernel(sink_ref, q_ref, kn_ref, vn_ref, kc_ref, vc_ref, o_ref, ko_ref, vo_ref,
                       *, seq, nb, n_real):
    step = pl.program_id(0)
    nbuf = kc_ref.shape[1]

    @pl.when(step < n_real)
    def _():
        rows = SWA_GROUP * seq
        l = lax.broadcasted_iota(jnp.int32, (rows, nbuf + seq), 0) % seq
        c = lax.broadcasted_iota(jnp.int32, (rows, nbuf + seq), 1)
        diff = l + nbuf - c
        mask = (diff >= 0) & (diff < WINDOW)
        grp = lax.broadcasted_iota(jnp.int32, (rows, 1), 0) // seq
        for b in range(nb):
            lo, hi = b * seq, (b + 1) * seq
            k_all = jnp.concatenate([kc_ref[b], kn_ref[lo:hi, :]], axis=0)
            v_all = jnp.concatenate([vc_ref[b], vn_ref[lo:hi, :]], axis=0)
            ko_ref[b] = k_all[seq:, :]
            vo_ref[b] = v_all[seq:, :]
            kb = k_all.astype(BF16)
            vb = v_all.astype(BF16)
            q = q_ref[lo:hi, :]
            pieces = []
            for h in range(SWA_KV_HEADS):
                kh = kb[:, h * SWA_HEAD_DIM:(h + 1) * SWA_HEAD_DIM]
                vh = vb[:, h * SWA_HEAD_DIM:(h + 1) * SWA_HEAD_DIM]
                qs = jnp.concatenate(
                    [q[:, (h * SWA_GROUP + g) * SWA_HEAD_DIM:(h * SWA_GROUP + g + 1) * SWA_HEAD_DIM]
                     for g in range(SWA_GROUP)], axis=0).astype(BF16)
                sink = jnp.zeros((rows, 1), F32)
                for g in range(SWA_GROUP):
                    sink = jnp.where(grp == g, sink_ref[h * SWA_GROUP + g], sink)
                o = _sink_softmax_pv(_dot_nt(qs, kh), mask, sink, vh)
                pieces += [o[g * seq:(g + 1) * seq, :] for g in range(SWA_GROUP)]
            o_ref[lo:hi, :] = jnp.concatenate(pieces, axis=-1)

    @pl.when(step >= n_real)
    def _():
        o_ref[...] = jnp.zeros_like(o_ref)


def _swa_sample_attn(qkv, row0, sinks, k_cache, v_cache, batch, seq, nb):
    nbuf = k_cache.shape[1]
    kcol = SWA_QD // SWA_KD
    rows = nb * seq
    n_real = batch // nb
    blk0 = row0 // rows
    rb = lambda s: blk0 + jnp.minimum(s, n_real - 1)
    cache = pl.BlockSpec((nb, nbuf, SWA_KD), lambda s: (jnp.minimum(s, n_real - 1), 0, 0))
    return pl.pallas_call(
        functools.partial(_swa_sample_kernel, seq=seq, nb=nb, n_real=n_real),
        grid=(TM // rows,),
        in_specs=[pl.BlockSpec(memory_space=pltpu.SMEM),
                  pl.BlockSpec((rows, SWA_QD), lambda s: (rb(s), 0)),
                  pl.BlockSpec((rows, SWA_KD), lambda s: (rb(s), kcol)),
                  pl.BlockSpec((rows, SWA_KD), lambda s: (rb(s), kcol + 1)),
                  cache, cache],
        out_specs=[pl.BlockSpec((rows, SWA_QD), lambda s: (s, 0)), cache, cache],
        out_shape=[jax.ShapeDtypeStruct((TM, SWA_QD), F32),
                   jax.ShapeDtypeStruct((batch, nbuf, SWA_KD), F32),
                   jax.ShapeDtypeStruct((batch, nbuf, SWA_KD), F32)],
        compiler_params=_cparams("arbitrary"),
        name="swa_sample_attn",
    )(sinks, qkv, qkv, qkv, k_cache, v_cache)


def _sg_in_kernel(x_ref, w_ref, b_ref, g_ref, beta_ref, o_ref, wb_ref):
    j = pl.program_id(0)

    @pl.when(pl.program_id(1) == 0)
    def _():
        wb_ref[...] = w_ref[...].astype(BF16)

    z = _gelu(_dot(x_ref[...], wb_ref[...]) + b_ref[...])

    @pl.when(j == 0)
    def _():
        o_ref[...] = z

    @pl.when(j == 1)
    def _():
        o_ref[...] = _layer_norm(z, g_ref[...], beta_ref[...])


def _sg_in(xb, w, layer, b, ln_g, ln_b):
    m = xb.shape[0]
    vec = pl.BlockSpec((1, SG_WIDTH), lambda j, i: (0, 0))
    return pl.pallas_call(
        _sg_in_kernel,
        grid=(2, m // TM),
        in_specs=[pl.BlockSpec((TM, D_MODEL), lambda j, i: (i, 0)),
                  _single_buffered((None, D_MODEL, SG_WIDTH), lambda j, i: (layer, 0, j)),
                  pl.BlockSpec((1, SG_WIDTH), lambda j, i: (0, j)),
                  vec, vec],
        out_specs=pl.BlockSpec((None, TM, SG_WIDTH), lambda j, i: (j, i, 0)),
        out_shape=jax.ShapeDtypeStruct((2, m, SG_WIDTH), F32),
        scratch_shapes=[pltpu.VMEM((D_MODEL, SG_WIDTH), BF16)],
        compiler_params=_cparams("arbitrary", "arbitrary"),
        name="sg_in",
    )(xb, w, b, ln_g, ln_b)


def _sg_mix(u_ref, v_ref, ws_ref, bs_ref, chunk, period):
    tm = u_ref.shape[0]
    r = lax.broadcasted_iota(jnp.int32, (chunk, chunk), 0)
    c = lax.broadcasted_iota(jnp.int32, (chunk, chunk), 1)
    mask = (c <= r) & (r // period == c // period)
    ws = [jnp.where(mask, ws_ref[g], 0.0).astype(BF16) for g in range(SG_GROUPS)]
    rows = []
    for n in range(tm // chunk):
        lo, hi = n * chunk, (n + 1) * chunk
        cols = []
        for g in range(SG_GROUPS):
            vg = v_ref[lo:hi, g * SG_GW:(g + 1) * SG_GW].astype(BF16)
            mixed = _dot(ws[g], vg) + bs_ref[:, g:g + 1]
            cols.append((u_ref[lo:hi, g * SG_GW:(g + 1) * SG_GW] * mixed).astype(BF16))
        rows.append(jnp.concatenate(cols, axis=-1))
    return jnp.concatenate(rows, axis=0)


def _sg_out_kernel(u_ref, v_ref, wsp_ref, bsp_ref, wss_ref, bss_ref, w_ref, bias_ref, res_ref, g_ref, b_ref,
                   of_ref, ob_ref, wb_ref, a_ref, *, n_prompt_tiles, sample_seq):
    i = pl.program_id(0)

    @pl.when(i == 0)
    def _():
        wb_ref[...] = w_ref[...].astype(BF16)

    @pl.when(i < n_prompt_tiles)
    def _():
        a_ref[...] = _sg_mix(u_ref, v_ref, wsp_ref, bsp_ref, SG_CHUNK, SG_CHUNK)

    @pl.when(i >= n_prompt_tiles)
    def _():
        chunk = wss_ref.shape[1]
        a_ref[...] = _sg_mix(u_ref, v_ref, wss_ref, bss_ref, chunk, sample_seq)

    y = _dot(a_ref[...], wb_ref[...]) + bias_ref[...]
    _residual_ln_store(y, res_ref, g_ref, b_ref, of_ref, ob_ref)


def _sg_out(uv, ws_p, bs_p, ws_s, bs_s, w_out, layer, b_out, res, g, b, tm, n_prompt_tiles, sample_seq):
    m = res.shape[0]
    vec = pl.BlockSpec((1, D_MODEL), lambda i: (0, 0))
    full = lambda a: pl.BlockSpec(a.shape, lambda i: (0,) * a.ndim)
    return pl.pallas_call(
        functools.partial(_sg_out_kernel, n_prompt_tiles=n_prompt_tiles, sample_seq=sample_seq),
        grid=(m // tm,),
        in_specs=[pl.BlockSpec((None, tm, SG_WIDTH), lambda i: (0, i, 0)),
                  pl.BlockSpec((None, tm, SG_WIDTH), lambda i: (1, i, 0)),
                  full(ws_p), full(bs_p), full(ws_s), full(bs_s),
                  _single_buffered((None, SG_WIDTH, D_MODEL), lambda i: (layer, 0, 0)),
                  vec,
                  pl.BlockSpec((tm, D_MODEL), lambda i: (i, 0)),
                  vec, vec],
        out_specs=[pl.BlockSpec((tm, D_MODEL), lambda i: (i, 0)),
                   pl.BlockSpec((tm, D_MODEL), lambda i: (i, 0))],
        out_shape=[jax.ShapeDtypeStruct((m, D_MODEL), F32),
                   jax.ShapeDtypeStruct((m, D_MODEL), BF16)],
        scratch_shapes=[pltpu.VMEM((SG_WIDTH, D_MODEL), BF16), pltpu.VMEM((tm, SG_WIDTH), BF16)],
        compiler_params=_cparams("arbitrary"),
        name="sg_out",
    )(uv, uv, ws_p, bs_p, ws_s, bs_s, w_out, b_out, res, g, b)


def kernel(x_prompt, x_sample, state_gla, cache_swa_k, cache_swa_v, state_ffn_conv, ln_mix_g, ln_mix_b, ln_ffn_g, ln_ffn_b, gla_w_in, gla_w_g2, gla_b_g, gla_norm_w, gla_w_out, swa_w_qkv, swa_b_qkv, swa_sinks, swa_w_out, swa_b_out, sg_w_in, sg_b_in, sg_ln_g, sg_ln_b, sg_w_s, sg_b_s, sg_w_out, sg_b_out, ffn_w_in, ffn_conv_w, ffn_conv_b, ffn_w_out):
    bp, lp, _ = x_prompt.shape
    bs, ls, _ = x_sample.shape
    mp, ms = bp * lp, bs * ls
    assert lp % TM == 0 and ms <= TM and TM % ls == 0 and ls >= CONV_W - 1
    n_prompt_tiles = mp // TM
    pad_rows = TM - ms
    tm_out = 256

    xf = jnp.concatenate([x_prompt.reshape(mp, D_MODEL), x_sample.reshape(ms, D_MODEL),
                          jnp.zeros((pad_rows, D_MODEL), F32)], axis=0)
    xb = xf.astype(BF16)
    gla_wg2p = jnp.pad(gla_w_g2.astype(BF16), ((0, 0), (0, GLA_GLOW_PAD - GLA_RANK), (0, 0)))
    conv_b3 = ffn_conv_b.reshape(DEPTH, 1, D_FF)

    gla_p, gla_s, swk_p, swv_p, swk_s, swv_s, sgv_s, conv_p, conv_s = ([] for _ in range(9))
    for i in range(DEPTH):
        j = i // N_MIXERS
        kind = i % N_MIXERS
        ln_g, ln_b = _row(ln_mix_g[i]), _row(ln_mix_b[i])
        if kind == 0:
            proj = _gla_proj(xb, gla_w_in, j)
            bg, nw = _row(gla_b_g[j]), _row(gla_norm_w[j])
            og_p, st_p = _gla_prompt(proj, gla_wg2p[j], bg, nw, bp, lp, 256)
            og_s, st_s = _gla_sample(proj, mp, gla_wg2p[j], bg, nw, state_gla, j, bs, ls, 8 if bs % 8 == 0 else 1)
            gla_p.append(st_p)
            gla_s.append(st_s)
            xf, xb = _mixer_out(og_p, og_s, gla_w_out, j, None, xf, ln_g, ln_b, tm_out, "gla_out")
        elif kind == 1:
            pos = jnp.concatenate([jnp.arange(lp), PAST_LEN + jnp.arange(TM) % ls])
            qkv = _swa_qkv(xb, swa_w_qkv, j, _row(swa_b_qkv[j]), _rope_tables(pos), n_prompt_tiles, lp, 512)
            sinks = swa_sinks[j].reshape(-1)
            oa_p = _swa_prompt_attn(qkv, sinks, bp, lp)
            kv = qkv[:mp].reshape(bp, lp, -1)[:, lp - WINDOW:, SWA_QD:]
            swk_p.append(kv[..., :SWA_KD].reshape(bp, WINDOW, SWA_KV_HEADS, SWA_HEAD_DIM))
            swv_p.append(kv[..., SWA_KD:].reshape(bp, WINDOW, SWA_KV_HEADS, SWA_HEAD_DIM))
            nbuf = cache_swa_k.shape[2]
            oa_s, ko, vo = _swa_sample_attn(qkv, mp, sinks, cache_swa_k[j].reshape(bs, nbuf, SWA_KD),
                                            cache_swa_v[j].reshape(bs, nbuf, SWA_KD), bs, ls,
                                            2 if bs % 2 == 0 else 1)
            swk_s.append(ko.reshape(bs, nbuf, SWA_KV_HEADS, SWA_HEAD_DIM))
            swv_s.append(vo.reshape(bs, nbuf, SWA_KV_HEADS, SWA_HEAD_DIM))
            xf, xb = _mixer_out(oa_p, oa_s, swa_w_out, j, _row(swa_b_out[j]), xf, ln_g, ln_b, tm_out, "swa_out")
        else:
            uv = _sg_in(xb, sg_w_in, j, _row(sg_b_in[j]), _row(sg_ln_g[j]), _row(sg_ln_b[j]))
            sgv_s.append(uv[1, mp:mp + ms].reshape(bs, ls, SG_WIDTH))
            reps = tm_out // ls
            ws_s = jnp.tile(sg_w_s[j][:, :ls, :ls], (1, reps, reps))
            bs_s = jnp.tile(sg_b_s[j][:, :ls].T, (reps, 1))
            xf, xb = _sg_out(uv, sg_w_s[j], sg_b_s[j].T, ws_s, bs_s, sg_w_out, j, _row(sg_b_out[j]),
                             xf, ln_g, ln_b, tm_out, mp // tm_out, ls)
        st = state_ffn_conv[i]
        prev1 = jnp.pad(st[:, 1:2], ((0, 0), (0, ls - 1), (0, 0))).reshape(ms, D_FF)
        prev2 = jnp.pad(st, ((0, 0), (0, ls - 2), (0, 0))).reshape(ms, D_FF)
        prev1 = jnp.pad(prev1, ((0, pad_rows), (0, 0)))
        prev2 = jnp.pad(prev2, ((0, pad_rows), (0, 0)))
        u, gl_p, gl_s = _ffn_in(xb, ffn_w_in, i, ffn_conv_w, conv_b3, prev1, prev2, bp, lp, ls, ms, 512)
        conv_p.append(gl_p.reshape(bp, 8, D_FF)[:, 8 - (CONV_W - 1):])
        conv_s.append(gl_s[:ms].reshape(bs, ls, D_FF)[:, ls - (CONV_W - 1):])
        xf, xb = _ffn_out(u, ffn_w_out[i].astype(BF16), xf, _row(ln_ffn_g[i]), _row(ln_ffn_b[i]), tm_out)

    yp = xf[:mp].reshape(bp, lp, D_MODEL)
    ys = xf[mp:mp + ms].reshape(bs, ls, D_MODEL)
    return (yp, ys, jnp.stack(gla_p), jnp.stack(gla_s), jnp.stack(swk_p), jnp.stack(swv_p),
            jnp.stack(swk_s), jnp.stack(swv_s), jnp.stack(sgv_s), jnp.stack(conv_p), jnp.stack(conv_s))
```

```python
import functools

import jax
import jax.numpy as jnp
from jax import lax
from jax.experimental import pallas as pl
from jax.experimental.pallas import tpu as pltpu

F32 = jnp.float32
BF16 = jnp.bfloat16

D_MODEL = 2048
DEPTH = 4
PAST_LEN = 16384
N_MIXERS = 3
ALPHA = (2 * DEPTH) ** 0.25
LN_EPS = 1e-5

GLA_HEADS = 4
GLA_DK = 256
GLA_DV = 512
GLA_RANK = 16
GLA_TAU = 16.0
GLA_CHUNK = 64
GLA_QK = GLA_HEADS * GLA_DK
GLA_VD = GLA_HEADS * GLA_DV
GLA_PROJ = 2 * GLA_QK + 2 * GLA_VD
GLA_GLOW_PAD = 128
GLA_PROJ_TN = 1280
GLA_PROJ_PAD = 5 * GLA_PROJ_TN
GLA_HEADS_PER_STEP = 2

SWA_HEAD_DIM = 64
SWA_Q_HEADS = 32
SWA_KV_HEADS = 8
SWA_GROUP = 4
WINDOW = 128
ROT_DIM = 16
ROPE_THETA = 500000.0
SWA_QD = SWA_Q_HEADS * SWA_HEAD_DIM
SWA_KD = SWA_KV_HEADS * SWA_HEAD_DIM

SG_WIDTH = 2048
SG_GROUPS = 4
SG_GW = SG_WIDTH // SG_GROUPS
SG_CHUNK = 128

D_FF = 5632
CONV_W = 3

TM = 1024
TM_SG_IN = 512
TM_OUT = 256
SAMPLE_PAD = TM
SAMPLE_ROWS = 256
MXU_COLS = 256
VMEM_LIMIT_BYTES = 56 * 1024 * 1024


def _cparams(*sem):
    return pltpu.CompilerParams(dimension_semantics=sem, vmem_limit_bytes=VMEM_LIMIT_BYTES)


def _dot(a, b):
    return jnp.dot(a, b, preferred_element_type=F32)


def _dot_nt(a, b):
    return lax.dot_general(a, b, (((1,), (1,)), ((), ())), preferred_element_type=F32)


def _dot_tn(a, b):
    return lax.dot_general(a, b, (((0,), (0,)), ((), ())), preferred_element_type=F32)


def _layer_norm(x, g, b):
    mu = jnp.mean(x, -1, keepdims=True)
    xc = x - mu
    var = jnp.mean(xc * xc, -1, keepdims=True)
    return xc * lax.rsqrt(var + LN_EPS) * g + b


def _gelu(x):
    return 0.5 * x * (1.0 + lax.erf(x * (0.5 ** 0.5)))


def _log_sigmoid(x):
    return jnp.minimum(x, 0.0) - jnp.log1p(jnp.exp(-jnp.abs(x)))


def _cumsum_rows(x, period):
    rowmod = lax.broadcasted_iota(jnp.int32, x.shape, 0) % period
    s = 1
    while s < period:
        x = x + jnp.where(rowmod >= s, pltpu.roll(x, s, axis=0), 0.0)
        s *= 2
    return x


def _single_buffered(shape, index_map):
    return pl.BlockSpec(shape, index_map, pipeline_mode=pl.Buffered(1))


def _row(v):
    return v.reshape(1, -1)


def _per_row_tile(i, n_prompt_tiles, tm, body):
    pl.when(i < n_prompt_tiles)(lambda: body(tm))
    pl.when(i >= n_prompt_tiles)(lambda: body(min(tm, SAMPLE_ROWS)))


def _gla_proj_kernel(x_ref, w_ref, o_ref, *, n_prompt_tiles):
    tm = o_ref.shape[0]

    def body(rows):
        o_ref[0:rows, :] = _dot(x_ref[0:rows, :], w_ref[...])
        if rows < tm:
            o_ref[rows:tm, :] = jnp.zeros((tm - rows, o_ref.shape[1]), F32)

    _per_row_tile(pl.program_id(1), n_prompt_tiles, tm, body)


def _gla_proj(xb, w_proj, layer, n_prompt_tiles):
    m, k = xb.shape
    tn = GLA_PROJ_TN
    return pl.pallas_call(
        functools.partial(_gla_proj_kernel, n_prompt_tiles=n_prompt_tiles),
        grid=(GLA_PROJ_PAD // tn, m // TM),
        in_specs=[pl.BlockSpec((TM, k), lambda j, i: (i, 0)),
                  pl.BlockSpec((None, k, tn), lambda j, i: (layer, 0, j))],
        out_specs=pl.BlockSpec((TM, tn), lambda j, i: (i, j)),
        out_shape=jax.ShapeDtypeStruct((m, GLA_PROJ_PAD), F32),
        compiler_params=_cparams("arbitrary", "arbitrary"),
        name="gla_proj",
    )(xb, w_proj)


def _gla_gate_and_decay(q, k, glow_bf16, wg2, bg, period):
    ga = _dot(glow_bf16, wg2) + bg
    cum = _cumsum_rows(_log_sigmoid(ga) * (1.0 / GLA_TAU), period)
    q_dec = (q * (GLA_DK ** -0.5) * jnp.exp(cum)).astype(BF16)
    k_inv = (k * jnp.exp(-cum)).astype(BF16)
    return cum, q_dec, k_inv


def _gla_finish(o, r, nw):
    o = o * lax.rsqrt(jnp.mean(o * o, -1, keepdims=True) + LN_EPS) * nw
    return o * (r * jax.nn.sigmoid(r))


def _causal(n):
    r = lax.broadcasted_iota(jnp.int32, (n, n), 0)
    c = lax.broadcasted_iota(jnp.int32, (n, n), 1)
    return c <= r


def _gla_prompt_kernel(q_ref, k_ref, v_ref, r_ref, gl_ref, wg2_ref, bg_ref, nw_ref,
                       o_ref, sout_ref, st_ref, *, t_rows, chunk, heads):
    t = pl.program_id(2)

    @pl.when(t == 0)
    def _():
        st_ref[...] = jnp.zeros_like(st_ref)

    glow = gl_ref[...].astype(BF16)
    causal = _causal(chunk)
    per_head = []
    for h in range(heads):
        dk = slice(h * GLA_DK, (h + 1) * GLA_DK)
        dv = slice(h * GLA_DV, (h + 1) * GLA_DV)
        k = k_ref[:, dk]
        cum, q_dec, k_inv = _gla_gate_and_decay(q_ref[:, dk], k, glow, wg2_ref[:, dk], bg_ref[:, dk], chunk)
        per_head.append((dv, cum, k, q_dec, k_inv, v_ref[:, dv].astype(BF16), []))
    for c in range(t_rows // chunk):
        lo, hi = c * chunk, (c + 1) * chunk
        for h, (dv, cum, k, q_dec, k_inv, v, outs) in enumerate(per_head):
            cum_c = cum[lo:hi]
            last = cum_c[chunk - 1:chunk, :]
            k_last = (k[lo:hi] * jnp.exp(last - cum_c)).astype(BF16)
            qd, ki, vc = q_dec[lo:hi], k_inv[lo:hi], v[lo:hi]
            attn = jnp.where(causal, _dot_nt(qd, ki), 0.0).astype(BF16)
            st = st_ref[h]
            outs.append(_dot(attn, vc) + _dot_nt(qd, st.astype(BF16)))
            st_ref[h] = st * jnp.exp(last) + _dot_tn(vc, k_last)
    for h, (dv, _, _, _, _, _, outs) in enumerate(per_head):
        o = jnp.concatenate(outs, axis=0)
        o_ref[:, dv] = _gla_finish(o, r_ref[:, dv], nw_ref[...]).astype(o_ref.dtype)

    @pl.when(t == pl.num_programs(2) - 1)
    def _():
        for h in range(heads):
            sout_ref[0, h] = st_ref[h].T


def _gla_prompt(proj, wg2p, bg, norm_w, batch, seq, t_rows):
    nt = seq // t_rows
    hs = GLA_HEADS_PER_STEP
    dk, dv = hs * GLA_DK, hs * GLA_DV
    row = lambda b, h, t: b * nt + t
    in_specs = [
        pl.BlockSpec((t_rows, dk), lambda b, h, t: (row(b, h, t), h)),
        pl.BlockSpec((t_rows, dk), lambda b, h, t: (row(b, h, t), GLA_QK // dk + h)),
        pl.BlockSpec((t_rows, dv), lambda b, h, t: (row(b, h, t), 2 * GLA_QK // dv + h)),
        pl.BlockSpec((t_rows, dv), lambda b, h, t: (row(b, h, t), (2 * GLA_QK + GLA_VD) // dv + h)),
        pl.BlockSpec((t_rows, GLA_GLOW_PAD), lambda b, h, t: (row(b, h, t), GLA_PROJ // GLA_GLOW_PAD)),
        pl.BlockSpec((GLA_GLOW_PAD, dk), lambda b, h, t: (0, h)),
        pl.BlockSpec((1, dk), lambda b, h, t: (0, h)),
        pl.BlockSpec((1, GLA_DV), lambda b, h, t: (0, 0)),
    ]
    return pl.pallas_call(
        functools.partial(_gla_prompt_kernel, t_rows=t_rows, chunk=GLA_CHUNK, heads=hs),
        grid=(batch, GLA_HEADS // hs, nt),
        in_specs=in_specs,
        out_specs=[pl.BlockSpec((t_rows, dv), lambda b, h, t: (row(b, h, t), h)),
                   pl.BlockSpec((1, hs, GLA_DK, GLA_DV), lambda b, h, t: (b, h, 0, 0))],
        out_shape=[jax.ShapeDtypeStruct((batch * seq, GLA_VD), BF16),
                   jax.ShapeDtypeStruct((batch, GLA_HEADS, GLA_DK, GLA_DV), F32)],
        scratch_shapes=[pltpu.VMEM((hs, GLA_DV, GLA_DK), F32)],
        compiler_params=_cparams("arbitrary", "arbitrary", "arbitrary"),
        name="gla_prompt",
    )(proj, proj, proj, proj, proj, wg2p, bg, norm_w)


def _gla_sample_kernel(q_ref, k_ref, v_ref, r_ref, gl_ref, wg2_ref, bg_ref, nw_ref, s0_ref,
                       o_ref, sout_ref, *, seq, nb, n_real):
    g = pl.program_id(1)

    @pl.when(g < n_real)
    def _():
        k = k_ref[...]
        cum, q_dec, k_inv = _gla_gate_and_decay(q_ref[...], k, gl_ref[...].astype(BF16), wg2_ref[...],
                                                bg_ref[...], seq)
        v = v_ref[...].astype(BF16)
        causal = _causal(seq)
        outs = []
        for b in range(nb):
            lo, hi = b * seq, (b + 1) * seq
            cum_b = cum[lo:hi]
            last = cum_b[seq - 1:seq, :]
            k_last = (k[lo:hi] * jnp.exp(last - cum_b)).astype(BF16)
            qd, ki, vc = q_dec[lo:hi], k_inv[lo:hi], v[lo:hi]
            attn = jnp.where(causal, _dot_nt(qd, ki), 0.0).astype(BF16)
            s = s0_ref[b]
            outs.append(_dot(attn, vc) + _dot(qd, s.astype(BF16)))
            decay = jnp.transpose(jnp.broadcast_to(jnp.exp(last), (128, GLA_DK)))[:, 0:1]
            sout_ref[b] = s * decay + _dot_tn(k_last, vc)
        o = jnp.concatenate(outs, axis=0)
        o_ref[...] = _gla_finish(o, r_ref[...], nw_ref[...])

    @pl.when(g >= n_real)
    def _():
        o_ref[...] = jnp.zeros_like(o_ref)


def _gla_sample(proj, row0, wg2p, bg, norm_w, state, layer, batch, seq, nb):
    rows = nb * seq
    n_real = batch // nb
    blk0 = row0 // rows
    rb = lambda g: blk0 + jnp.minimum(g, n_real - 1)
    sb = lambda g: jnp.minimum(g, n_real - 1)
    in_specs = [
        pl.BlockSpec((rows, GLA_DK), lambda h, g: (rb(g), h)),
        pl.BlockSpec((rows, GLA_DK), lambda h, g: (rb(g), GLA_HEADS + h)),
        pl.BlockSpec((rows, GLA_DV), lambda h, g: (rb(g), 2 * GLA_QK // GLA_DV + h)),
        pl.BlockSpec((rows, GLA_DV), lambda h, g: (rb(g), (2 * GLA_QK + GLA_VD) // GLA_DV + h)),
        pl.BlockSpec((rows, GLA_GLOW_PAD), lambda h, g: (rb(g), GLA_PROJ // GLA_GLOW_PAD)),
        pl.BlockSpec((GLA_GLOW_PAD, GLA_DK), lambda h, g: (0, h)),
        pl.BlockSpec((1, GLA_DK), lambda h, g: (0, h)),
        pl.BlockSpec((1, GLA_DV), lambda h, g: (0, 0)),
        pl.BlockSpec((None, nb, None, GLA_DK, GLA_DV), lambda h, g: (layer, sb(g), h, 0, 0)),
    ]
    return pl.pallas_call(
        functools.partial(_gla_sample_kernel, seq=seq, nb=nb, n_real=n_real),
        grid=(GLA_HEADS, SAMPLE_ROWS // rows),
        in_specs=in_specs,
        out_specs=[pl.BlockSpec((rows, GLA_DV), lambda h, g: (g, h)),
                   pl.BlockSpec((nb, None, GLA_DK, GLA_DV), lambda h, g: (sb(g), h, 0, 0))],
        out_shape=[jax.ShapeDtypeStruct((SAMPLE_ROWS, GLA_VD), F32),
                   jax.ShapeDtypeStruct((batch, GLA_HEADS, GLA_DK, GLA_DV), F32)],
        compiler_params=_cparams("arbitrary", "arbitrary"),
        name="gla_sample",
    )(proj, proj, proj, proj, proj, wg2p, bg, norm_w, state)


def _residual_ln_store(y, res_ref, g_ref, b_ref, of_ref, ob_ref):
    o = _layer_norm(ALPHA * res_ref[...] + y, g_ref[...], b_ref[...])
    of_ref[...] = o
    ob_ref[...] = o.astype(BF16)


def _zero_outputs(of_ref, ob_ref):
    of_ref[...] = jnp.zeros_like(of_ref)
    ob_ref[...] = jnp.zeros_like(ob_ref)


def _mixer_out_kernel(*refs, has_bias, n_prompt_tiles):
    if has_bias:
        ap_ref, as_ref, w_ref, bias_ref, res_ref, g_ref, b_ref, of_ref, ob_ref, wb_ref = refs
    else:
        ap_ref, as_ref, w_ref, res_ref, g_ref, b_ref, of_ref, ob_ref, wb_ref = refs
    i = pl.program_id(0)

    @pl.when(i == 0)
    def _():
        wb_ref[...] = w_ref[...].astype(BF16)

    def project(a):
        y = _dot(a, wb_ref[...])
        if has_bias:
            y = y + bias_ref[...]
        _residual_ln_store(y, res_ref, g_ref, b_ref, of_ref, ob_ref)

    pl.when(i < n_prompt_tiles)(lambda: project(ap_ref[...]))
    pl.when(i == n_prompt_tiles)(lambda: project(as_ref[...].astype(BF16)))
    pl.when(i > n_prompt_tiles)(lambda: _zero_outputs(of_ref, ob_ref))


def _mixer_out(a_prompt, a_sample, w, layer, bias, res, g, b, name):
    m = res.shape[0]
    k = a_prompt.shape[1]
    tm = TM_OUT
    n_prompt_tiles = a_prompt.shape[0] // tm
    assert a_sample.shape[0] == tm
    has_bias = bias is not None
    vec = pl.BlockSpec((1, D_MODEL), lambda i: (0, 0))
    in_specs = [pl.BlockSpec((tm, k), lambda i: (jnp.minimum(i, n_prompt_tiles - 1), 0)),
                pl.BlockSpec((tm, k), lambda i: (0, 0)),
                _single_buffered((None, k, D_MODEL), lambda i: (layer, 0, 0))]
    args = [a_prompt, a_sample, w]
    if has_bias:
        in_specs.append(vec)
        args.append(bias)
    in_specs += [pl.BlockSpec((tm, D_MODEL), lambda i: (i, 0)), vec, vec]
    args += [res, g, b]
    return pl.pallas_call(
        functools.partial(_mixer_out_kernel, has_bias=has_bias, n_prompt_tiles=n_prompt_tiles),
        grid=(m // tm,),
        in_specs=in_specs,
        out_specs=[pl.BlockSpec((tm, D_MODEL), lambda i: (i, 0)),
                   pl.BlockSpec((tm, D_MODEL), lambda i: (i, 0))],
        out_shape=[jax.ShapeDtypeStruct((m, D_MODEL), F32),
                   jax.ShapeDtypeStruct((m, D_MODEL), BF16)],
        scratch_shapes=[pltpu.VMEM((k, D_MODEL), BF16)],
        compiler_params=_cparams("arbitrary"),
        name=name,
    )(*args)


def _ffn_out_kernel(a_ref, w_ref, res_ref, g_ref, b_ref, of_ref, ob_ref, *, n_prompt_tiles):
    i = pl.program_id(0)
    pl.when(i <= n_prompt_tiles)(
        lambda: _residual_ln_store(_dot(a_ref[...], w_ref[...]), res_ref, g_ref, b_ref, of_ref, ob_ref))
    pl.when(i > n_prompt_tiles)(lambda: _zero_outputs(of_ref, ob_ref))


def _ffn_out(a, w, layer, res, g, b, n_prompt_tiles):
    m, k = a.shape
    tm = TM_OUT
    vec = pl.BlockSpec((1, D_MODEL), lambda i: (0, 0))
    return pl.pallas_call(
        functools.partial(_ffn_out_kernel, n_prompt_tiles=n_prompt_tiles),
        grid=(m // tm,),
        in_specs=[pl.BlockSpec((tm, k), lambda i: (i, 0)),
                  _single_buffered((None, k, D_MODEL), lambda i: (layer, 0, 0)),
                  pl.BlockSpec((tm, D_MODEL), lambda i: (i, 0)), vec, vec],
        out_specs=[pl.BlockSpec((tm, D_MODEL), lambda i: (i, 0)),
                   pl.BlockSpec((tm, D_MODEL), lambda i: (i, 0))],
        out_shape=[jax.ShapeDtypeStruct((m, D_MODEL), F32),
                   jax.ShapeDtypeStruct((m, D_MODEL), BF16)],
        compiler_params=_cparams("arbitrary"),
        name="ffn_out",
    )(a, w, res, g, b)


def _conv_glu(g, val, g1, g2, cw, cb):
    conv = cb + cw[0:1, :] * g2
    conv = conv + cw[1:2, :] * g1
    conv = conv + cw[2:3, :] * g
    return (_gelu(conv) * val).astype(BF16)


def _ffn_in_kernel(x_ref, wg_ref, wv_ref, cw_ref, cb_ref, p1_ref, p2_ref, u_ref, glp_ref, gls_ref,
                   wgb_ref, wvb_ref, carry_ref, gs_ref, vs_ref, *, seq_tiles, n_tiles, sample_seq):
    i = pl.program_id(1)
    tm, tn = u_ref.shape
    col_slices = [slice(c0, c0 + MXU_COLS) for c0 in range(0, tn, MXU_COLS)]

    def dots(rows):
        x = x_ref[0:rows, :]
        for cs in col_slices:
            gs_ref[0:rows, cs] = _dot(x, wgb_ref[:, cs])
            vs_ref[0:rows, cs] = _dot(x, wvb_ref[:, cs])

    def prompt_epilogue():
        row = lax.broadcasted_iota(jnp.int32, (tm, MXU_COLS), 0)
        first = (i - 1) % seq_tiles == 0
        for cs in col_slices:
            g, val = gs_ref[:, cs], vs_ref[:, cs]
            above = jnp.where(first, 0.0, carry_ref[:, cs])
            g1 = jnp.where(row >= 1, pltpu.roll(g, 1, axis=0), above[7:8, :])
            g2 = jnp.where(row >= 2, pltpu.roll(g, 2, axis=0),
                           jnp.where(row == 0, above[6:7, :], above[7:8, :]))
            u_ref[:, cs] = _conv_glu(g, val, g1, g2, cw_ref[:, cs], cb_ref[:, cs])
            carry_ref[:, cs] = g[tm - 8:tm, :]
            glp_ref[:, cs] = g[tm - 8:tm, :]

    @pl.when(i == 0)
    def _():
        wgb_ref[...] = wg_ref[...].astype(BF16)
        wvb_ref[...] = wv_ref[...].astype(BF16)
        dots(tm)

    @pl.when((i > 0) & (i < n_tiles - 1))
    def _():
        prompt_epilogue()
        dots(tm)

    @pl.when(i == n_tiles - 1)
    def _():
        prompt_epilogue()
        dots(SAMPLE_ROWS)

    @pl.when(i == n_tiles)
    def _():
        rows = SAMPLE_ROWS
        pos = lax.broadcasted_iota(jnp.int32, (rows, MXU_COLS), 0) % sample_seq
        for cs in col_slices:
            g, val = gs_ref[0:rows, cs], vs_ref[0:rows, cs]
            g1 = jnp.where(pos >= 1, pltpu.roll(g, 1, axis=0), p1_ref[:, cs])
            g2 = jnp.where(pos >= 2, pltpu.roll(g, 2, axis=0), p2_ref[:, cs])
            u_ref[0:rows, cs] = _conv_glu(g, val, g1, g2, cw_ref[:, cs], cb_ref[:, cs])
            gls_ref[:, cs] = g
        u_ref[rows:tm, :] = jnp.zeros((tm - rows, tn), BF16)


def _ffn_in(xb, w_in, layer, conv_w, conv_b, prev1, prev2, n_prompt_seqs, seq, sample_seq, tn):
    m = xb.shape[0]
    nj = D_FF // tn
    seq_tiles = seq // TM
    n_tiles = m // TM
    n_prompt_tiles = n_prompt_seqs * seq_tiles
    assert n_tiles == n_prompt_tiles + 1
    done = lambda i: jnp.maximum(i - 1, 0)
    sample = pl.BlockSpec((SAMPLE_ROWS, tn), lambda j, i: (0, j))
    return pl.pallas_call(
        functools.partial(_ffn_in_kernel, seq_tiles=seq_tiles, n_tiles=n_tiles, sample_seq=sample_seq),
        grid=(nj, n_tiles + 1),
        in_specs=[pl.BlockSpec((TM, D_MODEL), lambda j, i: (jnp.minimum(i, n_tiles - 1), 0)),
                  pl.BlockSpec((None, D_MODEL, tn), lambda j, i: (layer, 0, j)),
                  pl.BlockSpec((None, D_MODEL, tn), lambda j, i: (layer, 0, nj + j)),
                  pl.BlockSpec((None, CONV_W, tn), lambda j, i: (layer, 0, j)),
                  pl.BlockSpec((None, 1, tn), lambda j, i: (layer, 0, j)),
                  sample, sample],
        out_specs=[pl.BlockSpec((TM, tn), lambda j, i: (done(i), j)),
                   pl.BlockSpec((8, tn), lambda j, i: (jnp.minimum(done(i), n_prompt_tiles - 1) // seq_tiles, j)),
                   sample],
        out_shape=[jax.ShapeDtypeStruct((m, D_FF), BF16),
                   jax.ShapeDtypeStruct((n_prompt_seqs * 8, D_FF), F32),
                   jax.ShapeDtypeStruct((SAMPLE_ROWS, D_FF), F32)],
        scratch_shapes=[pltpu.VMEM((D_MODEL, tn), BF16), pltpu.VMEM((D_MODEL, tn), BF16),
                        pltpu.VMEM((8, tn), F32), pltpu.VMEM((TM, tn), F32), pltpu.VMEM((TM, tn), F32)],
        compiler_params=_cparams("arbitrary", "arbitrary"),
        name="ffn_in",
    )(xb, w_in, w_in, conv_w, conv_b, prev1, prev2)


def _swa_qkv_kernel(x_ref, w_ref, b_ref, ca_ref, cm_ref, cp_ref, o_ref, wb_ref, *, n_rot_tiles, n_prompt_tiles):
    j = pl.program_id(0)
    tm, tn = o_ref.shape

    @pl.when(pl.program_id(1) == 0)
    def _():
        wb_ref[...] = w_ref[...].astype(BF16)

    def body(rows):
        x = x_ref[0:rows, :]
        reps = MXU_COLS // ca_ref.shape[1]
        ca = jnp.concatenate([ca_ref[0:rows, :]] * reps, axis=1)
        cm = jnp.concatenate([cm_ref[0:rows, :]] * reps, axis=1)
        cp = jnp.concatenate([cp_ref[0:rows, :]] * reps, axis=1)
        lane = lax.broadcasted_iota(jnp.int32, (rows, MXU_COLS), 1) % SWA_HEAD_DIM
        rotated = lane < jnp.where(j < n_rot_tiles, ROT_DIM, 0)
        half = ROT_DIM // 2
        for c0 in range(0, tn, MXU_COLS):
            sl = slice(c0, c0 + MXU_COLS)
            y = _dot(x, wb_ref[:, sl]) + b_ref[:, sl]
            rot = y * ca + pltpu.roll(y, MXU_COLS - half, axis=1) * cm + pltpu.roll(y, half, axis=1) * cp
            o_ref[0:rows, sl] = jnp.where(rotated, rot, y)
        if rows < tm:
            o_ref[rows:tm, :] = jnp.zeros((tm - rows, tn), F32)

    _per_row_tile(pl.program_id(1), n_prompt_tiles, tm, body)


def _rope_tables(pos):
    half = ROT_DIM // 2
    inv = ROPE_THETA ** (-jnp.arange(half, dtype=F32) / half)
    ang = pos.astype(F32)[:, None] * inv
    cos, sin = jnp.cos(ang), jnp.sin(ang)
    n = pos.shape[0]
    rest = SWA_HEAD_DIM - ROT_DIM
    ca = jnp.concatenate([cos, cos, jnp.ones((n, rest), F32)], -1)
    cm = jnp.concatenate([-sin, jnp.zeros((n, half + rest), F32)], -1)
    cp = jnp.concatenate([jnp.zeros((n, half), F32), sin, jnp.zeros((n, rest), F32)], -1)
    return tuple(jnp.tile(t, (1, 128 // SWA_HEAD_DIM)) for t in (ca, cm, cp))


def _swa_qkv(xb, w, layer, b, tables, n_prompt_tiles, seq, tn):
    m = xb.shape[0]
    n = w.shape[2]
    pos_tiles = seq // TM
    tab = pl.BlockSpec((TM, 128), lambda j, i: (jnp.where(i < n_prompt_tiles, i % pos_tiles, pos_tiles), 0))
    return pl.pallas_call(
        functools.partial(_swa_qkv_kernel, n_rot_tiles=(SWA_QD + SWA_KD) // tn, n_prompt_tiles=n_prompt_tiles),
        grid=(n // tn, m // TM),
        in_specs=[pl.BlockSpec((TM, D_MODEL), lambda j, i: (i, 0)),
                  pl.BlockSpec((None, D_MODEL, tn), lambda j, i: (layer, 0, j)),
                  pl.BlockSpec((1, tn), lambda j, i: (0, j)),
                  tab, tab, tab],
        out_specs=pl.BlockSpec((TM, tn), lambda j, i: (i, j)),
        out_shape=jax.ShapeDtypeStruct((m, n), F32),
        scratch_shapes=[pltpu.VMEM((D_MODEL, tn), BF16)],
        compiler_params=_cparams("arbitrary", "arbitrary"),
        name="swa_qkv",
    )(xb, w, b, *tables)


def _sink_softmax_pv(s, mask, sink, v):
    s = jnp.where(mask, s * (SWA_HEAD_DIM ** -0.5), -jnp.inf)
    mx = jnp.maximum(jnp.max(s, -1, keepdims=True), sink)
    p = jnp.exp(s - mx)
    denom = jnp.sum(p, -1, keepdims=True) + jnp.exp(sink - mx)
    return _dot((p / denom).astype(BF16), v)


def _swa_prompt_kernel(sink_ref, q_ref, kp_ref, kc_ref, vp_ref, vc_ref, o_ref):
    i = pl.program_id(1)
    q = q_ref[...].astype(BF16)
    kb = jnp.concatenate([kp_ref[...], kc_ref[...]], axis=0).astype(BF16)
    vb = jnp.concatenate([vp_ref[...], vc_ref[...]], axis=0).astype(BF16)
    r = lax.broadcasted_iota(jnp.int32, (WINDOW, 2 * WINDOW), 0)
    c = lax.broadcasted_iota(jnp.int32, (WINDOW, 2 * WINDOW), 1)
    mask = (c > r) & (c <= r + WINDOW) & (c >= jnp.where(i > 0, 0, WINDOW))
    pieces = []
    for h in range(SWA_KV_HEADS):
        kh = kb[:, h * SWA_HEAD_DIM:(h + 1) * SWA_HEAD_DIM]
        vh = vb[:, h * SWA_HEAD_DIM:(h + 1) * SWA_HEAD_DIM]
        for g in range(SWA_GROUP):
            hq = h * SWA_GROUP + g
            qg = q[:, hq * SWA_HEAD_DIM:(hq + 1) * SWA_HEAD_DIM]
            pieces.append(_sink_softmax_pv(_dot_nt(qg, kh), mask, sink_ref[hq], vh))
    o_ref[...] = jnp.concatenate(pieces, axis=-1).astype(o_ref.dtype)


def _swa_prompt_attn(qkv, sinks, batch, seq):
    nb = seq // WINDOW
    kcol = SWA_QD // SWA_KD
    cur = lambda b, i: b * nb + i
    prev = lambda b, i: b * nb + jnp.maximum(i - 1, 0)
    return pl.pallas_call(
        _swa_prompt_kernel,
        grid=(batch, nb),
        in_specs=[pl.BlockSpec(memory_space=pltpu.SMEM),
                  pl.BlockSpec((WINDOW, SWA_QD), lambda b, i: (cur(b, i), 0)),
                  pl.BlockSpec((WINDOW, SWA_KD), lambda b, i: (prev(b, i), kcol)),
                  pl.BlockSpec((WINDOW, SWA_KD), lambda b, i: (cur(b, i), kcol)),
                  pl.BlockSpec((WINDOW, SWA_KD), lambda b, i: (prev(b, i), kcol + 1)),
                  pl.BlockSpec((WINDOW, SWA_KD), lambda b, i: (cur(b, i), kcol + 1))],
        out_specs=pl.BlockSpec((WINDOW, SWA_QD), lambda b, i: (cur(b, i), 0)),
        out_shape=jax.ShapeDtypeStruct((batch * seq, SWA_QD), BF16),
        compiler_params=_cparams("arbitrary", "arbitrary"),
        name="swa_prompt_attn",
    )(sinks, qkv, qkv, qkv, qkv, qkv)


def _swa_sample_kernel(sink_ref, q_ref, kn_ref, vn_ref, kc_ref, vc_ref, o_ref, ko_ref, vo_ref,
                       *, seq, nb, n_real):
    step = pl.program_id(0)
    nbuf = kc_ref.shape[1]

    @pl.when(step < n_real)
    def _():
        rows = SWA_GROUP * seq
        l = lax.broadcasted_iota(jnp.int32, (rows, nbuf + seq), 0) % seq
        c = lax.broadcasted_iota(jnp.int32, (rows, nbuf + seq), 1)
        diff = l + nbuf - c
        mask = (diff >= 0) & (diff < WINDOW)
        grp = lax.broadcasted_iota(jnp.int32, (rows, 1), 0) // seq
        for b in range(nb):
            lo, hi = b * seq, (b + 1) * seq
            k_all = jnp.concatenate([kc_ref[b], kn_ref[lo:hi, :]], axis=0)
            v_all = jnp.concatenate([vc_ref[b], vn_ref[lo:hi, :]], axis=0)
            ko_ref[b] = k_all[seq:, :]
            vo_ref[b] = v_all[seq:, :]
            kb = k_all.astype(BF16)
            vb = v_all.astype(BF16)
            q = q_ref[lo:hi, :]
            pieces = []
            for h in range(SWA_KV_HEADS):
                kh = kb[:, h * SWA_HEAD_DIM:(h + 1) * SWA_HEAD_DIM]
                vh = vb[:, h * SWA_HEAD_DIM:(h + 1) * SWA_HEAD_DIM]
                qs = jnp.concatenate(
                    [q[:, (h * SWA_GROUP + g) * SWA_HEAD_DIM:(h * SWA_GROUP + g + 1) * SWA_HEAD_DIM]
                     for g in range(SWA_GROUP)], axis=0).astype(BF16)
                sink = jnp.zeros((rows, 1), F32)
                for g in range(SWA_GROUP):
                    sink = jnp.where(grp == g, sink_ref[h * SWA_GROUP + g], sink)
                o = _sink_softmax_pv(_dot_nt(qs, kh), mask, sink, vh)
                pieces += [o[g * seq:(g + 1) * seq, :] for g in range(SWA_GROUP)]
            o_ref[lo:hi, :] = jnp.concatenate(pieces, axis=-1)

    @pl.when(step >= n_real)
    def _():
        o_ref[...] = jnp.zeros_like(o_ref)


def _swa_sample_attn(qkv, row0, sinks, k_cache, v_cache, batch, seq, nb):
    nbuf = k_cache.shape[1]
    kcol = SWA_QD // SWA_KD
    rows = nb * seq
    n_real = batch // nb
    blk0 = row0 // rows
    rb = lambda s: blk0 + jnp.minimum(s, n_real - 1)
    cache = pl.BlockSpec((nb, nbuf, SWA_KD), lambda s: (jnp.minimum(s, n_real - 1), 0, 0))
    return pl.pallas_call(
        functools.partial(_swa_sample_kernel, seq=seq, nb=nb, n_real=n_real),
        grid=(SAMPLE_ROWS // rows,),
        in_specs=[pl.BlockSpec(memory_space=pltpu.SMEM),
                  pl.BlockSpec((rows, SWA_QD), lambda s: (rb(s), 0)),
                  pl.BlockSpec((rows, SWA_KD), lambda s: (rb(s), kcol)),
                  pl.BlockSpec((rows, SWA_KD), lambda s: (rb(s), kcol + 1)),
                  cache, cache],
        out_specs=[pl.BlockSpec((rows, SWA_QD), lambda s: (s, 0)), cache, cache],
        out_shape=[jax.ShapeDtypeStruct((SAMPLE_ROWS, SWA_QD), F32),
                   jax.ShapeDtypeStruct((batch, nbuf, SWA_KD), F32),
                   jax.ShapeDtypeStruct((batch, nbuf, SWA_KD), F32)],
        compiler_params=_cparams("arbitrary"),
        name="swa_sample_attn",
    )(sinks, qkv, qkv, qkv, k_cache, v_cache)


def _sg_in_kernel(x_ref, w_ref, b_ref, g_ref, beta_ref, o_ref, wb_ref, *, n_prompt_tiles):
    j = pl.program_id(0)
    tm = o_ref.shape[0]

    @pl.when(pl.program_id(1) == 0)
    def _():
        wb_ref[...] = w_ref[...].astype(BF16)

    def body(rows, normalize):
        z = _gelu(_dot(x_ref[0:rows, :], wb_ref[...]) + b_ref[...])
        o_ref[0:rows, :] = _layer_norm(z, g_ref[...], beta_ref[...]) if normalize else z
        if rows < tm:
            o_ref[rows:tm, :] = jnp.zeros((tm - rows, o_ref.shape[1]), F32)

    pl.when(j == 0)(lambda: _per_row_tile(pl.program_id(1), n_prompt_tiles, tm, lambda rows: body(rows, False)))
    pl.when(j == 1)(lambda: _per_row_tile(pl.program_id(1), n_prompt_tiles, tm, lambda rows: body(rows, True)))


def _sg_in(xb, w, layer, b, ln_g, ln_b, n_prompt_tiles):
    m = xb.shape[0]
    tm = TM_SG_IN
    vec = pl.BlockSpec((1, SG_WIDTH), lambda j, i: (0, 0))
    return pl.pallas_call(
        functools.partial(_sg_in_kernel, n_prompt_tiles=n_prompt_tiles),
        grid=(2, m // tm),
        in_specs=[pl.BlockSpec((tm, D_MODEL), lambda j, i: (i, 0)),
                  _single_buffered((None, D_MODEL, SG_WIDTH), lambda j, i: (layer, 0, j)),
                  pl.BlockSpec((1, SG_WIDTH), lambda j, i: (0, j)),
                  vec, vec],
        out_specs=pl.BlockSpec((None, tm, SG_WIDTH), lambda j, i: (j, i, 0)),
        out_shape=jax.ShapeDtypeStruct((2, m, SG_WIDTH), F32),
        scratch_shapes=[pltpu.VMEM((D_MODEL, SG_WIDTH), BF16)],
        compiler_params=_cparams("arbitrary", "arbitrary"),
        name="sg_in",
    )(xb, w, b, ln_g, ln_b)


def _sg_mix(u_ref, v_ref, ws_ref, bs_ref, chunk, period):
    tm = u_ref.shape[0]
    r = lax.broadcasted_iota(jnp.int32, (chunk, chunk), 0)
    c = lax.broadcasted_iota(jnp.int32, (chunk, chunk), 1)
    mask = (c <= r) & (r // period == c // period)
    ws = [jnp.where(mask, ws_ref[g], 0.0).astype(BF16) for g in range(SG_GROUPS)]
    rows = []
    for n in range(tm // chunk):
        lo, hi = n * chunk, (n + 1) * chunk
        cols = []
        for g in range(SG_GROUPS):
            vg = v_ref[lo:hi, g * SG_GW:(g + 1) * SG_GW].astype(BF16)
            mixed = _dot(ws[g], vg) + bs_ref[:, g:g + 1]
            cols.append((u_ref[lo:hi, g * SG_GW:(g + 1) * SG_GW] * mixed).astype(BF16))
        rows.append(jnp.concatenate(cols, axis=-1))
    return jnp.concatenate(rows, axis=0)


def _sg_out_kernel(u_ref, v_ref, wsp_ref, bsp_ref, wss_ref, bss_ref, w_ref, bias_ref, res_ref, g_ref, b_ref,
                   of_ref, ob_ref, wb_ref, *, n_prompt_tiles, sample_seq):
    i = pl.program_id(0)

    @pl.when(i == 0)
    def _():
        wb_ref[...] = w_ref[...].astype(BF16)

    def project(a):
        _residual_ln_store(_dot(a, wb_ref[...]) + bias_ref[...], res_ref, g_ref, b_ref, of_ref, ob_ref)

    pl.when(i < n_prompt_tiles)(lambda: project(_sg_mix(u_ref, v_ref, wsp_ref, bsp_ref, SG_CHUNK, SG_CHUNK)))
    pl.when(i == n_prompt_tiles)(
        lambda: project(_sg_mix(u_ref, v_ref, wss_ref, bss_ref, wss_ref.shape[1], sample_seq)))
    pl.when(i > n_prompt_tiles)(lambda: _zero_outputs(of_ref, ob_ref))


def _sg_out(uv, ws_p, bs_p, ws_s, bs_s, w_out, layer, b_out, res, g, b, n_prompt_tiles, sample_seq):
    m = res.shape[0]
    tm = TM_OUT
    vec = pl.BlockSpec((1, D_MODEL), lambda i: (0, 0))
    full = lambda a: pl.BlockSpec(a.shape, lambda i: (0,) * a.ndim)
    return pl.pallas_call(
        functools.partial(_sg_out_kernel, n_prompt_tiles=n_prompt_tiles, sample_seq=sample_seq),
        grid=(m // tm,),
        in_specs=[pl.BlockSpec((None, tm, SG_WIDTH), lambda i: (0, i, 0)),
                  pl.BlockSpec((None, tm, SG_WIDTH), lambda i: (1, i, 0)),
                  full(ws_p), full(bs_p), full(ws_s), full(bs_s),
                  _single_buffered((None, SG_WIDTH, D_MODEL), lambda i: (layer, 0, 0)),
                  vec,
                  pl.BlockSpec((tm, D_MODEL), lambda i: (i, 0)),
                  vec, vec],
        out_specs=[pl.BlockSpec((tm, D_MODEL), lambda i: (i, 0)),
                   pl.BlockSpec((tm, D_MODEL), lambda i: (i, 0))],
        out_shape=[jax.ShapeDtypeStruct((m, D_MODEL), F32),
                   jax.ShapeDtypeStruct((m, D_MODEL), BF16)],
        scratch_shapes=[pltpu.VMEM((SG_WIDTH, D_MODEL), BF16)],
        compiler_params=_cparams("arbitrary"),
        name="sg_out",
    )(uv, uv, ws_p, bs_p, ws_s, bs_s, w_out, b_out, res, g, b)


def kernel(x_prompt, x_sample, state_gla, cache_swa_k, cache_swa_v, state_ffn_conv, ln_mix_g, ln_mix_b, ln_ffn_g, ln_ffn_b, gla_w_in, gla_w_g2, gla_b_g, gla_norm_w, gla_w_out, swa_w_qkv, swa_b_qkv, swa_sinks, swa_w_out, swa_b_out, sg_w_in, sg_b_in, sg_ln_g, sg_ln_b, sg_w_s, sg_b_s, sg_w_out, sg_b_out, ffn_w_in, ffn_conv_w, ffn_conv_b, ffn_w_out):
    bp, lp, _ = x_prompt.shape
    bs, ls, _ = x_sample.shape
    mp, ms = bp * lp, bs * ls
    assert lp % TM == 0 and ms <= SAMPLE_ROWS and SAMPLE_ROWS % ls == 0 and ls >= CONV_W - 1
    assert SAMPLE_ROWS == TM_OUT and SAMPLE_PAD % TM_SG_IN == 0

    xf = jnp.concatenate([x_prompt.reshape(mp, D_MODEL), x_sample.reshape(ms, D_MODEL),
                          jnp.zeros((SAMPLE_PAD - ms, D_MODEL), F32)], axis=0)
    xb = xf.astype(BF16)
    gla_w_proj = jnp.pad(gla_w_in.astype(BF16), ((0, 0), (0, 0), (0, GLA_PROJ_PAD - gla_w_in.shape[2])))
    gla_wg2p = jnp.pad(gla_w_g2.astype(BF16), ((0, 0), (0, GLA_GLOW_PAD - GLA_RANK), (0, 0)))
    ffn_w_out_b = ffn_w_out.astype(BF16)
    conv_b3 = ffn_conv_b.reshape(DEPTH, 1, D_FF)

    gla_p, gla_s, swk_p, swv_p, swk_s, swv_s, sgv_s, conv_p, conv_s = ([] for _ in range(9))
    for i in range(DEPTH):
        j = i // N_MIXERS
        kind = i % N_MIXERS
        ln_g, ln_b = _row(ln_mix_g[i]), _row(ln_mix_b[i])
        if kind == 0:
            proj = _gla_proj(xb, gla_w_proj, j, mp // TM)
            bg, nw = _row(gla_b_g[j]), _row(gla_norm_w[j])
            og_p, st_p = _gla_prompt(proj, gla_wg2p[j], bg, nw, bp, lp, 256)
            og_s, st_s = _gla_sample(proj, mp, gla_wg2p[j], bg, nw, state_gla, j, bs, ls, 8 if bs % 8 == 0 else 1)
            gla_p.append(st_p)
            gla_s.append(st_s)
            xf, xb = _mixer_out(og_p, og_s, gla_w_out, j, None, xf, ln_g, ln_b, "gla_out")
        elif kind == 1:
            pos = jnp.concatenate([jnp.arange(lp), PAST_LEN + jnp.arange(TM) % ls])
            qkv = _swa_qkv(xb, swa_w_qkv, j, _row(swa_b_qkv[j]), _rope_tables(pos), mp // TM, lp, 512)
            sinks = swa_sinks[j].reshape(-1)
            oa_p = _swa_prompt_attn(qkv, sinks, bp, lp)
            kv = jnp.stack([qkv[(b + 1) * lp - WINDOW:(b + 1) * lp, SWA_QD:] for b in range(bp)])
            swk_p.append(kv[..., :SWA_KD].reshape(bp, WINDOW, SWA_KV_HEADS, SWA_HEAD_DIM))
            swv_p.append(kv[..., SWA_KD:].reshape(bp, WINDOW, SWA_KV_HEADS, SWA_HEAD_DIM))
            nbuf = cache_swa_k.shape[2]
            oa_s, ko, vo = _swa_sample_attn(qkv, mp, sinks, cache_swa_k[j].reshape(bs, nbuf, SWA_KD),
                                            cache_swa_v[j].reshape(bs, nbuf, SWA_KD), bs, ls,
                                            2 if bs % 2 == 0 else 1)
            swk_s.append(ko.reshape(bs, nbuf, SWA_KV_HEADS, SWA_HEAD_DIM))
            swv_s.append(vo.reshape(bs, nbuf, SWA_KV_HEADS, SWA_HEAD_DIM))
            xf, xb = _mixer_out(oa_p, oa_s, swa_w_out, j, _row(swa_b_out[j]), xf, ln_g, ln_b, "swa_out")
        else:
            uv = _sg_in(xb, sg_w_in, j, _row(sg_b_in[j]), _row(sg_ln_g[j]), _row(sg_ln_b[j]), mp // TM_SG_IN)
            sgv_s.append(uv[1, mp:mp + ms].reshape(bs, ls, SG_WIDTH))
            reps = TM_OUT // ls
            ws_s = jnp.tile(sg_w_s[j][:, :ls, :ls], (1, reps, reps))
            bs_s = jnp.tile(sg_b_s[j][:, :ls].T, (reps, 1))
            xf, xb = _sg_out(uv, sg_w_s[j], sg_b_s[j].T, ws_s, bs_s, sg_w_out, j, _row(sg_b_out[j]),
                             xf, ln_g, ln_b, mp // TM_OUT, ls)
        st = state_ffn_conv[i]
        prev1 = jnp.pad(st[:, 1:2], ((0, 0), (0, ls - 1), (0, 0))).reshape(ms, D_FF)
        prev2 = jnp.pad(st, ((0, 0), (0, ls - 2), (0, 0))).reshape(ms, D_FF)
        prev1 = jnp.pad(prev1, ((0, SAMPLE_ROWS - ms), (0, 0)))
        prev2 = jnp.pad(prev2, ((0, SAMPLE_ROWS - ms), (0, 0)))
        u, gl_p, gl_s = _ffn_in(xb, ffn_w_in, i, ffn_conv_w, conv_b3, prev1, prev2, bp, lp, ls, 512)
        conv_p.append(gl_p.reshape(bp, 8, D_FF)[:, 8 - (CONV_W - 1):])
        conv_s.append(gl_s[:ms].reshape(bs, ls, D_FF)[:, ls - (CONV_W - 1):])
        xf, xb = _ffn_out(u, ffn_w_out_b, i, xf, _row(ln_ffn_g[i]), _row(ln_ffn_b[i]), mp // TM_OUT)

    yp = xf[:mp].reshape(bp, lp, D_MODEL)
    ys = xf[mp:mp + ms].reshape(bs, ls, D_MODEL)
    return (yp, ys, jnp.stack(gla_p), jnp.stack(gla_s), jnp.stack(swk_p), jnp.stack(swv_p),
            jnp.stack(swk_s), jnp.stack(swv_s), jnp.stack(sgv_s), jnp.stack(conv_p), jnp.stack(conv_s))
```

```python
import functools

import jax
import jax.numpy as jnp
from jax import lax
from jax.experimental import pallas as pl
from jax.experimental.pallas import tpu as pltpu

F32 = jnp.float32
BF16 = jnp.bfloat16

D_MODEL = 2048
DEPTH = 4
PAST_LEN = 16384
N_MIXERS = 3
ALPHA = (2 * DEPTH) ** 0.25
LN_EPS = 1e-5

GLA_HEADS = 4
GLA_DK = 256
GLA_DV = 512
GLA_RANK = 16
GLA_TAU = 16.0
GLA_CHUNK = 64
GLA_QK = GLA_HEADS * GLA_DK
GLA_VD = GLA_HEADS * GLA_DV
GLA_PROJ = 2 * GLA_QK + 2 * GLA_VD
GLA_GLOW_PAD = 128
GLA_PROJ_TN = 1280
GLA_PROJ_PAD = 5 * GLA_PROJ_TN
GLA_HEADS_PER_STEP = 4

SWA_HEAD_DIM = 64
SWA_Q_HEADS = 32
SWA_KV_HEADS = 8
SWA_GROUP = 4
WINDOW = 128
ROT_DIM = 16
ROPE_THETA = 500000.0
SWA_QD = SWA_Q_HEADS * SWA_HEAD_DIM
SWA_KD = SWA_KV_HEADS * SWA_HEAD_DIM

SG_WIDTH = 2048
SG_GROUPS = 4
SG_GW = SG_WIDTH // SG_GROUPS
SG_CHUNK = 128

D_FF = 5632
CONV_W = 3

TM = 1024
TM_SG_IN = 512
TM_OUT = 256
SAMPLE_PAD = TM
SAMPLE_ROWS = 256
MXU_COLS = 256
VMEM_LIMIT_BYTES = 56 * 1024 * 1024


def _cparams(*sem):
    return pltpu.CompilerParams(dimension_semantics=sem, vmem_limit_bytes=VMEM_LIMIT_BYTES)


def _dot(a, b):
    return jnp.dot(a, b, preferred_element_type=F32)


def _dot_nt(a, b):
    return lax.dot_general(a, b, (((1,), (1,)), ((), ())), preferred_element_type=F32)


def _dot_tn(a, b):
    return lax.dot_general(a, b, (((0,), (0,)), ((), ())), preferred_element_type=F32)


def _layer_norm(x, g, b):
    mu = jnp.mean(x, -1, keepdims=True)
    xc = x - mu
    var = jnp.mean(xc * xc, -1, keepdims=True)
    return xc * lax.rsqrt(var + LN_EPS) * g + b


def _gelu(x):
    return 0.5 * x * (1.0 + lax.erf(x * (0.5 ** 0.5)))


def _log_sigmoid(x):
    return jnp.minimum(x, 0.0) - jnp.log1p(jnp.exp(-jnp.abs(x)))


def _cumsum_rows(x, period):
    rowmod = lax.broadcasted_iota(jnp.int32, x.shape, 0) % period
    s = 1
    while s < period:
        x = x + jnp.where(rowmod >= s, pltpu.roll(x, s, axis=0), 0.0)
        s *= 2
    return x


def _single_buffered(shape, index_map):
    return pl.BlockSpec(shape, index_map, pipeline_mode=pl.Buffered(1))


def _row(v):
    return v.reshape(1, -1)


def _per_row_tile(i, n_prompt_tiles, tm, body):
    pl.when(i < n_prompt_tiles)(lambda: body(tm))
    pl.when(i >= n_prompt_tiles)(lambda: body(min(tm, SAMPLE_ROWS)))


def _lag(i):
    return jnp.maximum(i - 1, 0)


def _deferred_row_tiles(i, n_prompt_tiles, n_tiles, tm, first, dots, epilogue):
    s = min(tm, SAMPLE_ROWS)

    @pl.when(i == 0)
    def _():
        first()
        dots(tm)

    @pl.when((i > 0) & (i < n_prompt_tiles))
    def _():
        epilogue(tm)
        dots(tm)

    @pl.when(i == n_prompt_tiles)
    def _():
        epilogue(tm)
        dots(s)

    if n_tiles > n_prompt_tiles + 1:
        @pl.when((i > n_prompt_tiles) & (i < n_tiles))
        def _():
            epilogue(s)
            dots(s)

    pl.when(i == n_tiles)(lambda: epilogue(s))


def _gla_proj_kernel(x_ref, w_ref, o_ref, *, n_prompt_tiles):
    tm = o_ref.shape[0]

    def body(rows):
        o_ref[0:rows, :] = _dot(x_ref[0:rows, :], w_ref[...])
        if rows < tm:
            o_ref[rows:tm, :] = jnp.zeros((tm - rows, o_ref.shape[1]), F32)

    _per_row_tile(pl.program_id(1), n_prompt_tiles, tm, body)


def _gla_proj(xb, w_proj, layer, n_prompt_tiles):
    m, k = xb.shape
    tn = GLA_PROJ_TN
    return pl.pallas_call(
        functools.partial(_gla_proj_kernel, n_prompt_tiles=n_prompt_tiles),
        grid=(GLA_PROJ_PAD // tn, m // TM),
        in_specs=[pl.BlockSpec((TM, k), lambda j, i: (i, 0)),
                  pl.BlockSpec((None, k, tn), lambda j, i: (layer, 0, j))],
        out_specs=pl.BlockSpec((TM, tn), lambda j, i: (i, j)),
        out_shape=jax.ShapeDtypeStruct((m, GLA_PROJ_PAD), F32),
        compiler_params=_cparams("arbitrary", "arbitrary"),
        name="gla_proj",
    )(xb, w_proj)


def _gla_gate_and_decay(q, k, glow_bf16, wg2, bg, period):
    ga = _dot(glow_bf16, wg2) + bg
    cum = _cumsum_rows(_log_sigmoid(ga) * (1.0 / GLA_TAU), period)
    q_dec = (q * (GLA_DK ** -0.5) * jnp.exp(cum)).astype(BF16)
    k_inv = (k * jnp.exp(-cum)).astype(BF16)
    return cum, q_dec, k_inv


def _gla_finish(o, r, nw):
    o = o * lax.rsqrt(jnp.mean(o * o, -1, keepdims=True) + LN_EPS) * nw
    return o * (r * jax.nn.sigmoid(r))


def _causal(n):
    r = lax.broadcasted_iota(jnp.int32, (n, n), 0)
    c = lax.broadcasted_iota(jnp.int32, (n, n), 1)
    return c <= r


def _gla_prompt_kernel(q_ref, k_ref, v_ref, r_ref, gl_ref, wg2_ref, bg_ref, nw_ref,
                       o_ref, sout_ref, st_ref, *, t_rows, chunk, heads):
    t = pl.program_id(2)

    @pl.when(t == 0)
    def _():
        st_ref[...] = jnp.zeros_like(st_ref)

    glow = gl_ref[...].astype(BF16)
    causal = _causal(chunk)
    per_head = []
    for h in range(heads):
        dk = slice(h * GLA_DK, (h + 1) * GLA_DK)
        dv = slice(h * GLA_DV, (h + 1) * GLA_DV)
        k = k_ref[:, dk]
        cum, q_dec, k_inv = _gla_gate_and_decay(q_ref[:, dk], k, glow, wg2_ref[:, dk], bg_ref[:, dk], chunk)
        per_head.append((dv, cum, k, q_dec, k_inv, v_ref[:, dv].astype(BF16), []))
    for c in range(t_rows // chunk):
        lo, hi = c * chunk, (c + 1) * chunk
        for h, (dv, cum, k, q_dec, k_inv, v, outs) in enumerate(per_head):
            cum_c = cum[lo:hi]
            last = cum_c[chunk - 1:chunk, :]
            k_last = (k[lo:hi] * jnp.exp(last - cum_c)).astype(BF16)
            qd, ki, vc = q_dec[lo:hi], k_inv[lo:hi], v[lo:hi]
            attn = jnp.where(causal, _dot_nt(qd, ki), 0.0).astype(BF16)
            st = st_ref[h]
            outs.append(_dot(attn, vc) + _dot_nt(qd, st.astype(BF16)))
            st_ref[h] = st * jnp.exp(last) + _dot_tn(vc, k_last)
    for h, (dv, _, _, _, _, _, outs) in enumerate(per_head):
        o = jnp.concatenate(outs, axis=0)
        o_ref[:, dv] = _gla_finish(o, r_ref[:, dv], nw_ref[...]).astype(o_ref.dtype)

    @pl.when(t == pl.num_programs(2) - 1)
    def _():
        for h in range(heads):
            sout_ref[0, h] = st_ref[h].T


def _gla_prompt(proj, wg2p, bg, norm_w, batch, seq, t_rows):
    nt = seq // t_rows
    hs = GLA_HEADS_PER_STEP
    dk, dv = hs * GLA_DK, hs * GLA_DV
    row = lambda b, h, t: b * nt + t
    in_specs = [
        pl.BlockSpec((t_rows, dk), lambda b, h, t: (row(b, h, t), h)),
        pl.BlockSpec((t_rows, dk), lambda b, h, t: (row(b, h, t), GLA_QK // dk + h)),
        pl.BlockSpec((t_rows, dv), lambda b, h, t: (row(b, h, t), 2 * GLA_QK // dv + h)),
        pl.BlockSpec((t_rows, dv), lambda b, h, t: (row(b, h, t), (2 * GLA_QK + GLA_VD) // dv + h)),
        pl.BlockSpec((t_rows, GLA_GLOW_PAD), lambda b, h, t: (row(b, h, t), GLA_PROJ // GLA_GLOW_PAD)),
        pl.BlockSpec((GLA_GLOW_PAD, dk), lambda b, h, t: (0, h)),
        pl.BlockSpec((1, dk), lambda b, h, t: (0, h)),
        pl.BlockSpec((1, GLA_DV), lambda b, h, t: (0, 0)),
    ]
    return pl.pallas_call(
        functools.partial(_gla_prompt_kernel, t_rows=t_rows, chunk=GLA_CHUNK, heads=hs),
        grid=(batch, GLA_HEADS // hs, nt),
        in_specs=in_specs,
        out_specs=[pl.BlockSpec((t_rows, dv), lambda b, h, t: (row(b, h, t), h)),
                   pl.BlockSpec((1, hs, GLA_DK, GLA_DV), lambda b, h, t: (b, h, 0, 0))],
        out_shape=[jax.ShapeDtypeStruct((batch * seq, GLA_VD), BF16),
                   jax.ShapeDtypeStruct((batch, GLA_HEADS, GLA_DK, GLA_DV), F32)],
        scratch_shapes=[pltpu.VMEM((hs, GLA_DV, GLA_DK), F32)],
        compiler_params=_cparams("arbitrary", "arbitrary", "arbitrary"),
        name="gla_prompt",
    )(proj, proj, proj, proj, proj, wg2p, bg, norm_w)


def _gla_sample_kernel(q_ref, k_ref, v_ref, r_ref, gl_ref, wg2_ref, bg_ref, nw_ref, s0_ref,
                       o_ref, sout_ref, *, seq, nb, n_real):
    g = pl.program_id(1)

    @pl.when(g < n_real)
    def _():
        k = k_ref[...]
        cum, q_dec, k_inv = _gla_gate_and_decay(q_ref[...], k, gl_ref[...].astype(BF16), wg2_ref[...],
                                                bg_ref[...], seq)
        v = v_ref[...].astype(BF16)
        causal = _causal(seq)
        outs = []
        for b in range(nb):
            lo, hi = b * seq, (b + 1) * seq
            cum_b = cum[lo:hi]
            last = cum_b[seq - 1:seq, :]
            k_last = (k[lo:hi] * jnp.exp(last - cum_b)).astype(BF16)
            qd, ki, vc = q_dec[lo:hi], k_inv[lo:hi], v[lo:hi]
            attn = jnp.where(causal, _dot_nt(qd, ki), 0.0).astype(BF16)
            s = s0_ref[b]
            outs.append(_dot(attn, vc) + _dot(qd, s.astype(BF16)))
            decay = jnp.transpose(jnp.broadcast_to(jnp.exp(last), (128, GLA_DK)))[:, 0:1]
            sout_ref[b] = s * decay + _dot_tn(k_last, vc)
        o = jnp.concatenate(outs, axis=0)
        o_ref[...] = _gla_finish(o, r_ref[...], nw_ref[...])

    @pl.when(g >= n_real)
    def _():
        o_ref[...] = jnp.zeros_like(o_ref)


def _gla_sample(proj, row0, wg2p, bg, norm_w, state, layer, batch, seq, nb):
    rows = nb * seq
    n_real = batch // nb
    blk0 = row0 // rows
    rb = lambda g: blk0 + jnp.minimum(g, n_real - 1)
    sb = lambda g: jnp.minimum(g, n_real - 1)
    in_specs = [
        pl.BlockSpec((rows, GLA_DK), lambda h, g: (rb(g), h)),
        pl.BlockSpec((rows, GLA_DK), lambda h, g: (rb(g), GLA_HEADS + h)),
        pl.BlockSpec((rows, GLA_DV), lambda h, g: (rb(g), 2 * GLA_QK // GLA_DV + h)),
        pl.BlockSpec((rows, GLA_DV), lambda h, g: (rb(g), (2 * GLA_QK + GLA_VD) // GLA_DV + h)),
        pl.BlockSpec((rows, GLA_GLOW_PAD), lambda h, g: (rb(g), GLA_PROJ // GLA_GLOW_PAD)),
        pl.BlockSpec((GLA_GLOW_PAD, GLA_DK), lambda h, g: (0, h)),
        pl.BlockSpec((1, GLA_DK), lambda h, g: (0, h)),
        pl.BlockSpec((1, GLA_DV), lambda h, g: (0, 0)),
        pl.BlockSpec((None, nb, None, GLA_DK, GLA_DV), lambda h, g: (layer, sb(g), h, 0, 0)),
    ]
    return pl.pallas_call(
        functools.partial(_gla_sample_kernel, seq=seq, nb=nb, n_real=n_real),
        grid=(GLA_HEADS, SAMPLE_ROWS // rows),
        in_specs=in_specs,
        out_specs=[pl.BlockSpec((rows, GLA_DV), lambda h, g: (g, h)),
                   pl.BlockSpec((nb, None, GLA_DK, GLA_DV), lambda h, g: (sb(g), h, 0, 0))],
        out_shape=[jax.ShapeDtypeStruct((SAMPLE_ROWS, GLA_VD), F32),
                   jax.ShapeDtypeStruct((batch, GLA_HEADS, GLA_DK, GLA_DV), F32)],
        compiler_params=_cparams("arbitrary", "arbitrary"),
        name="gla_sample",
    )(proj, proj, proj, proj, proj, wg2p, bg, norm_w, state)


def _residual_ln_store(y, res_ref, g_ref, b_ref, of_ref, ob_ref):
    o = _layer_norm(ALPHA * res_ref[...] + y, g_ref[...], b_ref[...])
    of_ref[...] = o
    ob_ref[...] = o.astype(BF16)


def _zero_outputs(of_ref, ob_ref):
    of_ref[...] = jnp.zeros_like(of_ref)
    ob_ref[...] = jnp.zeros_like(ob_ref)


def _deferred_projection(i, n_prompt_tiles, first, a_prompt, a_sample, wb_ref, y_ref,
                         epilogue, sample_epilogue, zero_fill):
    def project(a):
        y_ref[...] = _dot(a(), wb_ref[...])

    @pl.when(i == 0)
    def _():
        first()
        project(a_prompt)

    @pl.when((i > 0) & (i < n_prompt_tiles))
    def _():
        epilogue()
        project(a_prompt)

    @pl.when(i == n_prompt_tiles)
    def _():
        epilogue()
        project(a_sample)

    pl.when(i == n_prompt_tiles + 1)(sample_epilogue)
    if zero_fill is not None:
        pl.when(i > n_prompt_tiles + 1)(zero_fill)


def _mixer_out_kernel(*refs, has_bias, n_prompt_tiles):
    if has_bias:
        ap_ref, as_ref, w_ref, bias_ref, res_ref, g_ref, b_ref, of_ref, ob_ref, wb_ref, y_ref = refs
    else:
        ap_ref, as_ref, w_ref, res_ref, g_ref, b_ref, of_ref, ob_ref, wb_ref, y_ref = refs

    def first():
        wb_ref[...] = w_ref[...].astype(BF16)

    def epilogue():
        y = y_ref[...] + bias_ref[...] if has_bias else y_ref[...]
        _residual_ln_store(y, res_ref, g_ref, b_ref, of_ref, ob_ref)

    _deferred_projection(pl.program_id(0), n_prompt_tiles, first, lambda: ap_ref[...],
                         lambda: as_ref[...].astype(BF16), wb_ref, y_ref, epilogue, epilogue,
                         lambda: _zero_outputs(of_ref, ob_ref))


def _mixer_out(a_prompt, a_sample, w, layer, bias, res, g, b, name):
    m = res.shape[0]
    k = a_prompt.shape[1]
    tm = TM_OUT
    n_prompt_tiles = a_prompt.shape[0] // tm
    assert a_sample.shape[0] == tm
    has_bias = bias is not None
    vec = pl.BlockSpec((1, D_MODEL), lambda i: (0, 0))
    lagged = pl.BlockSpec((tm, D_MODEL), lambda i: (_lag(i), 0))
    in_specs = [pl.BlockSpec((tm, k), lambda i: (jnp.minimum(i, n_prompt_tiles - 1), 0)),
                pl.BlockSpec((tm, k), lambda i: (0, 0)),
                _single_buffered((None, k, D_MODEL), lambda i: (layer, 0, 0))]
    args = [a_prompt, a_sample, w]
    if has_bias:
        in_specs.append(vec)
        args.append(bias)
    in_specs += [lagged, vec, vec]
    args += [res, g, b]
    return pl.pallas_call(
        functools.partial(_mixer_out_kernel, has_bias=has_bias, n_prompt_tiles=n_prompt_tiles),
        grid=(m // tm + 1,),
        in_specs=in_specs,
        out_specs=[lagged, lagged],
        out_shape=[jax.ShapeDtypeStruct((m, D_MODEL), F32),
                   jax.ShapeDtypeStruct((m, D_MODEL), BF16)],
        scratch_shapes=[pltpu.VMEM((k, D_MODEL), BF16), pltpu.VMEM((tm, D_MODEL), F32)],
        compiler_params=_cparams("arbitrary"),
        name=name,
    )(*args)


def _ffn_out_kernel(a_ref, w_ref, res_ref, g_ref, b_ref, of_ref, ob_ref, y_ref, *, n_prompt_tiles):
    def epilogue():
        _residual_ln_store(y_ref[...], res_ref, g_ref, b_ref, of_ref, ob_ref)

    _deferred_projection(pl.program_id(0), n_prompt_tiles, lambda: None, lambda: a_ref[...], lambda: a_ref[...],
                         w_ref, y_ref, epilogue, epilogue, lambda: _zero_outputs(of_ref, ob_ref))


def _ffn_out_final_kernel(a_ref, w_ref, res_ref, g_ref, b_ref, op_ref, os_ref, y_ref, *, n_prompt_tiles):
    def ln():
        return _layer_norm(ALPHA * res_ref[...] + y_ref[...], g_ref[...], b_ref[...])

    def epilogue():
        op_ref[...] = ln()

    def sample_epilogue():
        os_ref[...] = ln()

    _deferred_projection(pl.program_id(0), n_prompt_tiles, lambda: None, lambda: a_ref[...], lambda: a_ref[...],
                         w_ref, y_ref, epilogue, sample_epilogue, None)


def _ffn_out(a, w, layer, res, g, b, n_prompt_tiles, final):
    m, k = a.shape
    tm = TM_OUT
    vec = pl.BlockSpec((1, D_MODEL), lambda i: (0, 0))
    lagged = pl.BlockSpec((tm, D_MODEL), lambda i: (_lag(i), 0))
    in_specs = [pl.BlockSpec((tm, k), lambda i: (jnp.minimum(i, n_prompt_tiles), 0)),
                _single_buffered((None, k, D_MODEL), lambda i: (layer, 0, 0)),
                lagged, vec, vec]
    scratch = [pltpu.VMEM((tm, D_MODEL), F32)]
    if final:
        return pl.pallas_call(
            functools.partial(_ffn_out_final_kernel, n_prompt_tiles=n_prompt_tiles),
            grid=(n_prompt_tiles + 2,),
            in_specs=in_specs,
            out_specs=[pl.BlockSpec((tm, D_MODEL), lambda i: (jnp.minimum(_lag(i), n_prompt_tiles - 1), 0)),
                       pl.BlockSpec((tm, D_MODEL), lambda i: (0, 0))],
            out_shape=[jax.ShapeDtypeStruct((n_prompt_tiles * tm, D_MODEL), F32),
                       jax.ShapeDtypeStruct((tm, D_MODEL), F32)],
            scratch_shapes=scratch,
            compiler_params=_cparams("arbitrary"),
            name="ffn_out_final",
        )(a, w, res, g, b)
    return pl.pallas_call(
        functools.partial(_ffn_out_kernel, n_prompt_tiles=n_prompt_tiles),
        grid=(m // tm + 1,),
        in_specs=in_specs,
        out_specs=[lagged, lagged],
        out_shape=[jax.ShapeDtypeStruct((m, D_MODEL), F32),
                   jax.ShapeDtypeStruct((m, D_MODEL), BF16)],
        scratch_shapes=scratch,
        compiler_params=_cparams("arbitrary"),
        name="ffn_out",
    )(a, w, res, g, b)


def _conv_glu(g, val, g1, g2, cw, cb):
    conv = cb + cw[0:1, :] * g2
    conv = conv + cw[1:2, :] * g1
    conv = conv + cw[2:3, :] * g
    return (_gelu(conv) * val).astype(BF16)


def _ffn_in_kernel(x_ref, wg_ref, wv_ref, cw_ref, cb_ref, p1_ref, p2_ref, u_ref, glp_ref, gls_ref,
                   wgb_ref, wvb_ref, carry_ref, gs_ref, vs_ref, *, seq_tiles, n_tiles, sample_seq):
    i = pl.program_id(1)
    tm, tn = u_ref.shape
    col_slices = [slice(c0, c0 + MXU_COLS) for c0 in range(0, tn, MXU_COLS)]

    def dots(rows):
        x = x_ref[0:rows, :]
        for cs in col_slices:
            gs_ref[0:rows, cs] = _dot(x, wgb_ref[:, cs])
            vs_ref[0:rows, cs] = _dot(x, wvb_ref[:, cs])

    def prompt_epilogue():
        row = lax.broadcasted_iota(jnp.int32, (tm, MXU_COLS), 0)
        first = (i - 1) % seq_tiles == 0
        for cs in col_slices:
            g, val = gs_ref[:, cs], vs_ref[:, cs]
            above = jnp.where(first, 0.0, carry_ref[:, cs])
            g1 = jnp.where(row >= 1, pltpu.roll(g, 1, axis=0), above[7:8, :])
            g2 = jnp.where(row >= 2, pltpu.roll(g, 2, axis=0),
                           jnp.where(row == 0, above[6:7, :], above[7:8, :]))
            u_ref[:, cs] = _conv_glu(g, val, g1, g2, cw_ref[:, cs], cb_ref[:, cs])
            carry_ref[:, cs] = g[tm - 8:tm, :]
            glp_ref[:, cs] = g[tm - 8:tm, :]

    @pl.when(i == 0)
    def _():
        wgb_ref[...] = wg_ref[...].astype(BF16)
        wvb_ref[...] = wv_ref[...].astype(BF16)
        dots(tm)

    @pl.when((i > 0) & (i < n_tiles - 1))
    def _():
        prompt_epilogue()
        dots(tm)

    @pl.when(i == n_tiles - 1)
    def _():
        prompt_epilogue()
        dots(SAMPLE_ROWS)

    @pl.when(i == n_tiles)
    def _():
        rows = SAMPLE_ROWS
        pos = lax.broadcasted_iota(jnp.int32, (rows, MXU_COLS), 0) % sample_seq
        for cs in col_slices:
            g, val = gs_ref[0:rows, cs], vs_ref[0:rows, cs]
            g1 = jnp.where(pos >= 1, pltpu.roll(g, 1, axis=0), p1_ref[:, cs])
            g2 = jnp.where(pos >= 2, pltpu.roll(g, 2, axis=0), p2_ref[:, cs])
            u_ref[0:rows, cs] = _conv_glu(g, val, g1, g2, cw_ref[:, cs], cb_ref[:, cs])
            gls_ref[:, cs] = g
        u_ref[rows:tm, :] = jnp.zeros((tm - rows, tn), BF16)


def _ffn_in(xb, w_in, layer, conv_w, conv_b, prev1, prev2, n_prompt_seqs, seq, sample_seq, tn):
    m = xb.shape[0]
    nj = D_FF // tn
    seq_tiles = seq // TM
    n_tiles = m // TM
    n_prompt_tiles = n_prompt_seqs * seq_tiles
    assert n_tiles == n_prompt_tiles + 1
    done = lambda i: jnp.maximum(i - 1, 0)
    sample = pl.BlockSpec((SAMPLE_ROWS, tn), lambda j, i: (0, j))
    return pl.pallas_call(
        functools.partial(_ffn_in_kernel, seq_tiles=seq_tiles, n_tiles=n_tiles, sample_seq=sample_seq),
        grid=(nj, n_tiles + 1),
        in_specs=[pl.BlockSpec((TM, D_MODEL), lambda j, i: (jnp.minimum(i, n_tiles - 1), 0)),
                  pl.BlockSpec((None, D_MODEL, tn), lambda j, i: (layer, 0, j)),
                  pl.BlockSpec((None, D_MODEL, tn), lambda j, i: (layer, 0, nj + j)),
                  pl.BlockSpec((None, CONV_W, tn), lambda j, i: (layer, 0, j)),
                  pl.BlockSpec((None, 1, tn), lambda j, i: (layer, 0, j)),
                  sample, sample],
        out_specs=[pl.BlockSpec((TM, tn), lambda j, i: (done(i), j)),
                   pl.BlockSpec((8, tn), lambda j, i: (jnp.minimum(done(i), n_prompt_tiles - 1) // seq_tiles, j)),
                   sample],
        out_shape=[jax.ShapeDtypeStruct((m, D_FF), BF16),
                   jax.ShapeDtypeStruct((n_prompt_seqs * 8, D_FF), F32),
                   jax.ShapeDtypeStruct((SAMPLE_ROWS, D_FF), F32)],
        scratch_shapes=[pltpu.VMEM((D_MODEL, tn), BF16), pltpu.VMEM((D_MODEL, tn), BF16),
                        pltpu.VMEM((8, tn), F32), pltpu.VMEM((TM, tn), F32), pltpu.VMEM((TM, tn), F32)],
        compiler_params=_cparams("arbitrary", "arbitrary"),
        name="ffn_in",
    )(xb, w_in, w_in, conv_w, conv_b, prev1, prev2)


def _swa_qkv_kernel(x_ref, w_ref, b_ref, ca_ref, cm_ref, cp_ref, o_ref, wb_ref, y_ref,
                    *, n_rot_tiles, n_prompt_tiles, n_tiles):
    j = pl.program_id(0)
    tm, tn = o_ref.shape

    def first():
        wb_ref[...] = w_ref[...].astype(BF16)

    def dots(rows):
        x = x_ref[0:rows, :]
        for c0 in range(0, tn, MXU_COLS):
            sl = slice(c0, c0 + MXU_COLS)
            y_ref[0:rows, sl] = _dot(x, wb_ref[:, sl])

    def epilogue(rows):
        reps = MXU_COLS // ca_ref.shape[1]
        ca = jnp.concatenate([ca_ref[0:rows, :]] * reps, axis=1)
        cm = jnp.concatenate([cm_ref[0:rows, :]] * reps, axis=1)
        cp = jnp.concatenate([cp_ref[0:rows, :]] * reps, axis=1)
        lane = lax.broadcasted_iota(jnp.int32, (rows, MXU_COLS), 1) % SWA_HEAD_DIM
        rotated = lane < jnp.where(j < n_rot_tiles, ROT_DIM, 0)
        half = ROT_DIM // 2
        for c0 in range(0, tn, MXU_COLS):
            sl = slice(c0, c0 + MXU_COLS)
            y = y_ref[0:rows, sl] + b_ref[:, sl]
            rot = y * ca + pltpu.roll(y, MXU_COLS - half, axis=1) * cm + pltpu.roll(y, half, axis=1) * cp
            o_ref[0:rows, sl] = jnp.where(rotated, rot, y)
        if rows < tm:
            o_ref[rows:tm, :] = jnp.zeros((tm - rows, tn), F32)

    _deferred_row_tiles(pl.program_id(1), n_prompt_tiles, n_tiles, tm, first, dots, epilogue)


def _rope_tables(pos):
    half = ROT_DIM // 2
    inv = ROPE_THETA ** (-jnp.arange(half, dtype=F32) / half)
    ang = pos.astype(F32)[:, None] * inv
    cos, sin = jnp.cos(ang), jnp.sin(ang)
    n = pos.shape[0]
    rest = SWA_HEAD_DIM - ROT_DIM
    ca = jnp.concatenate([cos, cos, jnp.ones((n, rest), F32)], -1)
    cm = jnp.concatenate([-sin, jnp.zeros((n, half + rest), F32)], -1)
    cp = jnp.concatenate([jnp.zeros((n, half), F32), sin, jnp.zeros((n, rest), F32)], -1)
    return tuple(jnp.tile(t, (1, 128 // SWA_HEAD_DIM)) for t in (ca, cm, cp))


def _swa_qkv(xb, w, layer, b, tables, n_prompt_tiles, seq, tn):
    m = xb.shape[0]
    n = w.shape[2]
    pos_tiles = seq // TM
    n_tiles = m // TM
    tab = pl.BlockSpec((TM, 128), lambda j, i: (jnp.where(_lag(i) < n_prompt_tiles, _lag(i) % pos_tiles, pos_tiles), 0))
    return pl.pallas_call(
        functools.partial(_swa_qkv_kernel, n_rot_tiles=(SWA_QD + SWA_KD) // tn, n_prompt_tiles=n_prompt_tiles,
                          n_tiles=n_tiles),
        grid=(n // tn, n_tiles + 1),
        in_specs=[pl.BlockSpec((TM, D_MODEL), lambda j, i: (jnp.minimum(i, n_tiles - 1), 0)),
                  pl.BlockSpec((None, D_MODEL, tn), lambda j, i: (layer, 0, j)),
                  pl.BlockSpec((1, tn), lambda j, i: (0, j)),
                  tab, tab, tab],
        out_specs=pl.BlockSpec((TM, tn), lambda j, i: (_lag(i), j)),
        out_shape=jax.ShapeDtypeStruct((m, n), F32),
        scratch_shapes=[pltpu.VMEM((D_MODEL, tn), BF16), pltpu.VMEM((TM, tn), F32)],
        compiler_params=_cparams("arbitrary", "arbitrary"),
        name="swa_qkv",
    )(xb, w, b, *tables)


def _sink_softmax_pv(s, mask, sink, v):
    s = jnp.where(mask, s * (SWA_HEAD_DIM ** -0.5), -jnp.inf)
    mx = jnp.maximum(jnp.max(s, -1, keepdims=True), sink)
    p = jnp.exp(s - mx)
    denom = jnp.sum(p, -1, keepdims=True) + jnp.exp(sink - mx)
    return _dot((p / denom).astype(BF16), v)


def _swa_prompt_kernel(sink_ref, q_ref, kp_ref, kc_ref, vp_ref, vc_ref, o_ref):
    i = pl.program_id(1)
    q = q_ref[...].astype(BF16)
    kb = jnp.concatenate([kp_ref[...], kc_ref[...]], axis=0).astype(BF16)
    vb = jnp.concatenate([vp_ref[...], vc_ref[...]], axis=0).astype(BF16)
    r = lax.broadcasted_iota(jnp.int32, (WINDOW, 2 * WINDOW), 0)
    c = lax.broadcasted_iota(jnp.int32, (WINDOW, 2 * WINDOW), 1)
    mask = (c > r) & (c <= r + WINDOW) & (c >= jnp.where(i > 0, 0, WINDOW))
    pieces = []
    for h in range(SWA_KV_HEADS):
        kh = kb[:, h * SWA_HEAD_DIM:(h + 1) * SWA_HEAD_DIM]
        vh = vb[:, h * SWA_HEAD_DIM:(h + 1) * SWA_HEAD_DIM]
        for g in range(SWA_GROUP):
            hq = h * SWA_GROUP + g
            qg = q[:, hq * SWA_HEAD_DIM:(hq + 1) * SWA_HEAD_DIM]
            pieces.append(_sink_softmax_pv(_dot_nt(qg, kh), mask, sink_ref[hq], vh))
    o_ref[...] = jnp.concatenate(pieces, axis=-1).astype(o_ref.dtype)


def _swa_prompt_attn(qkv, sinks, batch, seq):
    nb = seq // WINDOW
    kcol = SWA_QD // SWA_KD
    cur = lambda b, i: b * nb + i
    prev = lambda b, i: b * nb + jnp.maximum(i - 1, 0)
    return pl.pallas_call(
        _swa_prompt_kernel,
        grid=(batch, nb),
        in_specs=[pl.BlockSpec(memory_space=pltpu.SMEM),
                  pl.BlockSpec((WINDOW, SWA_QD), lambda b, i: (cur(b, i), 0)),
                  pl.BlockSpec((WINDOW, SWA_KD), lambda b, i: (prev(b, i), kcol)),
                  pl.BlockSpec((WINDOW, SWA_KD), lambda b, i: (cur(b, i), kcol)),
                  pl.BlockSpec((WINDOW, SWA_KD), lambda b, i: (prev(b, i), kcol + 1)),
                  pl.BlockSpec((WINDOW, SWA_KD), lambda b, i: (cur(b, i), kcol + 1))],
        out_specs=pl.BlockSpec((WINDOW, SWA_QD), lambda b, i: (cur(b, i), 0)),
        out_shape=jax.ShapeDtypeStruct((batch * seq, SWA_QD), BF16),
        compiler_params=_cparams("arbitrary", "arbitrary"),
        name="swa_prompt_attn",
    )(sinks, qkv, qkv, qkv, qkv, qkv)


def _swa_sample_kernel(sink_ref, q_ref, kn_ref, vn_ref, kc_ref, vc_ref, o_ref, ko_ref, vo_ref,
                       *, seq, nb, n_real):
    step = pl.program_id(0)
    nbuf = kc_ref.shape[1]

    @pl.when(step < n_real)
    def _():
        rows = SWA_GROUP * seq
        l = lax.broadcasted_iota(jnp.int32, (rows, nbuf + seq), 0) % seq
        c = lax.broadcasted_iota(jnp.int32, (rows, nbuf + seq), 1)
        diff = l + nbuf - c
        mask = (diff >= 0) & (diff < WINDOW)
        grp = lax.broadcasted_iota(jnp.int32, (rows, 1), 0) // seq
        for b in range(nb):
            lo, hi = b * seq, (b + 1) * seq
            k_all = jnp.concatenate([kc_ref[b], kn_ref[lo:hi, :]], axis=0)
            v_all = jnp.concatenate([vc_ref[b], vn_ref[lo:hi, :]], axis=0)
            ko_ref[b] = k_all[seq:, :]
            vo_ref[b] = v_all[seq:, :]
            kb = k_all.astype(BF16)
            vb = v_all.astype(BF16)
            q = q_ref[lo:hi, :]
            pieces = []
            for h in range(SWA_KV_HEADS):
                kh = kb[:, h * SWA_HEAD_DIM:(h + 1) * SWA_HEAD_DIM]
                vh = vb[:, h * SWA_HEAD_DIM:(h + 1) * SWA_HEAD_DIM]
                qs = jnp.concatenate(
                    [q[:, (h * SWA_GROUP + g) * SWA_HEAD_DIM:(h * SWA_GROUP + g + 1) * SWA_HEAD_DIM]
                     for g in range(SWA_GROUP)], axis=0).astype(BF16)
                sink = jnp.zeros((rows, 1), F32)
                for g in range(SWA_GROUP):
                    sink = jnp.where(grp == g, sink_ref[h * SWA_GROUP + g], sink)
                o = _sink_softmax_pv(_dot_nt(qs, kh), mask, sink, vh)
                pieces += [o[g * seq:(g + 1) * seq, :] for g in range(SWA_GROUP)]
            o_ref[lo:hi, :] = jnp.concatenate(pieces, axis=-1)

    @pl.when(step >= n_real)
    def _():
        o_ref[...] = jnp.zeros_like(o_ref)


def _swa_sample_attn(qkv, row0, sinks, k_cache, v_cache, batch, seq, nb):
    nbuf = k_cache.shape[1]
    kcol = SWA_QD // SWA_KD
    rows = nb * seq
    n_real = batch // nb
    blk0 = row0 // rows
    rb = lambda s: blk0 + jnp.minimum(s, n_real - 1)
    cache = pl.BlockSpec((nb, nbuf, SWA_KD), lambda s: (jnp.minimum(s, n_real - 1), 0, 0))
    return pl.pallas_call(
        functools.partial(_swa_sample_kernel, seq=seq, nb=nb, n_real=n_real),
        grid=(SAMPLE_ROWS // rows,),
        in_specs=[pl.BlockSpec(memory_space=pltpu.SMEM),
                  pl.BlockSpec((rows, SWA_QD), lambda s: (rb(s), 0)),
                  pl.BlockSpec((rows, SWA_KD), lambda s: (rb(s), kcol)),
                  pl.BlockSpec((rows, SWA_KD), lambda s: (rb(s), kcol + 1)),
                  cache, cache],
        out_specs=[pl.BlockSpec((rows, SWA_QD), lambda s: (s, 0)), cache, cache],
        out_shape=[jax.ShapeDtypeStruct((SAMPLE_ROWS, SWA_QD), F32),
                   jax.ShapeDtypeStruct((batch, nbuf, SWA_KD), F32),
                   jax.ShapeDtypeStruct((batch, nbuf, SWA_KD), F32)],
        compiler_params=_cparams("arbitrary"),
        name="swa_sample_attn",
    )(sinks, qkv, qkv, qkv, k_cache, v_cache)


def _sg_in_kernel(x_ref, w_ref, b_ref, g_ref, beta_ref, o_ref, wb_ref, y_ref, *, n_prompt_tiles, n_tiles):
    j = pl.program_id(0)
    tm = o_ref.shape[0]

    def first():
        wb_ref[...] = w_ref[...].astype(BF16)

    def dots(rows):
        y_ref[0:rows, :] = _dot(x_ref[0:rows, :], wb_ref[...])

    def epilogue(rows, normalise):
        z = _gelu(y_ref[0:rows, :] + b_ref[...])
        o_ref[0:rows, :] = _layer_norm(z, g_ref[...], beta_ref[...]) if normalise else z
        if rows < tm:
            o_ref[rows:tm, :] = jnp.zeros((tm - rows, o_ref.shape[1]), F32)

    for half, normalise in ((0, False), (1, True)):
        pl.when(j == half)(functools.partial(
            _deferred_row_tiles, pl.program_id(1), n_prompt_tiles, n_tiles, tm, first, dots,
            functools.partial(epilogue, normalise=normalise)))


def _sg_in(xb, w, layer, b, ln_g, ln_b, n_prompt_tiles):
    m = xb.shape[0]
    tm = TM_SG_IN
    n_tiles = m // tm
    vec = pl.BlockSpec((1, SG_WIDTH), lambda j, i: (0, 0))
    return pl.pallas_call(
        functools.partial(_sg_in_kernel, n_prompt_tiles=n_prompt_tiles, n_tiles=n_tiles),
        grid=(2, n_tiles + 1),
        in_specs=[pl.BlockSpec((tm, D_MODEL), lambda j, i: (jnp.minimum(i, n_tiles - 1), 0)),
                  _single_buffered((None, D_MODEL, SG_WIDTH), lambda j, i: (layer, 0, j)),
                  pl.BlockSpec((1, SG_WIDTH), lambda j, i: (0, j)),
                  vec, vec],
        out_specs=pl.BlockSpec((None, tm, SG_WIDTH), lambda j, i: (j, _lag(i), 0)),
        out_shape=jax.ShapeDtypeStruct((2, m, SG_WIDTH), F32),
        scratch_shapes=[pltpu.VMEM((D_MODEL, SG_WIDTH), BF16), pltpu.VMEM((tm, SG_WIDTH), F32)],
        compiler_params=_cparams("arbitrary", "arbitrary"),
        name="sg_in",
    )(xb, w, b, ln_g, ln_b)


def _sg_mix(u_ref, v_ref, ws_ref, bs_ref, chunk, period):
    tm = u_ref.shape[0]
    r = lax.broadcasted_iota(jnp.int32, (chunk, chunk), 0)
    c = lax.broadcasted_iota(jnp.int32, (chunk, chunk), 1)
    mask = (c <= r) & (r // period == c // period)
    ws = [jnp.where(mask, ws_ref[g], 0.0).astype(BF16) for g in range(SG_GROUPS)]
    rows = []
    for n in range(tm // chunk):
        lo, hi = n * chunk, (n + 1) * chunk
        cols = []
        for g in range(SG_GROUPS):
            vg = v_ref[lo:hi, g * SG_GW:(g + 1) * SG_GW].astype(BF16)
            mixed = _dot(ws[g], vg) + bs_ref[:, g:g + 1]
            cols.append((u_ref[lo:hi, g * SG_GW:(g + 1) * SG_GW] * mixed).astype(BF16))
        rows.append(jnp.concatenate(cols, axis=-1))
    return jnp.concatenate(rows, axis=0)


def _sg_out_kernel(u_ref, v_ref, wsp_ref, bsp_ref, wss_ref, bss_ref, w_ref, bias_ref, res_ref, g_ref, b_ref,
                   of_ref, ob_ref, wb_ref, y_ref, *, n_prompt_tiles, sample_seq):
    def first():
        wb_ref[...] = w_ref[...].astype(BF16)

    def epilogue():
        _residual_ln_store(y_ref[...] + bias_ref[...], res_ref, g_ref, b_ref, of_ref, ob_ref)

    _deferred_projection(pl.program_id(0), n_prompt_tiles, first,
                         lambda: _sg_mix(u_ref, v_ref, wsp_ref, bsp_ref, SG_CHUNK, SG_CHUNK),
                         lambda: _sg_mix(u_ref, v_ref, wss_ref, bss_ref, wss_ref.shape[1], sample_seq),
                         wb_ref, y_ref, epilogue, epilogue, lambda: _zero_outputs(of_ref, ob_ref))


def _sg_out(uv, ws_p, bs_p, ws_s, bs_s, w_out, layer, b_out, res, g, b, n_prompt_tiles, sample_seq):
    m = res.shape[0]
    tm = TM_OUT
    vec = pl.BlockSpec((1, D_MODEL), lambda i: (0, 0))
    full = lambda a: pl.BlockSpec(a.shape, lambda i: (0,) * a.ndim)
    lagged = pl.BlockSpec((tm, D_MODEL), lambda i: (_lag(i), 0))
    return pl.pallas_call(
        functools.partial(_sg_out_kernel, n_prompt_tiles=n_prompt_tiles, sample_seq=sample_seq),
        grid=(m // tm + 1,),
        in_specs=[pl.BlockSpec((None, tm, SG_WIDTH), lambda i: (0, jnp.minimum(i, n_prompt_tiles), 0)),
                  pl.BlockSpec((None, tm, SG_WIDTH), lambda i: (1, jnp.minimum(i, n_prompt_tiles), 0)),
                  full(ws_p), full(bs_p), full(ws_s), full(bs_s),
                  _single_buffered((None, SG_WIDTH, D_MODEL), lambda i: (layer, 0, 0)),
                  vec, lagged, vec, vec],
        out_specs=[lagged, lagged],
        out_shape=[jax.ShapeDtypeStruct((m, D_MODEL), F32),
                   jax.ShapeDtypeStruct((m, D_MODEL), BF16)],
        scratch_shapes=[pltpu.VMEM((SG_WIDTH, D_MODEL), BF16), pltpu.VMEM((tm, D_MODEL), F32)],
        compiler_params=_cparams("arbitrary"),
        name="sg_out",
    )(uv, uv, ws_p, bs_p, ws_s, bs_s, w_out, b_out, res, g, b)


def kernel(x_prompt, x_sample, state_gla, cache_swa_k, cache_swa_v, state_ffn_conv, ln_mix_g, ln_mix_b, ln_ffn_g, ln_ffn_b, gla_w_in, gla_w_g2, gla_b_g, gla_norm_w, gla_w_out, swa_w_qkv, swa_b_qkv, swa_sinks, swa_w_out, swa_b_out, sg_w_in, sg_b_in, sg_ln_g, sg_ln_b, sg_w_s, sg_b_s, sg_w_out, sg_b_out, ffn_w_in, ffn_conv_w, ffn_conv_b, ffn_w_out):
    bp, lp, _ = x_prompt.shape
    bs, ls, _ = x_sample.shape
    mp, ms = bp * lp, bs * ls
    assert lp % TM == 0 and ms <= SAMPLE_ROWS and SAMPLE_ROWS % ls == 0 and ls >= CONV_W - 1
    assert SAMPLE_ROWS == TM_OUT and SAMPLE_PAD % TM_SG_IN == 0

    xf = jnp.concatenate([x_prompt.reshape(mp, D_MODEL), x_sample.reshape(ms, D_MODEL),
                          jnp.zeros((SAMPLE_PAD - ms, D_MODEL), F32)], axis=0)
    xb = xf.astype(BF16)
    gla_w_proj = jnp.pad(gla_w_in.astype(BF16), ((0, 0), (0, 0), (0, GLA_PROJ_PAD - gla_w_in.shape[2])))
    gla_wg2p = jnp.pad(gla_w_g2.astype(BF16), ((0, 0), (0, GLA_GLOW_PAD - GLA_RANK), (0, 0)))
    ffn_w_out_b = ffn_w_out.astype(BF16)
    conv_b3 = ffn_conv_b.reshape(DEPTH, 1, D_FF)

    gla_p, gla_s, swk_p, swv_p, swk_s, swv_s, sgv_s, conv_p, conv_s = ([] for _ in range(9))
    for i in range(DEPTH):
        j = i // N_MIXERS
        kind = i % N_MIXERS
        ln_g, ln_b = _row(ln_mix_g[i]), _row(ln_mix_b[i])
        if kind == 0:
            proj = _gla_proj(xb, gla_w_proj, j, mp // TM)
            bg, nw = _row(gla_b_g[j]), _row(gla_norm_w[j])
            og_p, st_p = _gla_prompt(proj, gla_wg2p[j], bg, nw, bp, lp, 256)
            og_s, st_s = _gla_sample(proj, mp, gla_wg2p[j], bg, nw, state_gla, j, bs, ls, 8 if bs % 8 == 0 else 1)
            gla_p.append(st_p)
            gla_s.append(st_s)
            xf, xb = _mixer_out(og_p, og_s, gla_w_out, j, None, xf, ln_g, ln_b, "gla_out")
        elif kind == 1:
            pos = jnp.concatenate([jnp.arange(lp), PAST_LEN + jnp.arange(TM) % ls])
            qkv = _swa_qkv(xb, swa_w_qkv, j, _row(swa_b_qkv[j]), _rope_tables(pos), mp // TM, lp, 512)
            sinks = swa_sinks[j].reshape(-1)
            oa_p = _swa_prompt_attn(qkv, sinks, bp, lp)
            kv = jnp.stack([qkv[(b + 1) * lp - WINDOW:(b + 1) * lp, SWA_QD:] for b in range(bp)])
            swk_p.append(kv[..., :SWA_KD].reshape(bp, WINDOW, SWA_KV_HEADS, SWA_HEAD_DIM))
            swv_p.append(kv[..., SWA_KD:].reshape(bp, WINDOW, SWA_KV_HEADS, SWA_HEAD_DIM))
            nbuf = cache_swa_k.shape[2]
            oa_s, ko, vo = _swa_sample_attn(qkv, mp, sinks, cache_swa_k[j].reshape(bs, nbuf, SWA_KD),
                                            cache_swa_v[j].reshape(bs, nbuf, SWA_KD), bs, ls,
                                            2 if bs % 2 == 0 else 1)
            swk_s.append(ko.reshape(bs, nbuf, SWA_KV_HEADS, SWA_HEAD_DIM))
            swv_s.append(vo.reshape(bs, nbuf, SWA_KV_HEADS, SWA_HEAD_DIM))
            xf, xb = _mixer_out(oa_p, oa_s, swa_w_out, j, _row(swa_b_out[j]), xf, ln_g, ln_b, "swa_out")
        else:
            uv = _sg_in(xb, sg_w_in, j, _row(sg_b_in[j]), _row(sg_ln_g[j]), _row(sg_ln_b[j]), mp // TM_SG_IN)
            sgv_s.append(uv[1, mp:mp + ms].reshape(bs, ls, SG_WIDTH))
            reps = TM_OUT // ls
            ws_s = jnp.tile(sg_w_s[j][:, :ls, :ls], (1, reps, reps))
            bs_s = jnp.tile(sg_b_s[j][:, :ls].T, (reps, 1))
            xf, xb = _sg_out(uv, sg_w_s[j], sg_b_s[j].T, ws_s, bs_s, sg_w_out, j, _row(sg_b_out[j]),
                             xf, ln_g, ln_b, mp // TM_OUT, ls)
        st = state_ffn_conv[i]
        prev1 = jnp.pad(st[:, 1:2], ((0, 0), (0, ls - 1), (0, 0))).reshape(ms, D_FF)
        prev2 = jnp.pad(st, ((0, 0), (0, ls - 2), (0, 0))).reshape(ms, D_FF)
        prev1 = jnp.pad(prev1, ((0, SAMPLE_ROWS - ms), (0, 0)))
        prev2 = jnp.pad(prev2, ((0, SAMPLE_ROWS - ms), (0, 0)))
        u, gl_p, gl_s = _ffn_in(xb, ffn_w_in, i, ffn_conv_w, conv_b3, prev1, prev2, bp, lp, ls, 512)
        conv_p.append(gl_p.reshape(bp, 8, D_FF)[:, 8 - (CONV_W - 1):])
        conv_s.append(gl_s[:ms].reshape(bs, ls, D_FF)[:, ls - (CONV_W - 1):])
        final = i == DEPTH - 1
        outs = _ffn_out(u, ffn_w_out_b, i, xf, _row(ln_ffn_g[i]), _row(ln_ffn_b[i]), mp // TM_OUT, final)
        if final:
            y_prompt_rows, y_sample_tile = outs
        else:
            xf, xb = outs

    yp = y_prompt_rows.reshape(bp, lp, D_MODEL)
    ys = y_sample_tile[:ms].reshape(bs, ls, D_MODEL)
    return (yp, ys, jnp.stack(gla_p), jnp.stack(gla_s), jnp.stack(swk_p), jnp.stack(swv_p),
            jnp.stack(swk_s), jnp.stack(swv_s), jnp.stack(sgv_s), jnp.stack(conv_p), jnp.stack(conv_s))
```

```python
import functools

import jax
import jax.numpy as jnp
from jax import lax
from jax.experimental import pallas as pl
from jax.experimental.pallas import tpu as pltpu

F32 = jnp.float32
BF16 = jnp.bfloat16

D_MODEL = 2048
DEPTH = 4
PAST_LEN = 16384
N_MIXERS = 3
ALPHA = (2 * DEPTH) ** 0.25
LN_EPS = 1e-5

GLA_HEADS = 4
GLA_DK = 256
GLA_DV = 512
GLA_RANK = 16
GLA_TAU = 16.0
GLA_CHUNK = 64
GLA_QK = GLA_HEADS * GLA_DK
GLA_VD = GLA_HEADS * GLA_DV
GLA_PROJ = 2 * GLA_QK + 2 * GLA_VD
GLA_GLOW_PAD = 128
GLA_PROJ_TN = 1280
GLA_PROJ_PAD = 5 * GLA_PROJ_TN
GLA_HEADS_PER_STEP = 4

SWA_HEAD_DIM = 64
SWA_Q_HEADS = 32
SWA_KV_HEADS = 8
SWA_GROUP = 4
WINDOW = 128
ROT_DIM = 16
ROPE_THETA = 500000.0
SWA_QD = SWA_Q_HEADS * SWA_HEAD_DIM
SWA_KD = SWA_KV_HEADS * SWA_HEAD_DIM

SG_WIDTH = 2048
SG_GROUPS = 4
SG_GW = SG_WIDTH // SG_GROUPS
SG_CHUNK = 128

D_FF = 5632
CONV_W = 3

TM = 1024
TM_SG_IN = 512
TM_OUT = 256
SAMPLE_PAD = TM
SAMPLE_ROWS = 256
MXU_COLS = 256
VMEM_LIMIT_BYTES = 56 * 1024 * 1024


def _cparams(*sem):
    return pltpu.CompilerParams(dimension_semantics=sem, vmem_limit_bytes=VMEM_LIMIT_BYTES)


def _dot(a, b):
    return jnp.dot(a, b, preferred_element_type=F32)


def _dot_nt(a, b):
    return lax.dot_general(a, b, (((1,), (1,)), ((), ())), preferred_element_type=F32)


def _dot_tn(a, b):
    return lax.dot_general(a, b, (((0,), (0,)), ((), ())), preferred_element_type=F32)


def _layer_norm(x, g, b):
    mu = jnp.mean(x, -1, keepdims=True)
    xc = x - mu
    var = jnp.mean(xc * xc, -1, keepdims=True)
    return xc * lax.rsqrt(var + LN_EPS) * g + b


def _gelu(x):
    return 0.5 * x * (1.0 + lax.erf(x * (0.5 ** 0.5)))


def _log_sigmoid(x):
    return jnp.minimum(x, 0.0) - jnp.log1p(jnp.exp(-jnp.abs(x)))


def _cumsum_rows(x, period):
    rowmod = lax.broadcasted_iota(jnp.int32, x.shape, 0) % period
    s = 1
    while s < period:
        x = x + jnp.where(rowmod >= s, pltpu.roll(x, s, axis=0), 0.0)
        s *= 2
    return x


def _single_buffered(shape, index_map):
    return pl.BlockSpec(shape, index_map, pipeline_mode=pl.Buffered(1))


def _row(v):
    return v.reshape(1, -1)


def _per_row_tile(i, n_prompt_tiles, tm, body):
    pl.when(i < n_prompt_tiles)(lambda: body(tm))
    pl.when(i >= n_prompt_tiles)(lambda: body(min(tm, SAMPLE_ROWS)))


def _lag(i):
    return jnp.maximum(i - 1, 0)


def _deferred_row_tiles(i, n_prompt_tiles, n_tiles, tm, first, dots, epilogue):
    s = min(tm, SAMPLE_ROWS)

    @pl.when(i == 0)
    def _():
        first()
        dots(tm)

    @pl.when((i > 0) & (i < n_prompt_tiles))
    def _():
        epilogue(tm)
        dots(tm)

    @pl.when(i == n_prompt_tiles)
    def _():
        epilogue(tm)
        dots(s)

    if n_tiles > n_prompt_tiles + 1:
        @pl.when((i > n_prompt_tiles) & (i < n_tiles))
        def _():
            epilogue(s)
            dots(s)

    pl.when(i == n_tiles)(lambda: epilogue(s))


def _gla_proj_kernel(x_ref, w_ref, o_ref, *, n_prompt_tiles):
    tm = o_ref.shape[0]

    def body(rows):
        o_ref[0:rows, :] = _dot(x_ref[0:rows, :], w_ref[...])
        if rows < tm:
            o_ref[rows:tm, :] = jnp.zeros((tm - rows, o_ref.shape[1]), F32)

    _per_row_tile(pl.program_id(1), n_prompt_tiles, tm, body)


def _gla_proj(xb, w_proj, layer, n_prompt_tiles):
    m, k = xb.shape
    tn = GLA_PROJ_TN
    return pl.pallas_call(
        functools.partial(_gla_proj_kernel, n_prompt_tiles=n_prompt_tiles),
        grid=(GLA_PROJ_PAD // tn, m // TM),
        in_specs=[pl.BlockSpec((TM, k), lambda j, i: (i, 0)),
                  pl.BlockSpec((None, k, tn), lambda j, i: (layer, 0, j))],
        out_specs=pl.BlockSpec((TM, tn), lambda j, i: (i, j)),
        out_shape=jax.ShapeDtypeStruct((m, GLA_PROJ_PAD), F32),
        compiler_params=_cparams("arbitrary", "arbitrary"),
        name="gla_proj",
    )(xb, w_proj)


def _gla_gate_and_decay(q, k, glow_bf16, wg2, bg, period):
    ga = _dot(glow_bf16, wg2) + bg
    cum = _cumsum_rows(_log_sigmoid(ga) * (1.0 / GLA_TAU), period)
    q_dec = (q * (GLA_DK ** -0.5) * jnp.exp(cum)).astype(BF16)
    k_inv = (k * jnp.exp(-cum)).astype(BF16)
    return cum, q_dec, k_inv


def _gla_finish(o, r, nw):
    o = o * lax.rsqrt(jnp.mean(o * o, -1, keepdims=True) + LN_EPS) * nw
    return o * (r * jax.nn.sigmoid(r))


def _causal(n):
    r = lax.broadcasted_iota(jnp.int32, (n, n), 0)
    c = lax.broadcasted_iota(jnp.int32, (n, n), 1)
    return c <= r


def _gla_prompt_kernel(q_ref, k_ref, v_ref, r_ref, gl_ref, wg2_ref, bg_ref, nw_ref,
                       o_ref, sout_ref, st_ref, *, t_rows, chunk, heads):
    t = pl.program_id(2)

    @pl.when(t == 0)
    def _():
        st_ref[...] = jnp.zeros_like(st_ref)

    glow = gl_ref[...].astype(BF16)
    causal = _causal(chunk)
    per_head = []
    for h in range(heads):
        dk = slice(h * GLA_DK, (h + 1) * GLA_DK)
        dv = slice(h * GLA_DV, (h + 1) * GLA_DV)
        k = k_ref[:, dk]
        cum, q_dec, k_inv = _gla_gate_and_decay(q_ref[:, dk], k, glow, wg2_ref[:, dk], bg_ref[:, dk], chunk)
        per_head.append((dv, cum, k, q_dec, k_inv, v_ref[:, dv].astype(BF16), []))
    for c in range(t_rows // chunk):
        lo, hi = c * chunk, (c + 1) * chunk
        for h, (dv, cum, k, q_dec, k_inv, v, outs) in enumerate(per_head):
            cum_c = cum[lo:hi]
            last = cum_c[chunk - 1:chunk, :]
            k_last = (k[lo:hi] * jnp.exp(last - cum_c)).astype(BF16)
            qd, ki, vc = q_dec[lo:hi], k_inv[lo:hi], v[lo:hi]
            attn = jnp.where(causal, _dot_nt(qd, ki), 0.0).astype(BF16)
            st = st_ref[h]
            outs.append(_dot(attn, vc) + _dot_nt(qd, st.astype(BF16)))
            st_ref[h] = st * jnp.exp(last) + _dot_tn(vc, k_last)
    for h, (dv, _, _, _, _, _, outs) in enumerate(per_head):
        o = jnp.concatenate(outs, axis=0)
        o_ref[:, dv] = _gla_finish(o, r_ref[:, dv], nw_ref[...]).astype(o_ref.dtype)

    @pl.when(t == pl.num_programs(2) - 1)
    def _():
        for h in range(heads):
            sout_ref[0, h] = st_ref[h].T


def _gla_prompt(proj, wg2p, bg, norm_w, batch, seq, t_rows):
    nt = seq // t_rows
    hs = GLA_HEADS_PER_STEP
    dk, dv = hs * GLA_DK, hs * GLA_DV
    row = lambda b, h, t: b * nt + t
    in_specs = [
        pl.BlockSpec((t_rows, dk), lambda b, h, t: (row(b, h, t), h)),
        pl.BlockSpec((t_rows, dk), lambda b, h, t: (row(b, h, t), GLA_QK // dk + h)),
        pl.BlockSpec((t_rows, dv), lambda b, h, t: (row(b, h, t), 2 * GLA_QK // dv + h)),
        pl.BlockSpec((t_rows, dv), lambda b, h, t: (row(b, h, t), (2 * GLA_QK + GLA_VD) // dv + h)),
        pl.BlockSpec((t_rows, GLA_GLOW_PAD), lambda b, h, t: (row(b, h, t), GLA_PROJ // GLA_GLOW_PAD)),
        pl.BlockSpec((GLA_GLOW_PAD, dk), lambda b, h, t: (0, h)),
        pl.BlockSpec((1, dk), lambda b, h, t: (0, h)),
        pl.BlockSpec((1, GLA_DV), lambda b, h, t: (0, 0)),
    ]
    return pl.pallas_call(
        functools.partial(_gla_prompt_kernel, t_rows=t_rows, chunk=GLA_CHUNK, heads=hs),
        grid=(batch, GLA_HEADS // hs, nt),
        in_specs=in_specs,
        out_specs=[pl.BlockSpec((t_rows, dv), lambda b, h, t: (row(b, h, t), h)),
                   pl.BlockSpec((1, hs, GLA_DK, GLA_DV), lambda b, h, t: (b, h, 0, 0))],
        out_shape=[jax.ShapeDtypeStruct((batch * seq, GLA_VD), BF16),
                   jax.ShapeDtypeStruct((batch, GLA_HEADS, GLA_DK, GLA_DV), F32)],
        scratch_shapes=[pltpu.VMEM((hs, GLA_DV, GLA_DK), F32)],
        compiler_params=_cparams("arbitrary", "arbitrary", "arbitrary"),
        name="gla_prompt",
    )(proj, proj, proj, proj, proj, wg2p, bg, norm_w)


def _gla_sample_kernel(*refs, seq, nb, n_real, n_earlier):
    q_ref, k_ref, v_ref, r_ref, gl_ref, wg2_ref, bg_ref, nw_ref, s0_ref = refs[:9]
    earlier_refs = refs[9:9 + n_earlier]
    o_ref, sout_ref = refs[9 + n_earlier:]
    new_state_ref = sout_ref.at[n_earlier] if n_earlier else sout_ref
    g = pl.program_id(1)

    @pl.when(g < n_real)
    def _():
        k = k_ref[...]
        cum, q_dec, k_inv = _gla_gate_and_decay(q_ref[...], k, gl_ref[...].astype(BF16), wg2_ref[...],
                                                bg_ref[...], seq)
        v = v_ref[...].astype(BF16)
        causal = _causal(seq)
        outs = []
        for b in range(nb):
            lo, hi = b * seq, (b + 1) * seq
            cum_b = cum[lo:hi]
            last = cum_b[seq - 1:seq, :]
            k_last = (k[lo:hi] * jnp.exp(last - cum_b)).astype(BF16)
            qd, ki, vc = q_dec[lo:hi], k_inv[lo:hi], v[lo:hi]
            attn = jnp.where(causal, _dot_nt(qd, ki), 0.0).astype(BF16)
            s = s0_ref[b]
            outs.append(_dot(attn, vc) + _dot(qd, s.astype(BF16)))
            decay = jnp.transpose(jnp.broadcast_to(jnp.exp(last), (128, GLA_DK)))[:, 0:1]
            new_state_ref[b] = s * decay + _dot_tn(k_last, vc)
        for e, earlier_ref in enumerate(earlier_refs):
            sout_ref[e] = earlier_ref[...]
        o = jnp.concatenate(outs, axis=0)
        o_ref[...] = _gla_finish(o, r_ref[...], nw_ref[...])

    @pl.when(g >= n_real)
    def _():
        o_ref[...] = jnp.zeros_like(o_ref)


def _gla_sample(proj, row0, wg2p, bg, norm_w, state, layer, batch, seq, nb, earlier=()):
    rows = nb * seq
    n_real = batch // nb
    blk0 = row0 // rows
    rb = lambda g: blk0 + jnp.minimum(g, n_real - 1)
    sb = lambda g: jnp.minimum(g, n_real - 1)
    in_specs = [
        pl.BlockSpec((rows, GLA_DK), lambda h, g: (rb(g), h)),
        pl.BlockSpec((rows, GLA_DK), lambda h, g: (rb(g), GLA_HEADS + h)),
        pl.BlockSpec((rows, GLA_DV), lambda h, g: (rb(g), 2 * GLA_QK // GLA_DV + h)),
        pl.BlockSpec((rows, GLA_DV), lambda h, g: (rb(g), (2 * GLA_QK + GLA_VD) // GLA_DV + h)),
        pl.BlockSpec((rows, GLA_GLOW_PAD), lambda h, g: (rb(g), GLA_PROJ // GLA_GLOW_PAD)),
        pl.BlockSpec((GLA_GLOW_PAD, GLA_DK), lambda h, g: (0, h)),
        pl.BlockSpec((1, GLA_DK), lambda h, g: (0, h)),
        pl.BlockSpec((1, GLA_DV), lambda h, g: (0, 0)),
        pl.BlockSpec((None, nb, None, GLA_DK, GLA_DV), lambda h, g: (layer, sb(g), h, 0, 0)),
    ]
    state_spec = pl.BlockSpec((nb, None, GLA_DK, GLA_DV), lambda h, g: (sb(g), h, 0, 0))
    state_shape = (batch, GLA_HEADS, GLA_DK, GLA_DV)
    n_earlier = len(earlier)
    if n_earlier:
        in_specs += [state_spec] * n_earlier
        out_state_spec = pl.BlockSpec((n_earlier + 1, nb, None, GLA_DK, GLA_DV), lambda h, g: (0, sb(g), h, 0, 0))
        state_shape = (n_earlier + 1,) + state_shape
    else:
        out_state_spec = state_spec
    return pl.pallas_call(
        functools.partial(_gla_sample_kernel, seq=seq, nb=nb, n_real=n_real, n_earlier=n_earlier),
        grid=(GLA_HEADS, SAMPLE_ROWS // rows),
        in_specs=in_specs,
        out_specs=[pl.BlockSpec((rows, GLA_DV), lambda h, g: (g, h)), out_state_spec],
        out_shape=[jax.ShapeDtypeStruct((SAMPLE_ROWS, GLA_VD), F32),
                   jax.ShapeDtypeStruct(state_shape, F32)],
        compiler_params=_cparams("arbitrary", "arbitrary"),
        name="gla_sample",
    )(proj, proj, proj, proj, proj, wg2p, bg, norm_w, state, *earlier)


def _residual_ln_store(y, res_ref, g_ref, b_ref, of_ref, ob_ref):
    o = _layer_norm(ALPHA * res_ref[...] + y, g_ref[...], b_ref[...])
    of_ref[...] = o
    ob_ref[...] = o.astype(BF16)


def _zero_outputs(of_ref, ob_ref):
    of_ref[...] = jnp.zeros_like(of_ref)
    ob_ref[...] = jnp.zeros_like(ob_ref)


def _deferred_projection(i, n_prompt_tiles, first, a_prompt, a_sample, wb_ref, y_ref,
                         epilogue, sample_epilogue, zero_fill):
    def project(a):
        y_ref[...] = _dot(a(), wb_ref[...])

    @pl.when(i == 0)
    def _():
        first()
        project(a_prompt)

    @pl.when((i > 0) & (i < n_prompt_tiles))
    def _():
        epilogue()
        project(a_prompt)

    @pl.when(i == n_prompt_tiles)
    def _():
        epilogue()
        project(a_sample)

    pl.when(i == n_prompt_tiles + 1)(sample_epilogue)
    if zero_fill is not None:
        pl.when(i > n_prompt_tiles + 1)(zero_fill)


def _mixer_out_kernel(*refs, has_bias, n_prompt_tiles):
    if has_bias:
        ap_ref, as_ref, w_ref, bias_ref, resp_ref, ress_ref, g_ref, b_ref, of_ref, ob_ref, wb_ref, y_ref = refs
    else:
        ap_ref, as_ref, w_ref, resp_ref, ress_ref, g_ref, b_ref, of_ref, ob_ref, wb_ref, y_ref = refs

    def first():
        wb_ref[...] = w_ref[...].astype(BF16)

    def epilogue(res_ref):
        y = y_ref[...] + bias_ref[...] if has_bias else y_ref[...]
        _residual_ln_store(y, res_ref, g_ref, b_ref, of_ref, ob_ref)

    _deferred_projection(pl.program_id(0), n_prompt_tiles, first, lambda: ap_ref[...],
                         lambda: as_ref[...].astype(BF16), wb_ref, y_ref,
                         functools.partial(epilogue, resp_ref), functools.partial(epilogue, ress_ref),
                         lambda: _zero_outputs(of_ref, ob_ref))


def _mixer_out(a_prompt, a_sample, w, layer, bias, res_prompt, res_sample, res_sample_tile, m, g, b, name):
    k = a_prompt.shape[1]
    tm = TM_OUT
    n_prompt_tiles = a_prompt.shape[0] // tm
    assert a_sample.shape[0] == tm
    has_bias = bias is not None
    vec = pl.BlockSpec((1, D_MODEL), lambda i: (0, 0))
    lagged = pl.BlockSpec((tm, D_MODEL), lambda i: (_lag(i), 0))
    in_specs = [pl.BlockSpec((tm, k), lambda i: (jnp.minimum(i, n_prompt_tiles - 1), 0)),
                pl.BlockSpec((tm, k), lambda i: (0, 0)),
                _single_buffered((None, k, D_MODEL), lambda i: (layer, 0, 0))]
    args = [a_prompt, a_sample, w]
    if has_bias:
        in_specs.append(vec)
        args.append(bias)
    in_specs += [pl.BlockSpec((tm, D_MODEL), lambda i: (jnp.minimum(_lag(i), n_prompt_tiles - 1), 0)),
                 pl.BlockSpec((tm, D_MODEL), lambda i: (res_sample_tile, 0)), vec, vec]
    args += [res_prompt, res_sample, g, b]
    return pl.pallas_call(
        functools.partial(_mixer_out_kernel, has_bias=has_bias, n_prompt_tiles=n_prompt_tiles),
        grid=(m // tm + 1,),
        in_specs=in_specs,
        out_specs=[lagged, lagged],
        out_shape=[jax.ShapeDtypeStruct((m, D_MODEL), F32),
                   jax.ShapeDtypeStruct((m, D_MODEL), BF16)],
        scratch_shapes=[pltpu.VMEM((k, D_MODEL), BF16), pltpu.VMEM((tm, D_MODEL), F32)],
        compiler_params=_cparams("arbitrary"),
        name=name,
    )(*args)


def _ffn_out_kernel(a_ref, w_ref, res_ref, g_ref, b_ref, of_ref, ob_ref, y_ref, *, n_prompt_tiles):
    def epilogue():
        _residual_ln_store(y_ref[...], res_ref, g_ref, b_ref, of_ref, ob_ref)

    _deferred_projection(pl.program_id(0), n_prompt_tiles, lambda: None, lambda: a_ref[...], lambda: a_ref[...],
                         w_ref, y_ref, epilogue, epilogue, lambda: _zero_outputs(of_ref, ob_ref))


def _ffn_out_final_kernel(a_ref, w_ref, res_ref, g_ref, b_ref, op_ref, os_ref, y_ref, *, n_prompt_tiles):
    def ln():
        return _layer_norm(ALPHA * res_ref[...] + y_ref[...], g_ref[...], b_ref[...])

    def epilogue():
        op_ref[...] = ln()

    def sample_epilogue():
        os_ref[...] = ln()

    _deferred_projection(pl.program_id(0), n_prompt_tiles, lambda: None, lambda: a_ref[...], lambda: a_ref[...],
                         w_ref, y_ref, epilogue, sample_epilogue, None)


def _ffn_out(a, w, layer, res, g, b, n_prompt_tiles, final):
    m, k = a.shape
    tm = TM_OUT
    vec = pl.BlockSpec((1, D_MODEL), lambda i: (0, 0))
    lagged = pl.BlockSpec((tm, D_MODEL), lambda i: (_lag(i), 0))
    in_specs = [pl.BlockSpec((tm, k), lambda i: (jnp.minimum(i, n_prompt_tiles), 0)),
                _single_buffered((None, k, D_MODEL), lambda i: (layer, 0, 0)),
                lagged, vec, vec]
    scratch = [pltpu.VMEM((tm, D_MODEL), F32)]
    if final:
        return pl.pallas_call(
            functools.partial(_ffn_out_final_kernel, n_prompt_tiles=n_prompt_tiles),
            grid=(n_prompt_tiles + 2,),
            in_specs=in_specs,
            out_specs=[pl.BlockSpec((tm, D_MODEL), lambda i: (jnp.minimum(_lag(i), n_prompt_tiles - 1), 0)),
                       pl.BlockSpec((tm, D_MODEL), lambda i: (0, 0))],
            out_shape=[jax.ShapeDtypeStruct((n_prompt_tiles * tm, D_MODEL), F32),
                       jax.ShapeDtypeStruct((tm, D_MODEL), F32)],
            scratch_shapes=scratch,
            compiler_params=_cparams("arbitrary"),
            name="ffn_out_final",
        )(a, w, res, g, b)
    return pl.pallas_call(
        functools.partial(_ffn_out_kernel, n_prompt_tiles=n_prompt_tiles),
        grid=(m // tm + 1,),
        in_specs=in_specs,
        out_specs=[lagged, lagged],
        out_shape=[jax.ShapeDtypeStruct((m, D_MODEL), F32),
                   jax.ShapeDtypeStruct((m, D_MODEL), BF16)],
        scratch_shapes=scratch,
        compiler_params=_cparams("arbitrary"),
        name="ffn_out",
    )(a, w, res, g, b)


def _conv_glu(g, val, g1, g2, cw, cb):
    conv = cb + cw[0:1, :] * g2
    conv = conv + cw[1:2, :] * g1
    conv = conv + cw[2:3, :] * g
    return (_gelu(conv) * val).astype(BF16)


def _ffn_in_kernel(x_ref, wg_ref, wv_ref, cw_ref, cb_ref, p1_ref, p2_ref, u_ref, glp_ref, gls_ref,
                   wgb_ref, wvb_ref, carry_ref, gs_ref, vs_ref, *, seq_tiles, n_tiles, sample_seq):
    i = pl.program_id(1)
    tm, tn = u_ref.shape
    col_slices = [slice(c0, c0 + MXU_COLS) for c0 in range(0, tn, MXU_COLS)]

    def dots(rows):
        x = x_ref[0:rows, :]
        for cs in col_slices:
            gs_ref[0:rows, cs] = _dot(x, wgb_ref[:, cs])
            vs_ref[0:rows, cs] = _dot(x, wvb_ref[:, cs])

    def prompt_epilogue():
        row = lax.broadcasted_iota(jnp.int32, (tm, MXU_COLS), 0)
        first = (i - 1) % seq_tiles == 0
        for cs in col_slices:
            g, val = gs_ref[:, cs], vs_ref[:, cs]
            above = jnp.where(first, 0.0, carry_ref[:, cs])
            g1 = jnp.where(row >= 1, pltpu.roll(g, 1, axis=0), above[7:8, :])
            g2 = jnp.where(row >= 2, pltpu.roll(g, 2, axis=0),
                           jnp.where(row == 0, above[6:7, :], above[7:8, :]))
            u_ref[:, cs] = _conv_glu(g, val, g1, g2, cw_ref[:, cs], cb_ref[:, cs])
            carry_ref[:, cs] = g[tm - 8:tm, :]
            glp_ref[:, cs] = g[tm - 8:tm, :]

    @pl.when(i == 0)
    def _():
        wgb_ref[...] = wg_ref[...].astype(BF16)
        wvb_ref[...] = wv_ref[...].astype(BF16)
        dots(tm)

    @pl.when((i > 0) & (i < n_tiles - 1))
    def _():
        prompt_epilogue()
        dots(tm)

    @pl.when(i == n_tiles - 1)
    def _():
        prompt_epilogue()
        dots(SAMPLE_ROWS)

    @pl.when(i == n_tiles)
    def _():
        rows = SAMPLE_ROWS
        pos = lax.broadcasted_iota(jnp.int32, (rows, MXU_COLS), 0) % sample_seq
        for cs in col_slices:
            g, val = gs_ref[0:rows, cs], vs_ref[0:rows, cs]
            g1 = jnp.where(pos >= 1, pltpu.roll(g, 1, axis=0), p1_ref[:, cs])
            g2 = jnp.where(pos >= 2, pltpu.roll(g, 2, axis=0), p2_ref[:, cs])
            u_ref[0:rows, cs] = _conv_glu(g, val, g1, g2, cw_ref[:, cs], cb_ref[:, cs])
            gls_ref[:, cs] = g
        u_ref[rows:tm, :] = jnp.zeros((tm - rows, tn), BF16)


def _ffn_in(xb, w_in, layer, conv_w, conv_b, prev1, prev2, n_prompt_seqs, seq, sample_seq, tn):
    m = xb.shape[0]
    nj = D_FF // tn
    seq_tiles = seq // TM
    n_tiles = m // TM
    n_prompt_tiles = n_prompt_seqs * seq_tiles
    assert n_tiles == n_prompt_tiles + 1
    done = lambda i: jnp.maximum(i - 1, 0)
    sample = pl.BlockSpec((SAMPLE_ROWS, tn), lambda j, i: (0, j))
    return pl.pallas_call(
        functools.partial(_ffn_in_kernel, seq_tiles=seq_tiles, n_tiles=n_tiles, sample_seq=sample_seq),
        grid=(nj, n_tiles + 1),
        in_specs=[pl.BlockSpec((TM, D_MODEL), lambda j, i: (jnp.minimum(i, n_tiles - 1), 0)),
                  pl.BlockSpec((None, D_MODEL, tn), lambda j, i: (layer, 0, j)),
                  pl.BlockSpec((None, D_MODEL, tn), lambda j, i: (layer, 0, nj + j)),
                  pl.BlockSpec((None, CONV_W, tn), lambda j, i: (layer, 0, j)),
                  pl.BlockSpec((None, 1, tn), lambda j, i: (layer, 0, j)),
                  sample, sample],
        out_specs=[pl.BlockSpec((TM, tn), lambda j, i: (done(i), j)),
                   pl.BlockSpec((8, tn), lambda j, i: (jnp.minimum(done(i), n_prompt_tiles - 1) // seq_tiles, j)),
                   sample],
        out_shape=[jax.ShapeDtypeStruct((m, D_FF), BF16),
                   jax.ShapeDtypeStruct((n_prompt_seqs * 8, D_FF), F32),
                   jax.ShapeDtypeStruct((SAMPLE_ROWS, D_FF), F32)],
        scratch_shapes=[pltpu.VMEM((D_MODEL, tn), BF16), pltpu.VMEM((D_MODEL, tn), BF16),
                        pltpu.VMEM((8, tn), F32), pltpu.VMEM((TM, tn), F32), pltpu.VMEM((TM, tn), F32)],
        compiler_params=_cparams("arbitrary", "arbitrary"),
        name="ffn_in",
    )(xb, w_in, w_in, conv_w, conv_b, prev1, prev2)


def _swa_qkv_kernel(x_ref, w_ref, b_ref, ca_ref, cm_ref, cp_ref, o_ref, wb_ref, y_ref,
                    *, n_rot_tiles, n_prompt_tiles, n_tiles):
    j = pl.program_id(0)
    tm, tn = o_ref.shape

    def first():
        wb_ref[...] = w_ref[...].astype(BF16)

    def dots(rows):
        x = x_ref[0:rows, :]
        for c0 in range(0, tn, MXU_COLS):
            sl = slice(c0, c0 + MXU_COLS)
            y_ref[0:rows, sl] = _dot(x, wb_ref[:, sl])

    def epilogue(rows):
        reps = MXU_COLS // ca_ref.shape[1]
        ca = jnp.concatenate([ca_ref[0:rows, :]] * reps, axis=1)
        cm = jnp.concatenate([cm_ref[0:rows, :]] * reps, axis=1)
        cp = jnp.concatenate([cp_ref[0:rows, :]] * reps, axis=1)
        lane = lax.broadcasted_iota(jnp.int32, (rows, MXU_COLS), 1) % SWA_HEAD_DIM
        rotated = lane < jnp.where(j < n_rot_tiles, ROT_DIM, 0)
        half = ROT_DIM // 2
        for c0 in range(0, tn, MXU_COLS):
            sl = slice(c0, c0 + MXU_COLS)
            y = y_ref[0:rows, sl] + b_ref[:, sl]
            rot = y * ca + pltpu.roll(y, MXU_COLS - half, axis=1) * cm + pltpu.roll(y, half, axis=1) * cp
            o_ref[0:rows, sl] = jnp.where(rotated, rot, y)
        if rows < tm:
            o_ref[rows:tm, :] = jnp.zeros((tm - rows, tn), F32)

    _deferred_row_tiles(pl.program_id(1), n_prompt_tiles, n_tiles, tm, first, dots, epilogue)


def _rope_tables(pos):
    half = ROT_DIM // 2
    inv = ROPE_THETA ** (-jnp.arange(half, dtype=F32) / half)
    ang = pos.astype(F32)[:, None] * inv
    cos, sin = jnp.cos(ang), jnp.sin(ang)
    n = pos.shape[0]
    rest = SWA_HEAD_DIM - ROT_DIM
    ca = jnp.concatenate([cos, cos, jnp.ones((n, rest), F32)], -1)
    cm = jnp.concatenate([-sin, jnp.zeros((n, half + rest), F32)], -1)
    cp = jnp.concatenate([jnp.zeros((n, half), F32), sin, jnp.zeros((n, rest), F32)], -1)
    return tuple(jnp.tile(t, (1, 128 // SWA_HEAD_DIM)) for t in (ca, cm, cp))


def _swa_qkv(xb, w, layer, b, tables, n_prompt_tiles, seq, tn):
    m = xb.shape[0]
    n = w.shape[2]
    pos_tiles = seq // TM
    n_tiles = m // TM
    tab = pl.BlockSpec((TM, 128), lambda j, i: (jnp.where(_lag(i) < n_prompt_tiles, _lag(i) % pos_tiles, pos_tiles), 0))
    return pl.pallas_call(
        functools.partial(_swa_qkv_kernel, n_rot_tiles=(SWA_QD + SWA_KD) // tn, n_prompt_tiles=n_prompt_tiles,
                          n_tiles=n_tiles),
        grid=(n // tn, n_tiles + 1),
        in_specs=[pl.BlockSpec((TM, D_MODEL), lambda j, i: (jnp.minimum(i, n_tiles - 1), 0)),
                  pl.BlockSpec((None, D_MODEL, tn), lambda j, i: (layer, 0, j)),
                  pl.BlockSpec((1, tn), lambda j, i: (0, j)),
                  tab, tab, tab],
        out_specs=pl.BlockSpec((TM, tn), lambda j, i: (_lag(i), j)),
        out_shape=jax.ShapeDtypeStruct((m, n), F32),
        scratch_shapes=[pltpu.VMEM((D_MODEL, tn), BF16), pltpu.VMEM((TM, tn), F32)],
        compiler_params=_cparams("arbitrary", "arbitrary"),
        name="swa_qkv",
    )(xb, w, b, *tables)


SWA_SCALE = SWA_HEAD_DIM ** -0.5


def _sink_softmax_pv(s, mask, sink, v):
    s = jnp.where(mask, s, -jnp.inf)
    mx = jnp.maximum(jnp.max(s, -1, keepdims=True), sink)
    p = jnp.exp(s - mx)
    denom = jnp.sum(p, -1, keepdims=True) + jnp.exp(sink - mx)
    return _dot((p / denom).astype(BF16), v)


def _swa_prompt_kernel(sink_ref, q_ref, kp_ref, kc_ref, vp_ref, vc_ref, o_ref):
    i = pl.program_id(1)
    q = q_ref[...] * SWA_SCALE
    kb = jnp.concatenate([kp_ref[...], kc_ref[...]], axis=0).astype(BF16)
    vb = jnp.concatenate([vp_ref[...], vc_ref[...]], axis=0).astype(BF16)
    rows = SWA_GROUP * WINDOW
    r = lax.broadcasted_iota(jnp.int32, (rows, 2 * WINDOW), 0) % WINDOW
    c = lax.broadcasted_iota(jnp.int32, (rows, 2 * WINDOW), 1)
    mask = (c > r) & (c <= r + WINDOW) & (c >= jnp.where(i > 0, 0, WINDOW))
    grp = lax.broadcasted_iota(jnp.int32, (rows, 1), 0) // WINDOW
    pieces = []
    for h in range(SWA_KV_HEADS):
        kh = kb[:, h * SWA_HEAD_DIM:(h + 1) * SWA_HEAD_DIM]
        vh = vb[:, h * SWA_HEAD_DIM:(h + 1) * SWA_HEAD_DIM]
        qs = jnp.concatenate(
            [q[:, (h * SWA_GROUP + g) * SWA_HEAD_DIM:(h * SWA_GROUP + g + 1) * SWA_HEAD_DIM]
             for g in range(SWA_GROUP)], axis=0).astype(BF16)
        sink = jnp.zeros((rows, 1), F32)
        for g in range(SWA_GROUP):
            sink = jnp.where(grp == g, sink_ref[h * SWA_GROUP + g], sink)
        o = _sink_softmax_pv(_dot_nt(qs, kh), mask, sink, vh)
        pieces += [o[g * WINDOW:(g + 1) * WINDOW, :] for g in range(SWA_GROUP)]
    o_ref[...] = jnp.concatenate(pieces, axis=-1).astype(o_ref.dtype)


def _swa_prompt_attn(qkv, sinks, batch, seq):
    nb = seq // WINDOW
    kcol = SWA_QD // SWA_KD
    cur = lambda b, i: b * nb + i
    prev = lambda b, i: b * nb + jnp.maximum(i - 1, 0)
    return pl.pallas_call(
        _swa_prompt_kernel,
        grid=(batch, nb),
        in_specs=[pl.BlockSpec(memory_space=pltpu.SMEM),
                  pl.BlockSpec((WINDOW, SWA_QD), lambda b, i: (cur(b, i), 0)),
                  pl.BlockSpec((WINDOW, SWA_KD), lambda b, i: (prev(b, i), kcol)),
                  pl.BlockSpec((WINDOW, SWA_KD), lambda b, i: (cur(b, i), kcol)),
                  pl.BlockSpec((WINDOW, SWA_KD), lambda b, i: (prev(b, i), kcol + 1)),
                  pl.BlockSpec((WINDOW, SWA_KD), lambda b, i: (cur(b, i), kcol + 1))],
        out_specs=pl.BlockSpec((WINDOW, SWA_QD), lambda b, i: (cur(b, i), 0)),
        out_shape=jax.ShapeDtypeStruct((batch * seq, SWA_QD), BF16),
        compiler_params=_cparams("arbitrary", "arbitrary"),
        name="swa_prompt_attn",
    )(sinks, qkv, qkv, qkv, qkv, qkv)


def _swa_sample_kernel(sink_ref, q_ref, kn_ref, vn_ref, kc_ref, vc_ref, o_ref, ko_ref, vo_ref,
                       *, seq, nb, n_real):
    step = pl.program_id(0)
    nbuf = kc_ref.shape[1]

    @pl.when(step < n_real)
    def _():
        rows = SWA_GROUP * seq
        l = lax.broadcasted_iota(jnp.int32, (rows, nbuf + seq), 0) % seq
        c = lax.broadcasted_iota(jnp.int32, (rows, nbuf + seq), 1)
        diff = l + nbuf - c
        mask = (diff >= 0) & (diff < WINDOW)
        grp = lax.broadcasted_iota(jnp.int32, (rows, 1), 0) // seq
        for b in range(nb):
            lo, hi = b * seq, (b + 1) * seq
            k_all = jnp.concatenate([kc_ref[b], kn_ref[lo:hi, :]], axis=0)
            v_all = jnp.concatenate([vc_ref[b], vn_ref[lo:hi, :]], axis=0)
            ko_ref[b] = k_all[seq:, :]
            vo_ref[b] = v_all[seq:, :]
            kb = k_all.astype(BF16)
            vb = v_all.astype(BF16)
            q = q_ref[lo:hi, :] * SWA_SCALE
            pieces = []
            for h in range(SWA_KV_HEADS):
                kh = kb[:, h * SWA_HEAD_DIM:(h + 1) * SWA_HEAD_DIM]
                vh = vb[:, h * SWA_HEAD_DIM:(h + 1) * SWA_HEAD_DIM]
                qs = jnp.concatenate(
                    [q[:, (h * SWA_GROUP + g) * SWA_HEAD_DIM:(h * SWA_GROUP + g + 1) * SWA_HEAD_DIM]
                     for g in range(SWA_GROUP)], axis=0).astype(BF16)
                sink = jnp.zeros((rows, 1), F32)
                for g in range(SWA_GROUP):
                    sink = jnp.where(grp == g, sink_ref[h * SWA_GROUP + g], sink)
                o = _sink_softmax_pv(_dot_nt(qs, kh), mask, sink, vh)
                pieces += [o[g * seq:(g + 1) * seq, :] for g in range(SWA_GROUP)]
            o_ref[lo:hi, :] = jnp.concatenate(pieces, axis=-1)

    @pl.when(step >= n_real)
    def _():
        o_ref[...] = jnp.zeros_like(o_ref)


def _swa_sample_attn(qkv, row0, sinks, k_cache, v_cache, batch, seq, nb):
    nbuf = k_cache.shape[1]
    kcol = SWA_QD // SWA_KD
    rows = nb * seq
    n_real = batch // nb
    blk0 = row0 // rows
    rb = lambda s: blk0 + jnp.minimum(s, n_real - 1)
    cache = pl.BlockSpec((nb, nbuf, SWA_KD), lambda s: (jnp.minimum(s, n_real - 1), 0, 0))
    return pl.pallas_call(
        functools.partial(_swa_sample_kernel, seq=seq, nb=nb, n_real=n_real),
        grid=(SAMPLE_ROWS // rows,),
        in_specs=[pl.BlockSpec(memory_space=pltpu.SMEM),
                  pl.BlockSpec((rows, SWA_QD), lambda s: (rb(s), 0)),
                  pl.BlockSpec((rows, SWA_KD), lambda s: (rb(s), kcol)),
                  pl.BlockSpec((rows, SWA_KD), lambda s: (rb(s), kcol + 1)),
                  cache, cache],
        out_specs=[pl.BlockSpec((rows, SWA_QD), lambda s: (s, 0)), cache, cache],
        out_shape=[jax.ShapeDtypeStruct((SAMPLE_ROWS, SWA_QD), F32),
                   jax.ShapeDtypeStruct((batch, nbuf, SWA_KD), F32),
                   jax.ShapeDtypeStruct((batch, nbuf, SWA_KD), F32)],
        compiler_params=_cparams("arbitrary"),
        name="swa_sample_attn",
    )(sinks, qkv, qkv, qkv, k_cache, v_cache)


def _sg_in_kernel(x_ref, w_ref, b_ref, g_ref, beta_ref, o_ref, wb_ref, *, n_prompt_tiles):
    j = pl.program_id(0)
    tm = o_ref.shape[0]

    @pl.when(pl.program_id(1) == 0)
    def _():
        wb_ref[...] = w_ref[...].astype(BF16)

    def body(rows, normalise):
        z = _gelu(_dot(x_ref[0:rows, :], wb_ref[...]) + b_ref[...])
        o_ref[0:rows, :] = _layer_norm(z, g_ref[...], beta_ref[...]) if normalise else z
        if rows < tm:
            o_ref[rows:tm, :] = jnp.zeros((tm - rows, o_ref.shape[1]), F32)

    for half, normalise in ((0, False), (1, True)):
        pl.when(j == half)(functools.partial(
            _per_row_tile, pl.program_id(1), n_prompt_tiles, tm, functools.partial(body, normalise=normalise)))


def _sg_in(xb, w, layer, b, ln_g, ln_b, n_prompt_tiles):
    m = xb.shape[0]
    tm = TM_SG_IN
    vec = pl.BlockSpec((1, SG_WIDTH), lambda j, i: (0, 0))
    return pl.pallas_call(
        functools.partial(_sg_in_kernel, n_prompt_tiles=n_prompt_tiles),
        grid=(2, m // tm),
        in_specs=[pl.BlockSpec((tm, D_MODEL), lambda j, i: (i, 0)),
                  _single_buffered((None, D_MODEL, SG_WIDTH), lambda j, i: (layer, 0, j)),
                  pl.BlockSpec((1, SG_WIDTH), lambda j, i: (0, j)),
                  vec, vec],
        out_specs=pl.BlockSpec((None, tm, SG_WIDTH), lambda j, i: (j, i, 0)),
        out_shape=jax.ShapeDtypeStruct((2, m, SG_WIDTH), F32),
        scratch_shapes=[pltpu.VMEM((D_MODEL, SG_WIDTH), BF16)],
        compiler_params=_cparams("arbitrary", "arbitrary"),
        name="sg_in",
    )(xb, w, b, ln_g, ln_b)


def _sg_mix(u_ref, v_ref, ws_ref, bs_ref, chunk, period):
    tm = u_ref.shape[0]
    r = lax.broadcasted_iota(jnp.int32, (chunk, chunk), 0)
    c = lax.broadcasted_iota(jnp.int32, (chunk, chunk), 1)
    mask = (c <= r) & (r // period == c // period)
    ws = [jnp.where(mask, ws_ref[g], 0.0).astype(BF16) for g in range(SG_GROUPS)]
    rows = []
    for n in range(tm // chunk):
        lo, hi = n * chunk, (n + 1) * chunk
        cols = []
        for g in range(SG_GROUPS):
            vg = v_ref[lo:hi, g * SG_GW:(g + 1) * SG_GW].astype(BF16)
            mixed = _dot(ws[g], vg) + bs_ref[:, g:g + 1]
            cols.append((u_ref[lo:hi, g * SG_GW:(g + 1) * SG_GW] * mixed).astype(BF16))
        rows.append(jnp.concatenate(cols, axis=-1))
    return jnp.concatenate(rows, axis=0)


def _sg_out_kernel(u_ref, v_ref, wsp_ref, bsp_ref, wss_ref, bss_ref, w_ref, bias_ref, res_ref, g_ref, b_ref,
                   of_ref, ob_ref, wb_ref, y_ref, *, n_prompt_tiles, sample_seq):
    def first():
        wb_ref[...] = w_ref[...].astype(BF16)

    def epilogue():
        _residual_ln_store(y_ref[...] + bias_ref[...], res_ref, g_ref, b_ref, of_ref, ob_ref)

    _deferred_projection(pl.program_id(0), n_prompt_tiles, first,
                         lambda: _sg_mix(u_ref, v_ref, wsp_ref, bsp_ref, SG_CHUNK, SG_CHUNK),
                         lambda: _sg_mix(u_ref, v_ref, wss_ref, bss_ref, wss_ref.shape[1], sample_seq),
                         wb_ref, y_ref, epilogue, epilogue, lambda: _zero_outputs(of_ref, ob_ref))


def _sg_out(uv, ws_p, bs_p, ws_s, bs_s, w_out, layer, b_out, res, g, b, n_prompt_tiles, sample_seq):
    m = res.shape[0]
    tm = TM_OUT
    vec = pl.BlockSpec((1, D_MODEL), lambda i: (0, 0))
    full = lambda a: pl.BlockSpec(a.shape, lambda i: (0,) * a.ndim)
    lagged = pl.BlockSpec((tm, D_MODEL), lambda i: (_lag(i), 0))
    return pl.pallas_call(
        functools.partial(_sg_out_kernel, n_prompt_tiles=n_prompt_tiles, sample_seq=sample_seq),
        grid=(m // tm + 1,),
        in_specs=[pl.BlockSpec((None, tm, SG_WIDTH), lambda i: (0, jnp.minimum(i, n_prompt_tiles), 0)),
                  pl.BlockSpec((None, tm, SG_WIDTH), lambda i: (1, jnp.minimum(i, n_prompt_tiles), 0)),
                  full(ws_p), full(bs_p), full(ws_s), full(bs_s),
                  _single_buffered((None, SG_WIDTH, D_MODEL), lambda i: (layer, 0, 0)),
                  vec, lagged, vec, vec],
        out_specs=[lagged, lagged],
        out_shape=[jax.ShapeDtypeStruct((m, D_MODEL), F32),
                   jax.ShapeDtypeStruct((m, D_MODEL), BF16)],
        scratch_shapes=[pltpu.VMEM((SG_WIDTH, D_MODEL), BF16), pltpu.VMEM((tm, D_MODEL), F32)],
        compiler_params=_cparams("arbitrary"),
        name="sg_out",
    )(uv, uv, ws_p, bs_p, ws_s, bs_s, w_out, b_out, res, g, b)


def kernel(x_prompt, x_sample, state_gla, cache_swa_k, cache_swa_v, state_ffn_conv, ln_mix_g, ln_mix_b, ln_ffn_g, ln_ffn_b, gla_w_in, gla_w_g2, gla_b_g, gla_norm_w, gla_w_out, swa_w_qkv, swa_b_qkv, swa_sinks, swa_w_out, swa_b_out, sg_w_in, sg_b_in, sg_ln_g, sg_ln_b, sg_w_s, sg_b_s, sg_w_out, sg_b_out, ffn_w_in, ffn_conv_w, ffn_conv_b, ffn_w_out):
    bp, lp, _ = x_prompt.shape
    bs, ls, _ = x_sample.shape
    mp, ms = bp * lp, bs * ls
    assert lp % TM == 0 and ms <= SAMPLE_ROWS and SAMPLE_ROWS % ls == 0 and ls >= CONV_W - 1
    assert SAMPLE_ROWS == TM_OUT and SAMPLE_PAD % TM_SG_IN == 0

    m_rows = mp + SAMPLE_PAD
    xp2, xs2 = x_prompt.reshape(mp, D_MODEL), x_sample.reshape(ms, D_MODEL)
    xb = jnp.concatenate([xp2.astype(BF16), xs2.astype(BF16), jnp.zeros((SAMPLE_PAD - ms, D_MODEL), BF16)], axis=0)
    residual = (xp2, jnp.pad(xs2, ((0, TM_OUT - ms), (0, 0))), 0)
    gla_w_proj = jnp.pad(gla_w_in.astype(BF16), ((0, 0), (0, 0), (0, GLA_PROJ_PAD - gla_w_in.shape[2])))
    gla_wg2p = jnp.pad(gla_w_g2.astype(BF16), ((0, 0), (0, GLA_GLOW_PAD - GLA_RANK), (0, 0)))
    ffn_w_out_b = ffn_w_out.astype(BF16)
    conv_b3 = ffn_conv_b.reshape(DEPTH, 1, D_FF)

    gla_p, gla_s, swk_p, swv_p, swk_s, swv_s, sgv_s, conv_p, conv_s = ([] for _ in range(9))
    for i in range(DEPTH):
        j = i // N_MIXERS
        kind = i % N_MIXERS
        ln_g, ln_b = _row(ln_mix_g[i]), _row(ln_mix_b[i])
        if kind == 0:
            proj = _gla_proj(xb, gla_w_proj, j, mp // TM)
            bg, nw = _row(gla_b_g[j]), _row(gla_norm_w[j])
            og_p, st_p = _gla_prompt(proj, gla_wg2p[j], bg, nw, bp, lp, 256)
            last_gla = j == gla_w_in.shape[0] - 1
            og_s, st_s = _gla_sample(proj, mp, gla_wg2p[j], bg, nw, state_gla, j, bs, ls, 8 if bs % 8 == 0 else 1,
                                     earlier=tuple(gla_s) if last_gla else ())
            gla_p.append(st_p)
            gla_s.append(st_s)
            xf, xb = _mixer_out(og_p, og_s, gla_w_out, j, None, *residual, m_rows, ln_g, ln_b, "gla_out")
        elif kind == 1:
            pos = jnp.concatenate([jnp.arange(lp), PAST_LEN + jnp.arange(TM) % ls])
            qkv = _swa_qkv(xb, swa_w_qkv, j, _row(swa_b_qkv[j]), _rope_tables(pos), mp // TM, lp, 512)
            sinks = swa_sinks[j].reshape(-1)
            oa_p = _swa_prompt_attn(qkv, sinks, bp, lp)
            kv = jnp.stack([qkv[(b + 1) * lp - WINDOW:(b + 1) * lp, SWA_QD:] for b in range(bp)])
            swk_p.append(kv[..., :SWA_KD].reshape(bp, WINDOW, SWA_KV_HEADS, SWA_HEAD_DIM))
            swv_p.append(kv[..., SWA_KD:].reshape(bp, WINDOW, SWA_KV_HEADS, SWA_HEAD_DIM))
            nbuf = cache_swa_k.shape[2]
            oa_s, ko, vo = _swa_sample_attn(qkv, mp, sinks, cache_swa_k[j].reshape(bs, nbuf, SWA_KD),
                                            cache_swa_v[j].reshape(bs, nbuf, SWA_KD), bs, ls,
                                            2 if bs % 2 == 0 else 1)
            swk_s.append(ko.reshape(bs, nbuf, SWA_KV_HEADS, SWA_HEAD_DIM))
            swv_s.append(vo.reshape(bs, nbuf, SWA_KV_HEADS, SWA_HEAD_DIM))
            xf, xb = _mixer_out(oa_p, oa_s, swa_w_out, j, _row(swa_b_out[j]), *residual, m_rows, ln_g, ln_b,
                                "swa_out")
        else:
            uv = _sg_in(xb, sg_w_in, j, _row(sg_b_in[j]), _row(sg_ln_g[j]), _row(sg_ln_b[j]), mp // TM_SG_IN)
            sgv_s.append(uv[1, mp:mp + ms].reshape(bs, ls, SG_WIDTH))
            reps = TM_OUT // ls
            ws_s = jnp.tile(sg_w_s[j][:, :ls, :ls], (1, reps, reps))
            bs_s = jnp.tile(sg_b_s[j][:, :ls].T, (reps, 1))
            xf, xb = _sg_out(uv, sg_w_s[j], sg_b_s[j].T, ws_s, bs_s, sg_w_out, j, _row(sg_b_out[j]),
                             xf, ln_g, ln_b, mp // TM_OUT, ls)
        st = state_ffn_conv[i]
        prev1 = jnp.pad(st[:, 1:2], ((0, 0), (0, ls - 1), (0, 0))).reshape(ms, D_FF)
        prev2 = jnp.pad(st, ((0, 0), (0, ls - 2), (0, 0))).reshape(ms, D_FF)
        prev1 = jnp.pad(prev1, ((0, SAMPLE_ROWS - ms), (0, 0)))
        prev2 = jnp.pad(prev2, ((0, SAMPLE_ROWS - ms), (0, 0)))
        u, gl_p, gl_s = _ffn_in(xb, ffn_w_in, i, ffn_conv_w, conv_b3, prev1, prev2, bp, lp, ls, 512)
        conv_p.append(gl_p.reshape(bp, 8, D_FF)[:, 8 - (CONV_W - 1):])
        conv_s.append(gl_s[:ms].reshape(bs, ls, D_FF)[:, ls - (CONV_W - 1):])
        final = i == DEPTH - 1
        outs = _ffn_out(u, ffn_w_out_b, i, xf, _row(ln_ffn_g[i]), _row(ln_ffn_b[i]), mp // TM_OUT, final)
        if final:
            y_prompt_rows, y_sample_tile = outs
        else:
            xf, xb = outs
            residual = (xf, xf, mp // TM_OUT)

    yp = y_prompt_rows.reshape(bp, lp, D_MODEL)
    ys = y_sample_tile[:ms].reshape(bs, ls, D_MODEL)
    gla_s_all = gla_s[-1] if len(gla_s) > 1 else jnp.stack(gla_s)
    return (yp, ys, jnp.stack(gla_p), gla_s_all, jnp.stack(swk_p), jnp.stack(swv_p),
            jnp.stack(swk_s), jnp.stack(swv_s), jnp.stack(sgv_s), jnp.stack(conv_p), jnp.stack(conv_s))
```

```python
import functools

import jax
import jax.numpy as jnp
from jax import lax
from jax.experimental import pallas as pl
from jax.experimental.pallas import tpu as pltpu

F32 = jnp.float32
BF16 = jnp.bfloat16

D_MODEL = 2048
DEPTH = 4
PAST_LEN = 16384
N_MIXERS = 3
ALPHA = (2 * DEPTH) ** 0.25
LN_EPS = 1e-5

GLA_HEADS = 4
GLA_DK = 256
GLA_DV = 512
GLA_RANK = 16
GLA_TAU = 16.0
GLA_CHUNK = 64
GLA_QK = GLA_HEADS * GLA_DK
GLA_VD = GLA_HEADS * GLA_DV
GLA_PROJ = 2 * GLA_QK + 2 * GLA_VD
GLA_GLOW_PAD = 128
GLA_PROJ_TN = 1280
GLA_PROJ_PAD = 5 * GLA_PROJ_TN
GLA_HEADS_PER_STEP = 4

SWA_HEAD_DIM = 64
SWA_Q_HEADS = 32
SWA_KV_HEADS = 8
SWA_GROUP = 4
WINDOW = 128
ROT_DIM = 16
ROPE_THETA = 500000.0
SWA_QD = SWA_Q_HEADS * SWA_HEAD_DIM
SWA_KD = SWA_KV_HEADS * SWA_HEAD_DIM

SG_WIDTH = 2048
SG_GROUPS = 4
SG_GW = SG_WIDTH // SG_GROUPS
SG_CHUNK = 128

D_FF = 5632
CONV_W = 3

TM = 1024
TM_FFN_IN = 1024
TM_SG_IN = 512
TM_OUT = 256
SAMPLE_PAD = TM
SAMPLE_ROWS = 256
MXU_COLS = 256
VMEM_LIMIT_BYTES = 56 * 1024 * 1024


def _cparams(*sem):
    return pltpu.CompilerParams(dimension_semantics=sem, vmem_limit_bytes=VMEM_LIMIT_BYTES)


def _dot(a, b):
    return jnp.dot(a, b, preferred_element_type=F32)


def _dot_nt(a, b):
    return lax.dot_general(a, b, (((1,), (1,)), ((), ())), preferred_element_type=F32)


def _dot_tn(a, b):
    return lax.dot_general(a, b, (((0,), (0,)), ((), ())), preferred_element_type=F32)


def _layer_norm(x, g, b):
    mu = jnp.mean(x, -1, keepdims=True)
    xc = x - mu
    var = jnp.mean(xc * xc, -1, keepdims=True)
    return xc * lax.rsqrt(var + LN_EPS) * g + b


def _gelu(x):
    return 0.5 * x * (1.0 + lax.erf(x * (0.5 ** 0.5)))


def _log_sigmoid(x):
    return jnp.minimum(x, 0.0) - jnp.log1p(jnp.exp(-jnp.abs(x)))


def _cumsum_rows(x, period):
    rowmod = lax.broadcasted_iota(jnp.int32, x.shape, 0) % period
    s = 1
    while s < period:
        x = x + jnp.where(rowmod >= s, pltpu.roll(x, s, axis=0), 0.0)
        s *= 2
    return x


def _single_buffered(shape, index_map):
    return pl.BlockSpec(shape, index_map, pipeline_mode=pl.Buffered(1))


def _row(v):
    return v.reshape(1, -1)


def _per_row_tile(i, n_prompt_tiles, tm, body):
    pl.when(i < n_prompt_tiles)(lambda: body(tm))
    pl.when(i >= n_prompt_tiles)(lambda: body(min(tm, SAMPLE_ROWS)))


def _lag(i):
    return jnp.maximum(i - 1, 0)


def _deferred_row_tiles(i, n_prompt_tiles, n_tiles, tm, first, dots, epilogue):
    s = min(tm, SAMPLE_ROWS)

    @pl.when(i == 0)
    def _():
        first()
        dots(tm)

    @pl.when((i > 0) & (i < n_prompt_tiles))
    def _():
        epilogue(tm)
        dots(tm)

    @pl.when(i == n_prompt_tiles)
    def _():
        epilogue(tm)
        dots(s)

    if n_tiles > n_prompt_tiles + 1:
        @pl.when((i > n_prompt_tiles) & (i < n_tiles))
        def _():
            epilogue(s)
            dots(s)

    pl.when(i == n_tiles)(lambda: epilogue(s))


def _gla_proj_kernel(x_ref, w_ref, o_ref, wz_ref, *, n_prompt_tiles, n_valid):
    tm, tn = o_ref.shape

    @pl.when(pl.program_id(1) == 0)
    def _():
        w = w_ref[...]
        col = lax.broadcasted_iota(jnp.int32, w.shape, 1) + pl.program_id(0) * tn
        wz_ref[...] = jnp.where(col < n_valid, w, jnp.zeros_like(w))

    def body(rows):
        o_ref[0:rows, :] = _dot(x_ref[0:rows, :], wz_ref[...])
        if rows < tm:
            o_ref[rows:tm, :] = jnp.zeros((tm - rows, tn), F32)

    _per_row_tile(pl.program_id(1), n_prompt_tiles, tm, body)


def _gla_proj(xb, w_in_b, layer, n_prompt_tiles):
    m, k = xb.shape
    tn = GLA_PROJ_TN
    return pl.pallas_call(
        functools.partial(_gla_proj_kernel, n_prompt_tiles=n_prompt_tiles, n_valid=w_in_b.shape[2]),
        grid=(GLA_PROJ_PAD // tn, m // TM),
        in_specs=[pl.BlockSpec((TM, k), lambda j, i: (i, 0)),
                  pl.BlockSpec((None, k, tn), lambda j, i: (layer, 0, j))],
        out_specs=pl.BlockSpec((TM, tn), lambda j, i: (i, j)),
        out_shape=jax.ShapeDtypeStruct((m, GLA_PROJ_PAD), F32),
        scratch_shapes=[pltpu.VMEM((k, tn), BF16)],
        compiler_params=_cparams("arbitrary", "arbitrary"),
        name="gla_proj",
    )(xb, w_in_b)


def _gla_gate_and_decay(q, k, glow_bf16, wg2, bg, period):
    ga = _dot(glow_bf16, wg2) + bg
    cum = _cumsum_rows(_log_sigmoid(ga) * (1.0 / GLA_TAU), period)
    q_dec = (q * (GLA_DK ** -0.5) * jnp.exp(cum)).astype(BF16)
    k_inv = (k * jnp.exp(-cum)).astype(BF16)
    return cum, q_dec, k_inv


def _gla_finish(o, r, nw):
    o = o * lax.rsqrt(jnp.mean(o * o, -1, keepdims=True) + LN_EPS) * nw
    return o * (r * jax.nn.sigmoid(r))


def _causal(n):
    r = lax.broadcasted_iota(jnp.int32, (n, n), 0)
    c = lax.broadcasted_iota(jnp.int32, (n, n), 1)
    return c <= r


def _gla_prompt_kernel(q_ref, k_ref, v_ref, r_ref, gl_ref, wg2_ref, bg_ref, nw_ref,
                       o_ref, sout_ref, st_ref, *, t_rows, chunk, heads):
    t = pl.program_id(2)

    @pl.when(t == 0)
    def _():
        st_ref[...] = jnp.zeros_like(st_ref)

    glow = gl_ref[...].astype(BF16)
    causal = _causal(chunk)
    per_head = []
    for h in range(heads):
        dk = slice(h * GLA_DK, (h + 1) * GLA_DK)
        dv = slice(h * GLA_DV, (h + 1) * GLA_DV)
        k = k_ref[:, dk]
        cum, q_dec, k_inv = _gla_gate_and_decay(q_ref[:, dk], k, glow, wg2_ref[:, dk], bg_ref[:, dk], chunk)
        per_head.append((dv, cum, k, q_dec, k_inv, v_ref[:, dv].astype(BF16), []))
    for c in range(t_rows // chunk):
        lo, hi = c * chunk, (c + 1) * chunk
        for h, (dv, cum, k, q_dec, k_inv, v, outs) in enumerate(per_head):
            cum_c = cum[lo:hi]
            last = cum_c[chunk - 1:chunk, :]
            k_last = (k[lo:hi] * jnp.exp(last - cum_c)).astype(BF16)
            qd, ki, vc = q_dec[lo:hi], k_inv[lo:hi], v[lo:hi]
            attn = jnp.where(causal, _dot_nt(qd, ki), 0.0).astype(BF16)
            st = st_ref[h]
            outs.append(_dot(attn, vc) + _dot_nt(qd, st.astype(BF16)))
            st_ref[h] = st * jnp.exp(last) + _dot_tn(vc, k_last)
    for h, (dv, _, _, _, _, _, outs) in enumerate(per_head):
        o = jnp.concatenate(outs, axis=0)
        o_ref[:, dv] = _gla_finish(o, r_ref[:, dv], nw_ref[...]).astype(o_ref.dtype)

    @pl.when(t == pl.num_programs(2) - 1)
    def _():
        for h in range(heads):
            sout_ref[0, h] = st_ref[h].T


def _gla_prompt(proj, wg2p, bg, norm_w, batch, seq, t_rows):
    nt = seq // t_rows
    hs = GLA_HEADS_PER_STEP
    dk, dv = hs * GLA_DK, hs * GLA_DV
    row = lambda b, h, t: b * nt + t
    in_specs = [
        pl.BlockSpec((t_rows, dk), lambda b, h, t: (row(b, h, t), h)),
        pl.BlockSpec((t_rows, dk), lambda b, h, t: (row(b, h, t), GLA_QK // dk + h)),
        pl.BlockSpec((t_rows, dv), lambda b, h, t: (row(b, h, t), 2 * GLA_QK // dv + h)),
        pl.BlockSpec((t_rows, dv), lambda b, h, t: (row(b, h, t), (2 * GLA_QK + GLA_VD) // dv + h)),
        pl.BlockSpec((t_rows, GLA_GLOW_PAD), lambda b, h, t: (row(b, h, t), GLA_PROJ // GLA_GLOW_PAD)),
        pl.BlockSpec((GLA_GLOW_PAD, dk), lambda b, h, t: (0, h)),
        pl.BlockSpec((1, dk), lambda b, h, t: (0, h)),
        pl.BlockSpec((1, GLA_DV), lambda b, h, t: (0, 0)),
    ]
    return pl.pallas_call(
        functools.partial(_gla_prompt_kernel, t_rows=t_rows, chunk=GLA_CHUNK, heads=hs),
        grid=(batch, GLA_HEADS // hs, nt),
        in_specs=in_specs,
        out_specs=[pl.BlockSpec((t_rows, dv), lambda b, h, t: (row(b, h, t), h)),
                   pl.BlockSpec((1, hs, GLA_DK, GLA_DV), lambda b, h, t: (b, h, 0, 0))],
        out_shape=[jax.ShapeDtypeStruct((batch * seq, GLA_VD), BF16),
                   jax.ShapeDtypeStruct((batch, GLA_HEADS, GLA_DK, GLA_DV), F32)],
        scratch_shapes=[pltpu.VMEM((hs, GLA_DV, GLA_DK), F32)],
        compiler_params=_cparams("arbitrary", "arbitrary", "arbitrary"),
        name="gla_prompt",
    )(proj, proj, proj, proj, proj, wg2p, bg, norm_w)


def _gla_sample_kernel(*refs, seq, nb, n_real, n_earlier):
    q_ref, k_ref, v_ref, r_ref, gl_ref, wg2_ref, bg_ref, nw_ref, s0_ref = refs[:9]
    earlier_refs = refs[9:9 + n_earlier]
    o_ref, sout_ref = refs[9 + n_earlier:]
    new_state_ref = sout_ref.at[n_earlier] if n_earlier else sout_ref
    g = pl.program_id(1)

    @pl.when(g < n_real)
    def _():
        k = k_ref[...]
        cum, q_dec, k_inv = _gla_gate_and_decay(q_ref[...], k, gl_ref[...].astype(BF16), wg2_ref[...],
                                                bg_ref[...], seq)
        v = v_ref[...].astype(BF16)
        causal = _causal(seq)
        outs = []
        for b in range(nb):
            lo, hi = b * seq, (b + 1) * seq
            cum_b = cum[lo:hi]
            last = cum_b[seq - 1:seq, :]
            k_last = (k[lo:hi] * jnp.exp(last - cum_b)).astype(BF16)
            qd, ki, vc = q_dec[lo:hi], k_inv[lo:hi], v[lo:hi]
            attn = jnp.where(causal, _dot_nt(qd, ki), 0.0).astype(BF16)
            s = s0_ref[b]
            outs.append(_dot(attn, vc) + _dot(qd, s.astype(BF16)))
            decay = jnp.transpose(jnp.broadcast_to(jnp.exp(last), (128, GLA_DK)))[:, 0:1]
            new_state_ref[b] = s * decay + _dot_tn(k_last, vc)
        for e, earlier_ref in enumerate(earlier_refs):
            sout_ref[e] = earlier_ref[...]
        o = jnp.concatenate(outs, axis=0)
        o_ref[...] = _gla_finish(o, r_ref[...], nw_ref[...])

    @pl.when(g >= n_real)
    def _():
        o_ref[...] = jnp.zeros_like(o_ref)


def _gla_sample(proj, row0, wg2p, bg, norm_w, state, layer, batch, seq, nb, earlier=()):
    rows = nb * seq
    n_real = batch // nb
    blk0 = row0 // rows
    rb = lambda g: blk0 + jnp.minimum(g, n_real - 1)
    sb = lambda g: jnp.minimum(g, n_real - 1)
    in_specs = [
        pl.BlockSpec((rows, GLA_DK), lambda h, g: (rb(g), h)),
        pl.BlockSpec((rows, GLA_DK), lambda h, g: (rb(g), GLA_HEADS + h)),
        pl.BlockSpec((rows, GLA_DV), lambda h, g: (rb(g), 2 * GLA_QK // GLA_DV + h)),
        pl.BlockSpec((rows, GLA_DV), lambda h, g: (rb(g), (2 * GLA_QK + GLA_VD) // GLA_DV + h)),
        pl.BlockSpec((rows, GLA_GLOW_PAD), lambda h, g: (rb(g), GLA_PROJ // GLA_GLOW_PAD)),
        pl.BlockSpec((GLA_GLOW_PAD, GLA_DK), lambda h, g: (0, h)),
        pl.BlockSpec((1, GLA_DK), lambda h, g: (0, h)),
        pl.BlockSpec((1, GLA_DV), lambda h, g: (0, 0)),
        pl.BlockSpec((None, nb, None, GLA_DK, GLA_DV), lambda h, g: (layer, sb(g), h, 0, 0)),
    ]
    state_spec = pl.BlockSpec((nb, None, GLA_DK, GLA_DV), lambda h, g: (sb(g), h, 0, 0))
    state_shape = (batch, GLA_HEADS, GLA_DK, GLA_DV)
    n_earlier = len(earlier)
    if n_earlier:
        in_specs += [state_spec] * n_earlier
        out_state_spec = pl.BlockSpec((n_earlier + 1, nb, None, GLA_DK, GLA_DV), lambda h, g: (0, sb(g), h, 0, 0))
        state_shape = (n_earlier + 1,) + state_shape
    else:
        out_state_spec = state_spec
    return pl.pallas_call(
        functools.partial(_gla_sample_kernel, seq=seq, nb=nb, n_real=n_real, n_earlier=n_earlier),
        grid=(GLA_HEADS, SAMPLE_ROWS // rows),
        in_specs=in_specs,
        out_specs=[pl.BlockSpec((rows, GLA_DV), lambda h, g: (g, h)), out_state_spec],
        out_shape=[jax.ShapeDtypeStruct((SAMPLE_ROWS, GLA_VD), F32),
                   jax.ShapeDtypeStruct(state_shape, F32)],
        compiler_params=_cparams("arbitrary", "arbitrary"),
        name="gla_sample",
    )(proj, proj, proj, proj, proj, wg2p, bg, norm_w, state, *earlier)


def _residual_ln_store(y, res_ref, g_ref, b_ref, of_ref, ob_ref):
    o = _layer_norm(ALPHA * res_ref[...] + y, g_ref[...], b_ref[...])
    of_ref[...] = o
    ob_ref[...] = o.astype(BF16)


def _zero_outputs(of_ref, ob_ref):
    of_ref[...] = jnp.zeros_like(of_ref)
    ob_ref[...] = jnp.zeros_like(ob_ref)


def _deferred_projection(i, n_prompt_tiles, first, a_prompt, a_sample, wb_ref, y_ref,
                         epilogue, sample_epilogue, zero_fill):
    def project(a):
        y_ref[...] = _dot(a(), wb_ref[...])

    @pl.when(i == 0)
    def _():
        first()
        project(a_prompt)

    @pl.when((i > 0) & (i < n_prompt_tiles))
    def _():
        epilogue()
        project(a_prompt)

    @pl.when(i == n_prompt_tiles)
    def _():
        epilogue()
        project(a_sample)

    pl.when(i == n_prompt_tiles + 1)(sample_epilogue)
    if zero_fill is not None:
        pl.when(i > n_prompt_tiles + 1)(zero_fill)


def _mixer_out_kernel(*refs, has_bias, n_prompt_tiles):
    if has_bias:
        ap_ref, as_ref, w_ref, bias_ref, resp_ref, ress_ref, g_ref, b_ref, of_ref, ob_ref, wb_ref, y_ref = refs
    else:
        ap_ref, as_ref, w_ref, resp_ref, ress_ref, g_ref, b_ref, of_ref, ob_ref, wb_ref, y_ref = refs

    def first():
        wb_ref[...] = w_ref[...].astype(BF16)

    def epilogue(res_ref):
        y = y_ref[...] + bias_ref[...] if has_bias else y_ref[...]
        _residual_ln_store(y, res_ref, g_ref, b_ref, of_ref, ob_ref)

    _deferred_projection(pl.program_id(0), n_prompt_tiles, first, lambda: ap_ref[...],
                         lambda: as_ref[...].astype(BF16), wb_ref, y_ref,
                         functools.partial(epilogue, resp_ref), functools.partial(epilogue, ress_ref),
                         lambda: _zero_outputs(of_ref, ob_ref))


def _mixer_out(a_prompt, a_sample, w, layer, bias, res_prompt, res_sample, res_sample_tile, m, g, b, name):
    k = a_prompt.shape[1]
    tm = TM_OUT
    n_prompt_tiles = a_prompt.shape[0] // tm
    assert a_sample.shape[0] == tm
    has_bias = bias is not None
    vec = pl.BlockSpec((1, D_MODEL), lambda i: (0, 0))
    lagged = pl.BlockSpec((tm, D_MODEL), lambda i: (_lag(i), 0))
    in_specs = [pl.BlockSpec((tm, k), lambda i: (jnp.minimum(i, n_prompt_tiles - 1), 0)),
                pl.BlockSpec((tm, k), lambda i: (0, 0)),
                _single_buffered((None, k, D_MODEL), lambda i: (layer, 0, 0))]
    args = [a_prompt, a_sample, w]
    if has_bias:
        in_specs.append(vec)
        args.append(bias)
    in_specs += [pl.BlockSpec((tm, D_MODEL), lambda i: (jnp.minimum(_lag(i), n_prompt_tiles - 1), 0)),
                 pl.BlockSpec((tm, D_MODEL), lambda i: (res_sample_tile, 0)), vec, vec]
    args += [res_prompt, res_sample, g, b]
    return pl.pallas_call(
        functools.partial(_mixer_out_kernel, has_bias=has_bias, n_prompt_tiles=n_prompt_tiles),
        grid=(m // tm + 1,),
        in_specs=in_specs,
        out_specs=[lagged, lagged],
        out_shape=[jax.ShapeDtypeStruct((m, D_MODEL), F32),
                   jax.ShapeDtypeStruct((m, D_MODEL), BF16)],
        scratch_shapes=[pltpu.VMEM((k, D_MODEL), BF16), pltpu.VMEM((tm, D_MODEL), F32)],
        compiler_params=_cparams("arbitrary"),
        name=name,
    )(*args)


def _ffn_out_kernel(ap_ref, as_ref, w_ref, res_ref, g_ref, b_ref, of_ref, ob_ref, y_ref, *, n_prompt_tiles):
    def epilogue():
        _residual_ln_store(y_ref[...], res_ref, g_ref, b_ref, of_ref, ob_ref)

    _deferred_projection(pl.program_id(0), n_prompt_tiles, lambda: None, lambda: ap_ref[...], lambda: as_ref[...],
                         w_ref, y_ref, epilogue, epilogue, lambda: _zero_outputs(of_ref, ob_ref))


def _ffn_out_final_kernel(ap_ref, as_ref, w_ref, res_ref, g_ref, b_ref, op_ref, os_ref, y_ref, *, n_prompt_tiles):
    def ln():
        return _layer_norm(ALPHA * res_ref[...] + y_ref[...], g_ref[...], b_ref[...])

    def epilogue():
        op_ref[...] = ln()

    def sample_epilogue():
        os_ref[...] = ln()

    _deferred_projection(pl.program_id(0), n_prompt_tiles, lambda: None, lambda: ap_ref[...], lambda: as_ref[...],
                         w_ref, y_ref, epilogue, sample_epilogue, None)


def _ffn_out(a_prompt, a_sample, w, layer, res, g, b, final):
    m = res.shape[0]
    k = a_prompt.shape[1]
    tm = TM_OUT
    n_prompt_tiles = a_prompt.shape[0] // tm
    assert a_sample.shape[0] == tm
    vec = pl.BlockSpec((1, D_MODEL), lambda i: (0, 0))
    lagged = pl.BlockSpec((tm, D_MODEL), lambda i: (_lag(i), 0))
    in_specs = [pl.BlockSpec((tm, k), lambda i: (jnp.minimum(i, n_prompt_tiles - 1), 0)),
                pl.BlockSpec((tm, k), lambda i: (0, 0)),
                _single_buffered((None, k, D_MODEL), lambda i: (layer, 0, 0)),
                lagged, vec, vec]
    scratch = [pltpu.VMEM((tm, D_MODEL), F32)]
    if final:
        return pl.pallas_call(
            functools.partial(_ffn_out_final_kernel, n_prompt_tiles=n_prompt_tiles),
            grid=(n_prompt_tiles + 2,),
            in_specs=in_specs,
            out_specs=[pl.BlockSpec((tm, D_MODEL), lambda i: (jnp.minimum(_lag(i), n_prompt_tiles - 1), 0)),
                       pl.BlockSpec((tm, D_MODEL), lambda i: (0, 0))],
            out_shape=[jax.ShapeDtypeStruct((n_prompt_tiles * tm, D_MODEL), F32),
                       jax.ShapeDtypeStruct((tm, D_MODEL), F32)],
            scratch_shapes=scratch,
            compiler_params=_cparams("arbitrary"),
            name="ffn_out_final",
        )(a_prompt, a_sample, w, res, g, b)
    return pl.pallas_call(
        functools.partial(_ffn_out_kernel, n_prompt_tiles=n_prompt_tiles),
        grid=(m // tm + 1,),
        in_specs=in_specs,
        out_specs=[lagged, lagged],
        out_shape=[jax.ShapeDtypeStruct((m, D_MODEL), F32),
                   jax.ShapeDtypeStruct((m, D_MODEL), BF16)],
        scratch_shapes=scratch,
        compiler_params=_cparams("arbitrary"),
        name="ffn_out",
    )(a_prompt, a_sample, w, res, g, b)


def _conv_glu(g, val, g1, g2, cw, cb):
    conv = cb + cw[0:1, :] * g2
    conv = conv + cw[1:2, :] * g1
    conv = conv + cw[2:3, :] * g
    return (_gelu(conv) * val).astype(BF16)


def _ffn_in_kernel(xp_ref, xs_ref, wg_ref, wv_ref, cw_ref, cb_ref, p1_ref, p2_ref, u_ref, us_ref, glp_ref, gls_ref,
                   wgb_ref, wvb_ref, carry_ref, gs_ref, vs_ref, *, seq_tiles, n_prompt_tiles, sample_seq):
    i = pl.program_id(1)
    tm, tn = u_ref.shape
    rows_s = us_ref.shape[0]
    col_slices = [slice(c0, c0 + MXU_COLS) for c0 in range(0, tn, MXU_COLS)]

    def dots(x_ref):
        rows = x_ref.shape[0]
        x = x_ref[...]
        for cs in col_slices:
            gs_ref[0:rows, cs] = _dot(x, wgb_ref[:, cs])
            vs_ref[0:rows, cs] = _dot(x, wvb_ref[:, cs])

    def prompt_epilogue():
        row = lax.broadcasted_iota(jnp.int32, (tm, MXU_COLS), 0)
        first = (i - 1) % seq_tiles == 0
        for cs in col_slices:
            g, val = gs_ref[:, cs], vs_ref[:, cs]
            above = jnp.where(first, 0.0, carry_ref[:, cs])
            g1 = jnp.where(row >= 1, pltpu.roll(g, 1, axis=0), above[7:8, :])
            g2 = jnp.where(row >= 2, pltpu.roll(g, 2, axis=0),
                           jnp.where(row == 0, above[6:7, :], above[7:8, :]))
            u_ref[:, cs] = _conv_glu(g, val, g1, g2, cw_ref[:, cs], cb_ref[:, cs])
            carry_ref[:, cs] = g[tm - 8:tm, :]
            glp_ref[:, cs] = g[tm - 8:tm, :]

    @pl.when(i == 0)
    def _():
        wgb_ref[...] = wg_ref[...].astype(BF16)
        wvb_ref[...] = wv_ref[...].astype(BF16)
        dots(xp_ref)

    @pl.when((i > 0) & (i < n_prompt_tiles))
    def _():
        prompt_epilogue()
        dots(xp_ref)

    @pl.when(i == n_prompt_tiles)
    def _():
        prompt_epilogue()
        dots(xs_ref)

    @pl.when(i == n_prompt_tiles + 1)
    def _():
        pos = lax.broadcasted_iota(jnp.int32, (rows_s, MXU_COLS), 0) % sample_seq
        for cs in col_slices:
            g, val = gs_ref[0:rows_s, cs], vs_ref[0:rows_s, cs]
            g1 = jnp.where(pos >= 1, pltpu.roll(g, 1, axis=0), p1_ref[:, cs])
            g2 = jnp.where(pos >= 2, pltpu.roll(g, 2, axis=0), p2_ref[:, cs])
            us_ref[:, cs] = _conv_glu(g, val, g1, g2, cw_ref[:, cs], cb_ref[:, cs])
            gls_ref[:, cs] = g


def _ffn_in(xb, w_in, layer, conv_w, conv_b, prev1, prev2, n_prompt_seqs, seq, sample_seq, tn):
    tm = TM_FFN_IN
    mp = n_prompt_seqs * seq
    nj = D_FF // tn
    seq_tiles = seq // tm
    n_prompt_tiles = mp // tm
    sample = pl.BlockSpec((SAMPLE_ROWS, tn), lambda j, i: (0, j))
    prompt_tile = lambda i: jnp.minimum(_lag(i), n_prompt_tiles - 1)
    return pl.pallas_call(
        functools.partial(_ffn_in_kernel, seq_tiles=seq_tiles, n_prompt_tiles=n_prompt_tiles,
                          sample_seq=sample_seq),
        grid=(nj, n_prompt_tiles + 2),
        in_specs=[pl.BlockSpec((tm, D_MODEL), lambda j, i: (jnp.minimum(i, n_prompt_tiles - 1), 0)),
                  pl.BlockSpec((SAMPLE_ROWS, D_MODEL), lambda j, i: (mp // SAMPLE_ROWS, 0)),
                  pl.BlockSpec((None, D_MODEL, tn), lambda j, i: (layer, 0, j)),
                  pl.BlockSpec((None, D_MODEL, tn), lambda j, i: (layer, 0, nj + j)),
                  pl.BlockSpec((None, CONV_W, tn), lambda j, i: (layer, 0, j)),
                  pl.BlockSpec((None, 1, tn), lambda j, i: (layer, 0, j)),
                  sample, sample],
        out_specs=[pl.BlockSpec((tm, tn), lambda j, i: (prompt_tile(i), j)),
                   sample,
                   pl.BlockSpec((8, tn), lambda j, i: (prompt_tile(i) // seq_tiles, j)),
                   sample],
        out_shape=[jax.ShapeDtypeStruct((mp, D_FF), BF16),
                   jax.ShapeDtypeStruct((SAMPLE_ROWS, D_FF), BF16),
                   jax.ShapeDtypeStruct((n_prompt_seqs * 8, D_FF), F32),
                   jax.ShapeDtypeStruct((SAMPLE_ROWS, D_FF), F32)],
        scratch_shapes=[pltpu.VMEM((D_MODEL, tn), BF16), pltpu.VMEM((D_MODEL, tn), BF16),
                        pltpu.VMEM((8, tn), F32), pltpu.VMEM((tm, tn), F32), pltpu.VMEM((tm, tn), F32)],
        compiler_params=_cparams("arbitrary", "arbitrary"),
        name="ffn_in",
    )(xb, xb, w_in, w_in, conv_w, conv_b, prev1, prev2)


def _swa_qkv_kernel(x_ref, w_ref, b_ref, ca_ref, cm_ref, cp_ref, o_ref, wb_ref, y_ref,
                    *, n_rot_cols, n_prompt_tiles, n_tiles):
    j = pl.program_id(0)
    tm, tn = o_ref.shape

    def first():
        wb_ref[...] = w_ref[...].astype(BF16)

    def dots(rows):
        x = x_ref[0:rows, :]
        for c0 in range(0, tn, MXU_COLS):
            sl = slice(c0, c0 + MXU_COLS)
            y_ref[0:rows, sl] = _dot(x, wb_ref[:, sl])

    def epilogue(rows):
        reps = MXU_COLS // ca_ref.shape[1]
        ca = jnp.concatenate([ca_ref[0:rows, :]] * reps, axis=1)
        cm = jnp.concatenate([cm_ref[0:rows, :]] * reps, axis=1)
        cp = jnp.concatenate([cp_ref[0:rows, :]] * reps, axis=1)
        lane = lax.broadcasted_iota(jnp.int32, (rows, MXU_COLS), 1) % SWA_HEAD_DIM
        half = ROT_DIM // 2
        for c0 in range(0, tn, MXU_COLS):
            sl = slice(c0, c0 + MXU_COLS)
            rotated = lane < jnp.where(j * tn + c0 < n_rot_cols, ROT_DIM, 0)
            y = y_ref[0:rows, sl] + b_ref[:, sl]
            rot = y * ca + pltpu.roll(y, MXU_COLS - half, axis=1) * cm + pltpu.roll(y, half, axis=1) * cp
            o_ref[0:rows, sl] = jnp.where(rotated, rot, y)
        if rows < tm:
            o_ref[rows:tm, :] = jnp.zeros((tm - rows, tn), F32)

    _deferred_row_tiles(pl.program_id(1), n_prompt_tiles, n_tiles, tm, first, dots, epilogue)


def _rope_tables(pos):
    half = ROT_DIM // 2
    inv = ROPE_THETA ** (-jnp.arange(half, dtype=F32) / half)
    ang = pos.astype(F32)[:, None] * inv
    cos, sin = jnp.cos(ang), jnp.sin(ang)
    n = pos.shape[0]
    rest = SWA_HEAD_DIM - ROT_DIM
    ca = jnp.concatenate([cos, cos, jnp.ones((n, rest), F32)], -1)
    cm = jnp.concatenate([-sin, jnp.zeros((n, half + rest), F32)], -1)
    cp = jnp.concatenate([jnp.zeros((n, half), F32), sin, jnp.zeros((n, rest), F32)], -1)
    return tuple(jnp.tile(t, (1, 128 // SWA_HEAD_DIM)) for t in (ca, cm, cp))


def _swa_qkv(xb, w, layer, b, tables, n_prompt_tiles, seq, tn):
    m = xb.shape[0]
    n = w.shape[2]
    pos_tiles = seq // TM
    n_tiles = m // TM
    tab = pl.BlockSpec((TM, 128), lambda j, i: (jnp.where(_lag(i) < n_prompt_tiles, _lag(i) % pos_tiles, pos_tiles), 0))
    return pl.pallas_call(
        functools.partial(_swa_qkv_kernel, n_rot_cols=SWA_QD + SWA_KD, n_prompt_tiles=n_prompt_tiles,
                          n_tiles=n_tiles),
        grid=(n // tn, n_tiles + 1),
        in_specs=[pl.BlockSpec((TM, D_MODEL), lambda j, i: (jnp.minimum(i, n_tiles - 1), 0)),
                  pl.BlockSpec((None, D_MODEL, tn), lambda j, i: (layer, 0, j)),
                  pl.BlockSpec((1, tn), lambda j, i: (0, j)),
                  tab, tab, tab],
        out_specs=pl.BlockSpec((TM, tn), lambda j, i: (_lag(i), j)),
        out_shape=jax.ShapeDtypeStruct((m, n), F32),
        scratch_shapes=[pltpu.VMEM((D_MODEL, tn), BF16), pltpu.VMEM((TM, tn), F32)],
        compiler_params=_cparams("arbitrary", "arbitrary"),
        name="swa_qkv",
    )(xb, w, b, *tables)


SWA_SCALE = SWA_HEAD_DIM ** -0.5


def _sink_softmax_pv(s, mask, sink, v):
    s = jnp.where(mask, s, -jnp.inf)
    mx = jnp.maximum(jnp.max(s, -1, keepdims=True), sink)
    p = jnp.exp(s - mx)
    denom = jnp.sum(p, -1, keepdims=True) + jnp.exp(sink - mx)
    return _dot((p / denom).astype(BF16), v)


def _swa_prompt_kernel(sink_ref, q_ref, kp_ref, kc_ref, vp_ref, vc_ref, o_ref):
    i = pl.program_id(1)
    q = q_ref[...] * SWA_SCALE
    kb = jnp.concatenate([kp_ref[...], kc_ref[...]], axis=0).astype(BF16)
    vb = jnp.concatenate([vp_ref[...], vc_ref[...]], axis=0).astype(BF16)
    rows = SWA_GROUP * WINDOW
    r = lax.broadcasted_iota(jnp.int32, (rows, 2 * WINDOW), 0) % WINDOW
    c = lax.broadcasted_iota(jnp.int32, (rows, 2 * WINDOW), 1)
    mask = (c > r) & (c <= r + WINDOW) & (c >= jnp.where(i > 0, 0, WINDOW))
    grp = lax.broadcasted_iota(jnp.int32, (rows, 1), 0) // WINDOW
    pieces = []
    for h in range(SWA_KV_HEADS):
        kh = kb[:, h * SWA_HEAD_DIM:(h + 1) * SWA_HEAD_DIM]
        vh = vb[:, h * SWA_HEAD_DIM:(h + 1) * SWA_HEAD_DIM]
        qs = jnp.concatenate(
            [q[:, (h * SWA_GROUP + g) * SWA_HEAD_DIM:(h * SWA_GROUP + g + 1) * SWA_HEAD_DIM]
             for g in range(SWA_GROUP)], axis=0).astype(BF16)
        sink = jnp.zeros((rows, 1), F32)
        for g in range(SWA_GROUP):
            sink = jnp.where(grp == g, sink_ref[h * SWA_GROUP + g], sink)
        o = _sink_softmax_pv(_dot_nt(qs, kh), mask, sink, vh)
        pieces += [o[g * WINDOW:(g + 1) * WINDOW, :] for g in range(SWA_GROUP)]
    o_ref[...] = jnp.concatenate(pieces, axis=-1).astype(o_ref.dtype)


def _swa_prompt_attn(qkv, sinks, batch, seq):
    nb = seq // WINDOW
    kcol = SWA_QD // SWA_KD
    cur = lambda b, i: b * nb + i
    prev = lambda b, i: b * nb + jnp.maximum(i - 1, 0)
    return pl.pallas_call(
        _swa_prompt_kernel,
        grid=(batch, nb),
        in_specs=[pl.BlockSpec(memory_space=pltpu.SMEM),
                  pl.BlockSpec((WINDOW, SWA_QD), lambda b, i: (cur(b, i), 0)),
                  pl.BlockSpec((WINDOW, SWA_KD), lambda b, i: (prev(b, i), kcol)),
                  pl.BlockSpec((WINDOW, SWA_KD), lambda b, i: (cur(b, i), kcol)),
                  pl.BlockSpec((WINDOW, SWA_KD), lambda b, i: (prev(b, i), kcol + 1)),
                  pl.BlockSpec((WINDOW, SWA_KD), lambda b, i: (cur(b, i), kcol + 1))],
        out_specs=pl.BlockSpec((WINDOW, SWA_QD), lambda b, i: (cur(b, i), 0)),
        out_shape=jax.ShapeDtypeStruct((batch * seq, SWA_QD), BF16),
        compiler_params=_cparams("arbitrary", "arbitrary"),
        name="swa_prompt_attn",
    )(sinks, qkv, qkv, qkv, qkv, qkv)


def _swa_sample_kernel(sink_ref, q_ref, kn_ref, vn_ref, kc_ref, vc_ref, o_ref, ko_ref, vo_ref,
                       *, seq, nb, n_real):
    step = pl.program_id(0)
    nbuf = kc_ref.shape[1]

    @pl.when(step < n_real)
    def _():
        rows = SWA_GROUP * seq
        l = lax.broadcasted_iota(jnp.int32, (rows, nbuf + seq), 0) % seq
        c = lax.broadcasted_iota(jnp.int32, (rows, nbuf + seq), 1)
        diff = l + nbuf - c
        mask = (diff >= 0) & (diff < WINDOW)
        grp = lax.broadcasted_iota(jnp.int32, (rows, 1), 0) // seq
        for b in range(nb):
            lo, hi = b * seq, (b + 1) * seq
            k_all = jnp.concatenate([kc_ref[b], kn_ref[lo:hi, :]], axis=0)
            v_all = jnp.concatenate([vc_ref[b], vn_ref[lo:hi, :]], axis=0)
            ko_ref[b] = k_all[seq:, :]
            vo_ref[b] = v_all[seq:, :]
            kb = k_all.astype(BF16)
            vb = v_all.astype(BF16)
            q = q_ref[lo:hi, :] * SWA_SCALE
            pieces = []
            for h in range(SWA_KV_HEADS):
                kh = kb[:, h * SWA_HEAD_DIM:(h + 1) * SWA_HEAD_DIM]
                vh = vb[:, h * SWA_HEAD_DIM:(h + 1) * SWA_HEAD_DIM]
                qs = jnp.concatenate(
                    [q[:, (h * SWA_GROUP + g) * SWA_HEAD_DIM:(h * SWA_GROUP + g + 1) * SWA_HEAD_DIM]
                     for g in range(SWA_GROUP)], axis=0).astype(BF16)
                sink = jnp.zeros((rows, 1), F32)
                for g in range(SWA_GROUP):
                    sink = jnp.where(grp == g, sink_ref[h * SWA_GROUP + g], sink)
                o = _sink_softmax_pv(_dot_nt(qs, kh), mask, sink, vh)
                pieces += [o[g * seq:(g + 1) * seq, :] for g in range(SWA_GROUP)]
            o_ref[lo:hi, :] = jnp.concatenate(pieces, axis=-1)

    @pl.when(step >= n_real)
    def _():
        o_ref[...] = jnp.zeros_like(o_ref)


def _swa_sample_attn(qkv, row0, sinks, k_cache, v_cache, batch, seq, nb):
    nbuf = k_cache.shape[1]
    kcol = SWA_QD // SWA_KD
    rows = nb * seq
    n_real = batch // nb
    blk0 = row0 // rows
    rb = lambda s: blk0 + jnp.minimum(s, n_real - 1)
    cache = pl.BlockSpec((nb, nbuf, SWA_KD), lambda s: (jnp.minimum(s, n_real - 1), 0, 0))
    return pl.pallas_call(
        functools.partial(_swa_sample_kernel, seq=seq, nb=nb, n_real=n_real),
        grid=(SAMPLE_ROWS // rows,),
        in_specs=[pl.BlockSpec(memory_space=pltpu.SMEM),
                  pl.BlockSpec((rows, SWA_QD), lambda s: (rb(s), 0)),
                  pl.BlockSpec((rows, SWA_KD), lambda s: (rb(s), kcol)),
                  pl.BlockSpec((rows, SWA_KD), lambda s: (rb(s), kcol + 1)),
                  cache, cache],
        out_specs=[pl.BlockSpec((rows, SWA_QD), lambda s: (s, 0)), cache, cache],
        out_shape=[jax.ShapeDtypeStruct((SAMPLE_ROWS, SWA_QD), F32),
                   jax.ShapeDtypeStruct((batch, nbuf, SWA_KD), F32),
                   jax.ShapeDtypeStruct((batch, nbuf, SWA_KD), F32)],
        compiler_params=_cparams("arbitrary"),
        name="swa_sample_attn",
    )(sinks, qkv, qkv, qkv, k_cache, v_cache)


def _sg_in_kernel(x_ref, w_ref, b_ref, g_ref, beta_ref, o_ref, wb_ref, *, n_prompt_tiles):
    j = pl.program_id(0)
    tm = o_ref.shape[0]

    @pl.when(pl.program_id(1) == 0)
    def _():
        wb_ref[...] = w_ref[...].astype(BF16)

    def body(rows, normalise):
        z = _gelu(_dot(x_ref[0:rows, :], wb_ref[...]) + b_ref[...])
        o_ref[0:rows, :] = _layer_norm(z, g_ref[...], beta_ref[...]) if normalise else z
        if rows < tm:
            o_ref[rows:tm, :] = jnp.zeros((tm - rows, o_ref.shape[1]), F32)

    for half, normalise in ((0, False), (1, True)):
        pl.when(j == half)(functools.partial(
            _per_row_tile, pl.program_id(1), n_prompt_tiles, tm, functools.partial(body, normalise=normalise)))


def _sg_in(xb, w, layer, b, ln_g, ln_b, n_prompt_tiles):
    m = xb.shape[0]
    tm = TM_SG_IN
    vec = pl.BlockSpec((1, SG_WIDTH), lambda j, i: (0, 0))
    return pl.pallas_call(
        functools.partial(_sg_in_kernel, n_prompt_tiles=n_prompt_tiles),
        grid=(2, m // tm),
        in_specs=[pl.BlockSpec((tm, D_MODEL), lambda j, i: (i, 0)),
                  _single_buffered((None, D_MODEL, SG_WIDTH), lambda j, i: (layer, 0, j)),
                  pl.BlockSpec((1, SG_WIDTH), lambda j, i: (0, j)),
                  vec, vec],
        out_specs=pl.BlockSpec((None, tm, SG_WIDTH), lambda j, i: (j, i, 0)),
        out_shape=jax.ShapeDtypeStruct((2, m, SG_WIDTH), F32),
        scratch_shapes=[pltpu.VMEM((D_MODEL, SG_WIDTH), BF16)],
        compiler_params=_cparams("arbitrary", "arbitrary"),
        name="sg_in",
    )(xb, w, b, ln_g, ln_b)


def _sg_mix(u_ref, v_ref, ws_ref, bs_ref, chunk, period):
    tm = u_ref.shape[0]
    r = lax.broadcasted_iota(jnp.int32, (chunk, chunk), 0)
    c = lax.broadcasted_iota(jnp.int32, (chunk, chunk), 1)
    mask = (c <= r) & (r // period == c // period)
    ws = [jnp.where(mask, ws_ref[g], 0.0).astype(BF16) for g in range(SG_GROUPS)]
    rows = []
    for n in range(tm // chunk):
        lo, hi = n * chunk, (n + 1) * chunk
        cols = []
        for g in range(SG_GROUPS):
            vg = v_ref[lo:hi, g * SG_GW:(g + 1) * SG_GW].astype(BF16)
            mixed = _dot(ws[g], vg) + bs_ref[:, g:g + 1]
            cols.append((u_ref[lo:hi, g * SG_GW:(g + 1) * SG_GW] * mixed).astype(BF16))
        rows.append(jnp.concatenate(cols, axis=-1))
    return jnp.concatenate(rows, axis=0)


def _sg_out_kernel(u_ref, v_ref, wsp_ref, bsp_ref, wss_ref, bss_ref, w_ref, bias_ref, res_ref, g_ref, b_ref,
                   of_ref, ob_ref, wb_ref, y_ref, *, n_prompt_tiles, sample_seq):
    def first():
        wb_ref[...] = w_ref[...].astype(BF16)

    def epilogue():
        _residual_ln_store(y_ref[...] + bias_ref[...], res_ref, g_ref, b_ref, of_ref, ob_ref)

    _deferred_projection(pl.program_id(0), n_prompt_tiles, first,
                         lambda: _sg_mix(u_ref, v_ref, wsp_ref, bsp_ref, SG_CHUNK, SG_CHUNK),
                         lambda: _sg_mix(u_ref, v_ref, wss_ref, bss_ref, wss_ref.shape[1], sample_seq),
                         wb_ref, y_ref, epilogue, epilogue, lambda: _zero_outputs(of_ref, ob_ref))


def _sg_out(uv, ws_p, bs_p, ws_s, bs_s, w_out, layer, b_out, res, g, b, n_prompt_tiles, sample_seq):
    m = res.shape[0]
    tm = TM_OUT
    vec = pl.BlockSpec((1, D_MODEL), lambda i: (0, 0))
    full = lambda a: pl.BlockSpec(a.shape, lambda i: (0,) * a.ndim)
    lagged = pl.BlockSpec((tm, D_MODEL), lambda i: (_lag(i), 0))
    return pl.pallas_call(
        functools.partial(_sg_out_kernel, n_prompt_tiles=n_prompt_tiles, sample_seq=sample_seq),
        grid=(m // tm + 1,),
        in_specs=[pl.BlockSpec((None, tm, SG_WIDTH), lambda i: (0, jnp.minimum(i, n_prompt_tiles), 0)),
                  pl.BlockSpec((None, tm, SG_WIDTH), lambda i: (1, jnp.minimum(i, n_prompt_tiles), 0)),
                  full(ws_p), full(bs_p), full(ws_s), full(bs_s),
                  _single_buffered((None, SG_WIDTH, D_MODEL), lambda i: (layer, 0, 0)),
                  vec, lagged, vec, vec],
        out_specs=[lagged, lagged],
        out_shape=[jax.ShapeDtypeStruct((m, D_MODEL), F32),
                   jax.ShapeDtypeStruct((m, D_MODEL), BF16)],
        scratch_shapes=[pltpu.VMEM((SG_WIDTH, D_MODEL), BF16), pltpu.VMEM((tm, D_MODEL), F32)],
        compiler_params=_cparams("arbitrary"),
        name="sg_out",
    )(uv, uv, ws_p, bs_p, ws_s, bs_s, w_out, b_out, res, g, b)


def kernel(x_prompt, x_sample, state_gla, cache_swa_k, cache_swa_v, state_ffn_conv, ln_mix_g, ln_mix_b, ln_ffn_g, ln_ffn_b, gla_w_in, gla_w_g2, gla_b_g, gla_norm_w, gla_w_out, swa_w_qkv, swa_b_qkv, swa_sinks, swa_w_out, swa_b_out, sg_w_in, sg_b_in, sg_ln_g, sg_ln_b, sg_w_s, sg_b_s, sg_w_out, sg_b_out, ffn_w_in, ffn_conv_w, ffn_conv_b, ffn_w_out):
    bp, lp, _ = x_prompt.shape
    bs, ls, _ = x_sample.shape
    mp, ms = bp * lp, bs * ls
    assert lp % TM == 0 and ms <= SAMPLE_ROWS and SAMPLE_ROWS % ls == 0 and ls >= CONV_W - 1
    assert SAMPLE_ROWS == TM_OUT and SAMPLE_PAD % TM_SG_IN == 0

    m_rows = mp + SAMPLE_PAD
    xp2, xs2 = x_prompt.reshape(mp, D_MODEL), x_sample.reshape(ms, D_MODEL)
    xb = jnp.concatenate([xp2.astype(BF16), xs2.astype(BF16), jnp.zeros((SAMPLE_PAD - ms, D_MODEL), BF16)], axis=0)
    residual = (xp2, jnp.pad(xs2, ((0, TM_OUT - ms), (0, 0))), 0)
    gla_w_proj = gla_w_in.astype(BF16)
    gla_wg2p = jnp.pad(gla_w_g2.astype(BF16), ((0, 0), (0, GLA_GLOW_PAD - GLA_RANK), (0, 0)))
    ffn_w_out_b = ffn_w_out.astype(BF16)
    conv_b3 = ffn_conv_b.reshape(DEPTH, 1, D_FF)

    gla_p, gla_s, swk_p, swv_p, swk_s, swv_s, sgv_s, conv_p, conv_s = ([] for _ in range(9))
    for i in range(DEPTH):
        j = i // N_MIXERS
        kind = i % N_MIXERS
        ln_g, ln_b = _row(ln_mix_g[i]), _row(ln_mix_b[i])
        if kind == 0:
            proj = _gla_proj(xb, gla_w_proj, j, mp // TM)
            bg, nw = _row(gla_b_g[j]), _row(gla_norm_w[j])
            og_p, st_p = _gla_prompt(proj, gla_wg2p[j], bg, nw, bp, lp, 256)
            last_gla = j == gla_w_in.shape[0] - 1
            og_s, st_s = _gla_sample(proj, mp, gla_wg2p[j], bg, nw, state_gla, j, bs, ls, 8 if bs % 8 == 0 else 1,
                                     earlier=tuple(gla_s) if last_gla else ())
            gla_p.append(st_p)
            gla_s.append(st_s)
            xf, xb = _mixer_out(og_p, og_s, gla_w_out, j, None, *residual, m_rows, ln_g, ln_b, "gla_out")
        elif kind == 1:
            pos = jnp.concatenate([jnp.arange(lp), PAST_LEN + jnp.arange(TM) % ls])
            qkv = _swa_qkv(xb, swa_w_qkv, j, _row(swa_b_qkv[j]), _rope_tables(pos), mp // TM, lp, 1024)
            sinks = swa_sinks[j].reshape(-1)
            oa_p = _swa_prompt_attn(qkv, sinks, bp, lp)
            kv = jnp.stack([qkv[(b + 1) * lp - WINDOW:(b + 1) * lp, SWA_QD:] for b in range(bp)])
            swk_p.append(kv[..., :SWA_KD].reshape(bp, WINDOW, SWA_KV_HEADS, SWA_HEAD_DIM))
            swv_p.append(kv[..., SWA_KD:].reshape(bp, WINDOW, SWA_KV_HEADS, SWA_HEAD_DIM))
            nbuf = cache_swa_k.shape[2]
            oa_s, ko, vo = _swa_sample_attn(qkv, mp, sinks, cache_swa_k[j].reshape(bs, nbuf, SWA_KD),
                                            cache_swa_v[j].reshape(bs, nbuf, SWA_KD), bs, ls,
                                            2 if bs % 2 == 0 else 1)
            swk_s.append(ko.reshape(bs, nbuf, SWA_KV_HEADS, SWA_HEAD_DIM))
            swv_s.append(vo.reshape(bs, nbuf, SWA_KV_HEADS, SWA_HEAD_DIM))
            xf, xb = _mixer_out(oa_p, oa_s, swa_w_out, j, _row(swa_b_out[j]), *residual, m_rows, ln_g, ln_b,
                                "swa_out")
        else:
            uv = _sg_in(xb, sg_w_in, j, _row(sg_b_in[j]), _row(sg_ln_g[j]), _row(sg_ln_b[j]), mp // TM_SG_IN)
            sgv_s.append(uv[1, mp:mp + ms].reshape(bs, ls, SG_WIDTH))
            reps = TM_OUT // ls
            ws_s = jnp.tile(sg_w_s[j][:, :ls, :ls], (1, reps, reps))
            bs_s = jnp.tile(sg_b_s[j][:, :ls].T, (reps, 1))
            xf, xb = _sg_out(uv, sg_w_s[j], sg_b_s[j].T, ws_s, bs_s, sg_w_out, j, _row(sg_b_out[j]),
                             xf, ln_g, ln_b, mp // TM_OUT, ls)
        st = state_ffn_conv[i]
        prev1 = jnp.pad(st[:, 1:2], ((0, 0), (0, ls - 1), (0, 0))).reshape(ms, D_FF)
        prev2 = jnp.pad(st, ((0, 0), (0, ls - 2), (0, 0))).reshape(ms, D_FF)
        prev1 = jnp.pad(prev1, ((0, SAMPLE_ROWS - ms), (0, 0)))
        prev2 = jnp.pad(prev2, ((0, SAMPLE_ROWS - ms), (0, 0)))
        u_p, u_s, gl_p, gl_s = _ffn_in(xb, ffn_w_in, i, ffn_conv_w, conv_b3, prev1, prev2, bp, lp, ls, 512)
        conv_p.append(gl_p.reshape(bp, 8, D_FF)[:, 8 - (CONV_W - 1):])
        conv_s.append(gl_s[:ms].reshape(bs, ls, D_FF)[:, ls - (CONV_W - 1):])
        final = i == DEPTH - 1
        outs = _ffn_out(u_p, u_s, ffn_w_out_b, i, xf, _row(ln_ffn_g[i]), _row(ln_ffn_b[i]), final)
        if final:
            y_prompt_rows, y_sample_tile = outs
        else:
            xf, xb = outs
            residual = (xf, xf, mp // TM_OUT)

    yp = y_prompt_rows.reshape(bp, lp, D_MODEL)
    ys = y_sample_tile[:ms].reshape(bs, ls, D_MODEL)
    gla_s_all = gla_s[-1] if len(gla_s) > 1 else jnp.stack(gla_s)
    return (yp, ys, jnp.stack(gla_p), gla_s_all, jnp.stack(swk_p), jnp.stack(swv_p),
            jnp.stack(swk_s), jnp.stack(swv_s), jnp.stack(sgv_s), jnp.stack(conv_p), jnp.stack(conv_s))
```

```python
import functools

import jax
import jax.numpy as jnp
from jax import lax
from jax.experimental import pallas as pl
from jax.experimental.pallas import tpu as pltpu

F32 = jnp.float32
BF16 = jnp.bfloat16

D_MODEL = 2048
DEPTH = 4
PAST_LEN = 16384
N_MIXERS = 3
ALPHA = (2 * DEPTH) ** 0.25
LN_EPS = 1e-5

GLA_HEADS = 4
GLA_DK = 256
GLA_DV = 512
GLA_RANK = 16
GLA_TAU = 16.0
GLA_CHUNK = 64
GLA_QK = GLA_HEADS * GLA_DK
GLA_VD = GLA_HEADS * GLA_DV
GLA_PROJ = 2 * GLA_QK + 2 * GLA_VD
GLA_GLOW_PAD = 128
GLA_PROJ_TN = 1280
GLA_PROJ_PAD = 5 * GLA_PROJ_TN
GLA_HEADS_PER_STEP = 4

SWA_HEAD_DIM = 64
SWA_Q_HEADS = 32
SWA_KV_HEADS = 8
SWA_GROUP = 4
WINDOW = 128
ROT_DIM = 16
ROPE_THETA = 500000.0
SWA_QD = SWA_Q_HEADS * SWA_HEAD_DIM
SWA_KD = SWA_KV_HEADS * SWA_HEAD_DIM

SG_WIDTH = 2048
SG_GROUPS = 4
SG_GW = SG_WIDTH // SG_GROUPS
SG_CHUNK = 128

D_FF = 5632
CONV_W = 3

TM = 1024
TM_FFN_IN = 1024
TM_SG_IN = 512
TM_OUT = 256
SAMPLE_PAD = TM
SAMPLE_ROWS = 256
MXU_COLS = 256
VMEM_LIMIT_BYTES = 56 * 1024 * 1024


def _cparams(*sem):
    return pltpu.CompilerParams(dimension_semantics=sem, vmem_limit_bytes=VMEM_LIMIT_BYTES)


def _dot(a, b):
    return jnp.dot(a, b, preferred_element_type=F32)


def _dot_nt(a, b):
    return lax.dot_general(a, b, (((1,), (1,)), ((), ())), preferred_element_type=F32)


def _dot_tn(a, b):
    return lax.dot_general(a, b, (((0,), (0,)), ((), ())), preferred_element_type=F32)


def _layer_norm(x, g, b):
    mu = jnp.mean(x, -1, keepdims=True)
    xc = x - mu
    var = jnp.mean(xc * xc, -1, keepdims=True)
    return xc * lax.rsqrt(var + LN_EPS) * g + b


def _gelu(x):
    return 0.5 * x * (1.0 + lax.erf(x * (0.5 ** 0.5)))


def _log_sigmoid(x):
    return jnp.minimum(x, 0.0) - jnp.log1p(jnp.exp(-jnp.abs(x)))


def _cumsum_rows(x, period):
    rowmod = lax.broadcasted_iota(jnp.int32, x.shape, 0) % period
    s = 1
    while s < period:
        x = x + jnp.where(rowmod >= s, pltpu.roll(x, s, axis=0), 0.0)
        s *= 2
    return x


def _single_buffered(shape, index_map):
    return pl.BlockSpec(shape, index_map, pipeline_mode=pl.Buffered(1))


def _row(v):
    return v.reshape(1, -1)


def _per_row_tile(i, n_prompt_tiles, tm, body):
    pl.when(i < n_prompt_tiles)(lambda: body(tm))
    pl.when(i >= n_prompt_tiles)(lambda: body(min(tm, SAMPLE_ROWS)))


def _lag(i):
    return jnp.maximum(i - 1, 0)


def _deferred_row_tiles(i, n_prompt_tiles, n_tiles, tm, first, dots, epilogue):
    s = min(tm, SAMPLE_ROWS)

    @pl.when(i == 0)
    def _():
        first()
        dots(tm)

    @pl.when((i > 0) & (i < n_prompt_tiles))
    def _():
        epilogue(tm)
        dots(tm)

    @pl.when(i == n_prompt_tiles)
    def _():
        epilogue(tm)
        dots(s)

    if n_tiles > n_prompt_tiles + 1:
        @pl.when((i > n_prompt_tiles) & (i < n_tiles))
        def _():
            epilogue(s)
            dots(s)

    pl.when(i == n_tiles)(lambda: epilogue(s))


def _gla_proj_kernel(x_ref, w_ref, o_ref, wz_ref, *, n_prompt_tiles, n_valid):
    tm, tn = o_ref.shape

    @pl.when(pl.program_id(1) == 0)
    def _():
        w = w_ref[...]
        col = lax.broadcasted_iota(jnp.int32, w.shape, 1) + pl.program_id(0) * tn
        wz_ref[...] = jnp.where(col < n_valid, w, jnp.zeros_like(w))

    def body(rows):
        o_ref[0:rows, :] = _dot(x_ref[0:rows, :], wz_ref[...])
        if rows < tm:
            o_ref[rows:tm, :] = jnp.zeros((tm - rows, tn), F32)

    _per_row_tile(pl.program_id(1), n_prompt_tiles, tm, body)


def _gla_proj(xb, w_in_b, layer, n_prompt_tiles):
    m, k = xb.shape
    tn = GLA_PROJ_TN
    return pl.pallas_call(
        functools.partial(_gla_proj_kernel, n_prompt_tiles=n_prompt_tiles, n_valid=w_in_b.shape[2]),
        grid=(GLA_PROJ_PAD // tn, m // TM),
        in_specs=[pl.BlockSpec((TM, k), lambda j, i: (i, 0)),
                  pl.BlockSpec((None, k, tn), lambda j, i: (layer, 0, j))],
        out_specs=pl.BlockSpec((TM, tn), lambda j, i: (i, j)),
        out_shape=jax.ShapeDtypeStruct((m, GLA_PROJ_PAD), F32),
        scratch_shapes=[pltpu.VMEM((k, tn), BF16)],
        compiler_params=_cparams("arbitrary", "arbitrary"),
        name="gla_proj",
    )(xb, w_in_b)


def _gla_gate_and_decay(q, k, glow_bf16, wg2, bg, period):
    ga = _dot(glow_bf16, wg2) + bg
    cum = _cumsum_rows(_log_sigmoid(ga) * (1.0 / GLA_TAU), period)
    q_dec = (q * (GLA_DK ** -0.5) * jnp.exp(cum)).astype(BF16)
    k_inv = (k * jnp.exp(-cum)).astype(BF16)
    return cum, q_dec, k_inv


def _gla_finish(o, r, nw):
    o = o * lax.rsqrt(jnp.mean(o * o, -1, keepdims=True) + LN_EPS) * nw
    return o * (r * jax.nn.sigmoid(r))


def _causal(n):
    r = lax.broadcasted_iota(jnp.int32, (n, n), 0)
    c = lax.broadcasted_iota(jnp.int32, (n, n), 1)
    return c <= r


def _gla_prompt_kernel(q_ref, k_ref, v_ref, r_ref, gl_ref, wg2_ref, bg_ref, nw_ref,
                       o_ref, sout_ref, st_ref, *, t_rows, chunk, heads):
    t = pl.program_id(2)

    @pl.when(t == 0)
    def _():
        st_ref[...] = jnp.zeros_like(st_ref)

    glow = gl_ref[...].astype(BF16)
    causal = _causal(chunk)
    per_head = []
    for h in range(heads):
        dk = slice(h * GLA_DK, (h + 1) * GLA_DK)
        dv = slice(h * GLA_DV, (h + 1) * GLA_DV)
        k = k_ref[:, dk]
        cum, q_dec, k_inv = _gla_gate_and_decay(q_ref[:, dk], k, glow, wg2_ref[:, dk], bg_ref[:, dk], chunk)
        per_head.append((dv, cum, k, q_dec, k_inv, v_ref[:, dv].astype(BF16), []))
    for c in range(t_rows // chunk):
        lo, hi = c * chunk, (c + 1) * chunk
        for h, (dv, cum, k, q_dec, k_inv, v, outs) in enumerate(per_head):
            cum_c = cum[lo:hi]
            last = cum_c[chunk - 1:chunk, :]
            k_last = (k[lo:hi] * jnp.exp(last - cum_c)).astype(BF16)
            qd, ki, vc = q_dec[lo:hi], k_inv[lo:hi], v[lo:hi]
            attn = jnp.where(causal, _dot_nt(qd, ki), 0.0).astype(BF16)
            st = st_ref[h]
            outs.append(_dot(attn, vc) + _dot_nt(qd, st.astype(BF16)))
            st_ref[h] = st * jnp.exp(last) + _dot_tn(vc, k_last)
    for h, (dv, _, _, _, _, _, outs) in enumerate(per_head):
        o = jnp.concatenate(outs, axis=0)
        o_ref[:, dv] = _gla_finish(o, r_ref[:, dv], nw_ref[...]).astype(o_ref.dtype)

    @pl.when(t == pl.num_programs(2) - 1)
    def _():
        for h in range(heads):
            sout_ref[0, h] = st_ref[h].T


def _gla_prompt(proj, wg2p, bg, norm_w, batch, seq, t_rows):
    nt = seq // t_rows
    hs = GLA_HEADS_PER_STEP
    dk, dv = hs * GLA_DK, hs * GLA_DV
    row = lambda b, h, t: b * nt + t
    in_specs = [
        pl.BlockSpec((t_rows, dk), lambda b, h, t: (row(b, h, t), h)),
        pl.BlockSpec((t_rows, dk), lambda b, h, t: (row(b, h, t), GLA_QK // dk + h)),
        pl.BlockSpec((t_rows, dv), lambda b, h, t: (row(b, h, t), 2 * GLA_QK // dv + h)),
        pl.BlockSpec((t_rows, dv), lambda b, h, t: (row(b, h, t), (2 * GLA_QK + GLA_VD) // dv + h)),
        pl.BlockSpec((t_rows, GLA_GLOW_PAD), lambda b, h, t: (row(b, h, t), GLA_PROJ // GLA_GLOW_PAD)),
        pl.BlockSpec((GLA_GLOW_PAD, dk), lambda b, h, t: (0, h)),
        pl.BlockSpec((1, dk), lambda b, h, t: (0, h)),
        pl.BlockSpec((1, GLA_DV), lambda b, h, t: (0, 0)),
    ]
    return pl.pallas_call(
        functools.partial(_gla_prompt_kernel, t_rows=t_rows, chunk=GLA_CHUNK, heads=hs),
        grid=(batch, GLA_HEADS // hs, nt),
        in_specs=in_specs,
        out_specs=[pl.BlockSpec((t_rows, dv), lambda b, h, t: (row(b, h, t), h)),
                   pl.BlockSpec((1, hs, GLA_DK, GLA_DV), lambda b, h, t: (b, h, 0, 0))],
        out_shape=[jax.ShapeDtypeStruct((batch * seq, GLA_VD), BF16),
                   jax.ShapeDtypeStruct((batch, GLA_HEADS, GLA_DK, GLA_DV), F32)],
        scratch_shapes=[pltpu.VMEM((hs, GLA_DV, GLA_DK), F32)],
        compiler_params=_cparams("arbitrary", "arbitrary", "arbitrary"),
        name="gla_prompt",
    )(proj, proj, proj, proj, proj, wg2p, bg, norm_w)


def _gla_sample_kernel(*refs, seq, nb, n_real, n_earlier):
    q_ref, k_ref, v_ref, r_ref, gl_ref, wg2_ref, bg_ref, nw_ref, s0_ref = refs[:9]
    earlier_refs = refs[9:9 + n_earlier]
    o_ref, sout_ref = refs[9 + n_earlier:]
    new_state_ref = sout_ref.at[n_earlier] if n_earlier else sout_ref
    g = pl.program_id(1)

    @pl.when(g < n_real)
    def _():
        k = k_ref[...]
        cum, q_dec, k_inv = _gla_gate_and_decay(q_ref[...], k, gl_ref[...].astype(BF16), wg2_ref[...],
                                                bg_ref[...], seq)
        v = v_ref[...].astype(BF16)
        causal = _causal(seq)
        outs = []
        for b in range(nb):
            lo, hi = b * seq, (b + 1) * seq
            cum_b = cum[lo:hi]
            last = cum_b[seq - 1:seq, :]
            k_last = (k[lo:hi] * jnp.exp(last - cum_b)).astype(BF16)
            qd, ki, vc = q_dec[lo:hi], k_inv[lo:hi], v[lo:hi]
            attn = jnp.where(causal, _dot_nt(qd, ki), 0.0).astype(BF16)
            s = s0_ref[b]
            outs.append(_dot(attn, vc) + _dot(qd, s.astype(BF16)))
            decay = jnp.transpose(jnp.broadcast_to(jnp.exp(last), (128, GLA_DK)))[:, 0:1]
            new_state_ref[b] = s * decay + _dot_tn(k_last, vc)
        for e, earlier_ref in enumerate(earlier_refs):
            sout_ref[e] = earlier_ref[...]
        o = jnp.concatenate(outs, axis=0)
        o_ref[...] = _gla_finish(o, r_ref[...], nw_ref[...])

    @pl.when(g >= n_real)
    def _():
        o_ref[...] = jnp.zeros_like(o_ref)


def _gla_sample(proj, row0, wg2p, bg, norm_w, state, layer, batch, seq, nb, earlier=()):
    rows = nb * seq
    n_real = batch // nb
    blk0 = row0 // rows
    rb = lambda g: blk0 + jnp.minimum(g, n_real - 1)
    sb = lambda g: jnp.minimum(g, n_real - 1)
    in_specs = [
        pl.BlockSpec((rows, GLA_DK), lambda h, g: (rb(g), h)),
        pl.BlockSpec((rows, GLA_DK), lambda h, g: (rb(g), GLA_HEADS + h)),
        pl.BlockSpec((rows, GLA_DV), lambda h, g: (rb(g), 2 * GLA_QK // GLA_DV + h)),
        pl.BlockSpec((rows, GLA_DV), lambda h, g: (rb(g), (2 * GLA_QK + GLA_VD) // GLA_DV + h)),
        pl.BlockSpec((rows, GLA_GLOW_PAD), lambda h, g: (rb(g), GLA_PROJ // GLA_GLOW_PAD)),
        pl.BlockSpec((GLA_GLOW_PAD, GLA_DK), lambda h, g: (0, h)),
        pl.BlockSpec((1, GLA_DK), lambda h, g: (0, h)),
        pl.BlockSpec((1, GLA_DV), lambda h, g: (0, 0)),
        pl.BlockSpec((None, nb, None, GLA_DK, GLA_DV), lambda h, g: (layer, sb(g), h, 0, 0)),
    ]
    state_spec = pl.BlockSpec((nb, None, GLA_DK, GLA_DV), lambda h, g: (sb(g), h, 0, 0))
    state_shape = (batch, GLA_HEADS, GLA_DK, GLA_DV)
    n_earlier = len(earlier)
    if n_earlier:
        in_specs += [state_spec] * n_earlier
        out_state_spec = pl.BlockSpec((n_earlier + 1, nb, None, GLA_DK, GLA_DV), lambda h, g: (0, sb(g), h, 0, 0))
        state_shape = (n_earlier + 1,) + state_shape
    else:
        out_state_spec = state_spec
    return pl.pallas_call(
        functools.partial(_gla_sample_kernel, seq=seq, nb=nb, n_real=n_real, n_earlier=n_earlier),
        grid=(GLA_HEADS, SAMPLE_ROWS // rows),
        in_specs=in_specs,
        out_specs=[pl.BlockSpec((rows, GLA_DV), lambda h, g: (g, h)), out_state_spec],
        out_shape=[jax.ShapeDtypeStruct((SAMPLE_ROWS, GLA_VD), F32),
                   jax.ShapeDtypeStruct(state_shape, F32)],
        compiler_params=_cparams("arbitrary", "arbitrary"),
        name="gla_sample",
    )(proj, proj, proj, proj, proj, wg2p, bg, norm_w, state, *earlier)


def _residual_ln_store(y, res_ref, g_ref, b_ref, of_ref, ob_ref):
    o = _layer_norm(ALPHA * res_ref[...] + y, g_ref[...], b_ref[...])
    of_ref[...] = o
    ob_ref[...] = o.astype(BF16)


def _zero_outputs(of_ref, ob_ref):
    of_ref[...] = jnp.zeros_like(of_ref)
    ob_ref[...] = jnp.zeros_like(ob_ref)


def _deferred_projection(i, n_prompt_tiles, first, a_prompt, a_sample, wb_ref, y_ref,
                         epilogue, sample_epilogue, zero_fill):
    def project(a):
        y_ref[...] = _dot(a(), wb_ref[...])

    @pl.when(i == 0)
    def _():
        first()
        project(a_prompt)

    @pl.when((i > 0) & (i < n_prompt_tiles))
    def _():
        epilogue()
        project(a_prompt)

    @pl.when(i == n_prompt_tiles)
    def _():
        epilogue()
        project(a_sample)

    pl.when(i == n_prompt_tiles + 1)(sample_epilogue)
    if zero_fill is not None:
        pl.when(i > n_prompt_tiles + 1)(zero_fill)


def _mixer_out_kernel(*refs, has_bias, n_prompt_tiles):
    if has_bias:
        ap_ref, as_ref, w_ref, bias_ref, resp_ref, ress_ref, g_ref, b_ref, of_ref, ob_ref, wb_ref, y_ref = refs
    else:
        ap_ref, as_ref, w_ref, resp_ref, ress_ref, g_ref, b_ref, of_ref, ob_ref, wb_ref, y_ref = refs

    def first():
        wb_ref[...] = w_ref[...].astype(BF16)

    def epilogue(res_ref):
        y = y_ref[...] + bias_ref[...] if has_bias else y_ref[...]
        _residual_ln_store(y, res_ref, g_ref, b_ref, of_ref, ob_ref)

    _deferred_projection(pl.program_id(0), n_prompt_tiles, first, lambda: ap_ref[...],
                         lambda: as_ref[...].astype(BF16), wb_ref, y_ref,
                         functools.partial(epilogue, resp_ref), functools.partial(epilogue, ress_ref),
                         lambda: _zero_outputs(of_ref, ob_ref))


def _mixer_out(a_prompt, a_sample, w, layer, bias, res_prompt, res_sample, res_sample_tile, m, g, b, name):
    k = a_prompt.shape[1]
    tm = TM_OUT
    n_prompt_tiles = a_prompt.shape[0] // tm
    assert a_sample.shape[0] == tm
    has_bias = bias is not None
    vec = pl.BlockSpec((1, D_MODEL), lambda i: (0, 0))
    lagged = pl.BlockSpec((tm, D_MODEL), lambda i: (_lag(i), 0))
    in_specs = [pl.BlockSpec((tm, k), lambda i: (jnp.minimum(i, n_prompt_tiles - 1), 0)),
                pl.BlockSpec((tm, k), lambda i: (0, 0)),
                _single_buffered((None, k, D_MODEL), lambda i: (layer, 0, 0))]
    args = [a_prompt, a_sample, w]
    if has_bias:
        in_specs.append(vec)
        args.append(bias)
    in_specs += [pl.BlockSpec((tm, D_MODEL), lambda i: (jnp.minimum(_lag(i), n_prompt_tiles - 1), 0)),
                 pl.BlockSpec((tm, D_MODEL), lambda i: (res_sample_tile, 0)), vec, vec]
    args += [res_prompt, res_sample, g, b]
    return pl.pallas_call(
        functools.partial(_mixer_out_kernel, has_bias=has_bias, n_prompt_tiles=n_prompt_tiles),
        grid=(m // tm + 1,),
        in_specs=in_specs,
        out_specs=[lagged, lagged],
        out_shape=[jax.ShapeDtypeStruct((m, D_MODEL), F32),
                   jax.ShapeDtypeStruct((m, D_MODEL), BF16)],
        scratch_shapes=[pltpu.VMEM((k, D_MODEL), BF16), pltpu.VMEM((tm, D_MODEL), F32)],
        compiler_params=_cparams("arbitrary"),
        name=name,
    )(*args)


def _ffn_out_kernel(ap_ref, as_ref, w_ref, res_ref, g_ref, b_ref, of_ref, ob_ref, y_ref, *, n_prompt_tiles):
    def epilogue():
        _residual_ln_store(y_ref[...], res_ref, g_ref, b_ref, of_ref, ob_ref)

    _deferred_projection(pl.program_id(0), n_prompt_tiles, lambda: None, lambda: ap_ref[...], lambda: as_ref[...],
                         w_ref, y_ref, epilogue, epilogue, lambda: _zero_outputs(of_ref, ob_ref))


def _ffn_out_final_kernel(ap_ref, as_ref, w_ref, res_ref, g_ref, b_ref, op_ref, os_ref, y_ref, *, n_prompt_tiles):
    def ln():
        return _layer_norm(ALPHA * res_ref[...] + y_ref[...], g_ref[...], b_ref[...])

    def epilogue():
        op_ref[...] = ln()

    def sample_epilogue():
        os_ref[...] = ln()

    _deferred_projection(pl.program_id(0), n_prompt_tiles, lambda: None, lambda: ap_ref[...], lambda: as_ref[...],
                         w_ref, y_ref, epilogue, sample_epilogue, None)


def _ffn_out(a_prompt, a_sample, w, layer, res, g, b, final):
    m = res.shape[0]
    k = a_prompt.shape[1]
    tm = TM_OUT
    n_prompt_tiles = a_prompt.shape[0] // tm
    assert a_sample.shape[0] == tm
    vec = pl.BlockSpec((1, D_MODEL), lambda i: (0, 0))
    lagged = pl.BlockSpec((tm, D_MODEL), lambda i: (_lag(i), 0))
    in_specs = [pl.BlockSpec((tm, k), lambda i: (jnp.minimum(i, n_prompt_tiles - 1), 0)),
                pl.BlockSpec((tm, k), lambda i: (0, 0)),
                _single_buffered((None, k, D_MODEL), lambda i: (layer, 0, 0)),
                lagged, vec, vec]
    scratch = [pltpu.VMEM((tm, D_MODEL), F32)]
    if final:
        return pl.pallas_call(
            functools.partial(_ffn_out_final_kernel, n_prompt_tiles=n_prompt_tiles),
            grid=(n_prompt_tiles + 2,),
            in_specs=in_specs,
            out_specs=[pl.BlockSpec((tm, D_MODEL), lambda i: (jnp.minimum(_lag(i), n_prompt_tiles - 1), 0)),
                       pl.BlockSpec((tm, D_MODEL), lambda i: (0, 0))],
            out_shape=[jax.ShapeDtypeStruct((n_prompt_tiles * tm, D_MODEL), F32),
                       jax.ShapeDtypeStruct((tm, D_MODEL), F32)],
            scratch_shapes=scratch,
            compiler_params=_cparams("arbitrary"),
            name="ffn_out_final",
        )(a_prompt, a_sample, w, res, g, b)
    return pl.pallas_call(
        functools.partial(_ffn_out_kernel, n_prompt_tiles=n_prompt_tiles),
        grid=(m // tm + 1,),
        in_specs=in_specs,
        out_specs=[lagged, lagged],
        out_shape=[jax.ShapeDtypeStruct((m, D_MODEL), F32),
                   jax.ShapeDtypeStruct((m, D_MODEL), BF16)],
        scratch_shapes=scratch,
        compiler_params=_cparams("arbitrary"),
        name="ffn_out",
    )(a_prompt, a_sample, w, res, g, b)


def _conv_glu(g, val, g1, g2, cw, cb):
    conv = cb + cw[0:1, :] * g2
    conv = conv + cw[1:2, :] * g1
    conv = conv + cw[2:3, :] * g
    return (_gelu(conv) * val).astype(BF16)


def _ffn_in_kernel(xp_ref, xs_ref, wg_ref, wv_ref, cw_ref, cb_ref, p1_ref, p2_ref, u_ref, us_ref, glp_ref, gls_ref,
                   wgb_ref, wvb_ref, carry_ref, gs_ref, vs_ref, *, seq_tiles, n_prompt_tiles, sample_seq):
    i = pl.program_id(1)
    tm, tn = u_ref.shape
    rows_s = us_ref.shape[0]
    col_slices = [slice(c0, c0 + MXU_COLS) for c0 in range(0, tn, MXU_COLS)]

    def dots(x_ref):
        rows = x_ref.shape[0]
        x = x_ref[...]
        for cs in col_slices:
            gs_ref[0:rows, cs] = _dot(x, wgb_ref[:, cs])
            vs_ref[0:rows, cs] = _dot(x, wvb_ref[:, cs])

    def prompt_epilogue():
        row = lax.broadcasted_iota(jnp.int32, (tm, MXU_COLS), 0)
        first = (i - 1) % seq_tiles == 0
        for cs in col_slices:
            g, val = gs_ref[:, cs], vs_ref[:, cs]
            above = jnp.where(first, 0.0, carry_ref[:, cs])
            g1 = jnp.where(row >= 1, pltpu.roll(g, 1, axis=0), above[7:8, :])
            g2 = jnp.where(row >= 2, pltpu.roll(g, 2, axis=0),
                           jnp.where(row == 0, above[6:7, :], above[7:8, :]))
            u_ref[:, cs] = _conv_glu(g, val, g1, g2, cw_ref[:, cs], cb_ref[:, cs])
            carry_ref[:, cs] = g[tm - 8:tm, :]
            glp_ref[:, cs] = g[tm - 8:tm, :]

    @pl.when(i == 0)
    def _():
        wgb_ref[...] = wg_ref[...].astype(BF16)
        wvb_ref[...] = wv_ref[...].astype(BF16)
        dots(xp_ref)

    @pl.when((i > 0) & (i < n_prompt_tiles))
    def _():
        prompt_epilogue()
        dots(xp_ref)

    @pl.when(i == n_prompt_tiles)
    def _():
        prompt_epilogue()
        dots(xs_ref)

    @pl.when(i == n_prompt_tiles + 1)
    def _():
        pos = lax.broadcasted_iota(jnp.int32, (rows_s, MXU_COLS), 0) % sample_seq
        for cs in col_slices:
            g, val = gs_ref[0:rows_s, cs], vs_ref[0:rows_s, cs]
            g1 = jnp.where(pos >= 1, pltpu.roll(g, 1, axis=0), p1_ref[:, cs])
            g2 = jnp.where(pos >= 2, pltpu.roll(g, 2, axis=0), p2_ref[:, cs])
            us_ref[:, cs] = _conv_glu(g, val, g1, g2, cw_ref[:, cs], cb_ref[:, cs])
            gls_ref[:, cs] = g


def _ffn_in(xb, w_in, layer, conv_w, conv_b, prev1, prev2, n_prompt_seqs, seq, sample_seq, tn):
    tm = TM_FFN_IN
    mp = n_prompt_seqs * seq
    nj = D_FF // tn
    seq_tiles = seq // tm
    n_prompt_tiles = mp // tm
    sample = pl.BlockSpec((SAMPLE_ROWS, tn), lambda j, i: (0, j))
    prompt_tile = lambda i: jnp.minimum(_lag(i), n_prompt_tiles - 1)
    return pl.pallas_call(
        functools.partial(_ffn_in_kernel, seq_tiles=seq_tiles, n_prompt_tiles=n_prompt_tiles,
                          sample_seq=sample_seq),
        grid=(nj, n_prompt_tiles + 2),
        in_specs=[pl.BlockSpec((tm, D_MODEL), lambda j, i: (jnp.minimum(i, n_prompt_tiles - 1), 0)),
                  pl.BlockSpec((SAMPLE_ROWS, D_MODEL), lambda j, i: (mp // SAMPLE_ROWS, 0)),
                  pl.BlockSpec((None, D_MODEL, tn), lambda j, i: (layer, 0, j)),
                  pl.BlockSpec((None, D_MODEL, tn), lambda j, i: (layer, 0, nj + j)),
                  pl.BlockSpec((None, CONV_W, tn), lambda j, i: (layer, 0, j)),
                  pl.BlockSpec((None, 1, tn), lambda j, i: (layer, 0, j)),
                  sample, sample],
        out_specs=[pl.BlockSpec((tm, tn), lambda j, i: (prompt_tile(i), j)),
                   sample,
                   pl.BlockSpec((8, tn), lambda j, i: (prompt_tile(i) // seq_tiles, j)),
                   sample],
        out_shape=[jax.ShapeDtypeStruct((mp, D_FF), BF16),
                   jax.ShapeDtypeStruct((SAMPLE_ROWS, D_FF), BF16),
                   jax.ShapeDtypeStruct((n_prompt_seqs * 8, D_FF), F32),
                   jax.ShapeDtypeStruct((SAMPLE_ROWS, D_FF), F32)],
        scratch_shapes=[pltpu.VMEM((D_MODEL, tn), BF16), pltpu.VMEM((D_MODEL, tn), BF16),
                        pltpu.VMEM((8, tn), F32), pltpu.VMEM((tm, tn), F32), pltpu.VMEM((tm, tn), F32)],
        compiler_params=_cparams("arbitrary", "arbitrary"),
        name="ffn_in",
    )(xb, xb, w_in, w_in, conv_w, conv_b, prev1, prev2)


def _swa_qkv_kernel(x_ref, w_ref, b_ref, ca_ref, cm_ref, cp_ref, o_ref, wb_ref, y_ref,
                    *, n_rot_cols, n_prompt_tiles, n_tiles):
    j = pl.program_id(0)
    tm, tn = o_ref.shape

    def first():
        wb_ref[...] = w_ref[...].astype(BF16)

    def dots(rows):
        x = x_ref[0:rows, :]
        for c0 in range(0, tn, MXU_COLS):
            sl = slice(c0, c0 + MXU_COLS)
            y_ref[0:rows, sl] = _dot(x, wb_ref[:, sl])

    def epilogue(rows):
        reps = MXU_COLS // ca_ref.shape[1]
        ca = jnp.concatenate([ca_ref[0:rows, :]] * reps, axis=1)
        cm = jnp.concatenate([cm_ref[0:rows, :]] * reps, axis=1)
        cp = jnp.concatenate([cp_ref[0:rows, :]] * reps, axis=1)
        lane = lax.broadcasted_iota(jnp.int32, (rows, MXU_COLS), 1) % SWA_HEAD_DIM
        half = ROT_DIM // 2
        for c0 in range(0, tn, MXU_COLS):
            sl = slice(c0, c0 + MXU_COLS)
            rotated = lane < jnp.where(j * tn + c0 < n_rot_cols, ROT_DIM, 0)
            y = y_ref[0:rows, sl] + b_ref[:, sl]
            rot = y * ca + pltpu.roll(y, MXU_COLS - half, axis=1) * cm + pltpu.roll(y, half, axis=1) * cp
            o_ref[0:rows, sl] = jnp.where(rotated, rot, y)
        if rows < tm:
            o_ref[rows:tm, :] = jnp.zeros((tm - rows, tn), F32)

    _deferred_row_tiles(pl.program_id(1), n_prompt_tiles, n_tiles, tm, first, dots, epilogue)


def _rope_tables(pos):
    half = ROT_DIM // 2
    inv = ROPE_THETA ** (-jnp.arange(half, dtype=F32) / half)
    ang = pos.astype(F32)[:, None] * inv
    cos, sin = jnp.cos(ang), jnp.sin(ang)
    n = pos.shape[0]
    rest = SWA_HEAD_DIM - ROT_DIM
    ca = jnp.concatenate([cos, cos, jnp.ones((n, rest), F32)], -1)
    cm = jnp.concatenate([-sin, jnp.zeros((n, half + rest), F32)], -1)
    cp = jnp.concatenate([jnp.zeros((n, half), F32), sin, jnp.zeros((n, rest), F32)], -1)
    return tuple(jnp.tile(t, (1, 128 // SWA_HEAD_DIM)) for t in (ca, cm, cp))


def _swa_qkv(xb, w, layer, b, tables, n_prompt_tiles, seq, tn):
    m = xb.shape[0]
    n = w.shape[2]
    pos_tiles = seq // TM
    n_tiles = m // TM
    tab = pl.BlockSpec((TM, 128), lambda j, i: (jnp.where(_lag(i) < n_prompt_tiles, _lag(i) % pos_tiles, pos_tiles), 0))
    return pl.pallas_call(
        functools.partial(_swa_qkv_kernel, n_rot_cols=SWA_QD + SWA_KD, n_prompt_tiles=n_prompt_tiles,
                          n_tiles=n_tiles),
        grid=(n // tn, n_tiles + 1),
        in_specs=[pl.BlockSpec((TM, D_MODEL), lambda j, i: (jnp.minimum(i, n_tiles - 1), 0)),
                  pl.BlockSpec((None, D_MODEL, tn), lambda j, i: (layer, 0, j)),
                  pl.BlockSpec((1, tn), lambda j, i: (0, j)),
                  tab, tab, tab],
        out_specs=pl.BlockSpec((TM, tn), lambda j, i: (_lag(i), j)),
        out_shape=jax.ShapeDtypeStruct((m, n), F32),
        scratch_shapes=[pltpu.VMEM((D_MODEL, tn), BF16), pltpu.VMEM((TM, tn), F32)],
        compiler_params=_cparams("arbitrary", "arbitrary"),
        name="swa_qkv",
    )(xb, w, b, *tables)


SWA_SCALE = SWA_HEAD_DIM ** -0.5


def _sink_softmax_pv(s, mask, sink, v):
    s = jnp.where(mask, s, -jnp.inf)
    mx = jnp.maximum(jnp.max(s, -1, keepdims=True), sink)
    p = jnp.exp(s - mx)
    denom = jnp.sum(p, -1, keepdims=True) + jnp.exp(sink - mx)
    return _dot((p / denom).astype(BF16), v)


def _swa_prompt_kernel(sink_ref, q_ref, kp_ref, kc_ref, vp_ref, vc_ref, o_ref):
    i = pl.program_id(1)
    q = q_ref[...] * SWA_SCALE
    kb = jnp.concatenate([kp_ref[...], kc_ref[...]], axis=0).astype(BF16)
    vb = jnp.concatenate([vp_ref[...], vc_ref[...]], axis=0).astype(BF16)
    rows = SWA_GROUP * WINDOW
    r = lax.broadcasted_iota(jnp.int32, (rows, 2 * WINDOW), 0) % WINDOW
    c = lax.broadcasted_iota(jnp.int32, (rows, 2 * WINDOW), 1)
    mask = (c > r) & (c <= r + WINDOW) & (c >= jnp.where(i > 0, 0, WINDOW))
    grp = lax.broadcasted_iota(jnp.int32, (rows, 1), 0) // WINDOW
    pieces = []
    for h in range(SWA_KV_HEADS):
        kh = kb[:, h * SWA_HEAD_DIM:(h + 1) * SWA_HEAD_DIM]
        vh = vb[:, h * SWA_HEAD_DIM:(h + 1) * SWA_HEAD_DIM]
        qs = jnp.concatenate(
            [q[:, (h * SWA_GROUP + g) * SWA_HEAD_DIM:(h * SWA_GROUP + g + 1) * SWA_HEAD_DIM]
             for g in range(SWA_GROUP)], axis=0).astype(BF16)
        sink = jnp.zeros((rows, 1), F32)
        for g in range(SWA_GROUP):
            sink = jnp.where(grp == g, sink_ref[h * SWA_GROUP + g], sink)
        o = _sink_softmax_pv(_dot_nt(qs, kh), mask, sink, vh)
        pieces += [o[g * WINDOW:(g + 1) * WINDOW, :] for g in range(SWA_GROUP)]
    o_ref[...] = jnp.concatenate(pieces, axis=-1).astype(o_ref.dtype)


def _swa_prompt_attn(qkv, sinks, batch, seq):
    nb = seq // WINDOW
    kcol = SWA_QD // SWA_KD
    cur = lambda b, i: b * nb + i
    prev = lambda b, i: b * nb + jnp.maximum(i - 1, 0)
    return pl.pallas_call(
        _swa_prompt_kernel,
        grid=(batch, nb),
        in_specs=[pl.BlockSpec(memory_space=pltpu.SMEM),
                  pl.BlockSpec((WINDOW, SWA_QD), lambda b, i: (cur(b, i), 0)),
                  pl.BlockSpec((WINDOW, SWA_KD), lambda b, i: (prev(b, i), kcol)),
                  pl.BlockSpec((WINDOW, SWA_KD), lambda b, i: (cur(b, i), kcol)),
                  pl.BlockSpec((WINDOW, SWA_KD), lambda b, i: (prev(b, i), kcol + 1)),
                  pl.BlockSpec((WINDOW, SWA_KD), lambda b, i: (cur(b, i), kcol + 1))],
        out_specs=pl.BlockSpec((WINDOW, SWA_QD), lambda b, i: (cur(b, i), 0)),
        out_shape=jax.ShapeDtypeStruct((batch * seq, SWA_QD), BF16),
        compiler_params=_cparams("arbitrary", "arbitrary"),
        name="swa_prompt_attn",
    )(sinks, qkv, qkv, qkv, qkv, qkv)


def _swa_sample_kernel(sink_ref, q_ref, kn_ref, vn_ref, kc_ref, vc_ref, o_ref, ko_ref, vo_ref,
                       *, seq, nb, n_real):
    step = pl.program_id(0)
    nbuf = kc_ref.shape[1]

    @pl.when(step < n_real)
    def _():
        rows = SWA_Q_HEADS * seq
        l = lax.broadcasted_iota(jnp.int32, (rows, nbuf + seq), 0) % seq
        c = lax.broadcasted_iota(jnp.int32, (rows, nbuf + seq), 1)
        diff = l + nbuf - c
        mask = (diff >= 0) & (diff < WINDOW)
        row_head = lax.broadcasted_iota(jnp.int32, (rows, SWA_KD), 0) // seq % SWA_KV_HEADS
        lane_head = lax.broadcasted_iota(jnp.int32, (rows, SWA_KD), 1) // SWA_HEAD_DIM
        own = row_head == lane_head
        sink = sink_ref[...]
        for b in range(nb):
            lo, hi = b * seq, (b + 1) * seq
            k_all = jnp.concatenate([kc_ref[b], kn_ref[lo:hi, :]], axis=0)
            v_all = jnp.concatenate([vc_ref[b], vn_ref[lo:hi, :]], axis=0)
            ko_ref[b] = k_all[seq:, :]
            vo_ref[b] = v_all[seq:, :]
            q = q_ref[lo:hi, :] * SWA_SCALE
            blocks = []
            for g in range(SWA_GROUP):
                q_g = jnp.concatenate(
                    [q[:, (h * SWA_GROUP + g) * SWA_HEAD_DIM:(h * SWA_GROUP + g + 1) * SWA_HEAD_DIM]
                     for h in range(SWA_KV_HEADS)], axis=1)
                blocks += [q_g] * SWA_KV_HEADS
            q_own = jnp.where(own, jnp.concatenate(blocks, axis=0), 0.0).astype(BF16)
            o = _sink_softmax_pv(_dot_nt(q_own, k_all.astype(BF16)), mask, sink, v_all.astype(BF16))
            pieces = []
            for h in range(SWA_KV_HEADS):
                for g in range(SWA_GROUP):
                    r0 = (g * SWA_KV_HEADS + h) * seq
                    pieces.append(o[r0:r0 + seq, h * SWA_HEAD_DIM:(h + 1) * SWA_HEAD_DIM])
            o_ref[lo:hi, :] = jnp.concatenate(pieces, axis=-1)

    @pl.when(step >= n_real)
    def _():
        o_ref[...] = jnp.zeros_like(o_ref)


def _swa_sample_attn(qkv, row0, sinks, k_cache, v_cache, batch, seq, nb):
    nbuf = k_cache.shape[1]
    sink_rows = jnp.repeat(sinks.T.reshape(-1), seq)[:, None]
    kcol = SWA_QD // SWA_KD
    rows = nb * seq
    n_real = batch // nb
    blk0 = row0 // rows
    rb = lambda s: blk0 + jnp.minimum(s, n_real - 1)
    cache = pl.BlockSpec((nb, nbuf, SWA_KD), lambda s: (jnp.minimum(s, n_real - 1), 0, 0))
    return pl.pallas_call(
        functools.partial(_swa_sample_kernel, seq=seq, nb=nb, n_real=n_real),
        grid=(SAMPLE_ROWS // rows,),
        in_specs=[pl.BlockSpec(sink_rows.shape, lambda s: (0, 0)),
                  pl.BlockSpec((rows, SWA_QD), lambda s: (rb(s), 0)),
                  pl.BlockSpec((rows, SWA_KD), lambda s: (rb(s), kcol)),
                  pl.BlockSpec((rows, SWA_KD), lambda s: (rb(s), kcol + 1)),
                  cache, cache],
        out_specs=[pl.BlockSpec((rows, SWA_QD), lambda s: (s, 0)), cache, cache],
        out_shape=[jax.ShapeDtypeStruct((SAMPLE_ROWS, SWA_QD), F32),
                   jax.ShapeDtypeStruct((batch, nbuf, SWA_KD), F32),
                   jax.ShapeDtypeStruct((batch, nbuf, SWA_KD), F32)],
        compiler_params=_cparams("arbitrary"),
        name="swa_sample_attn",
    )(sink_rows, qkv, qkv, qkv, k_cache, v_cache)


def _sg_in_kernel(x_ref, w_ref, b_ref, g_ref, beta_ref, o_ref, wb_ref, *, n_prompt_tiles):
    j = pl.program_id(0)
    tm = o_ref.shape[0]

    @pl.when(pl.program_id(1) == 0)
    def _():
        wb_ref[...] = w_ref[...].astype(BF16)

    def body(rows, normalise):
        z = _gelu(_dot(x_ref[0:rows, :], wb_ref[...]) + b_ref[...])
        o_ref[0:rows, :] = _layer_norm(z, g_ref[...], beta_ref[...]) if normalise else z
        if rows < tm:
            o_ref[rows:tm, :] = jnp.zeros((tm - rows, o_ref.shape[1]), F32)

    for half, normalise in ((0, False), (1, True)):
        pl.when(j == half)(functools.partial(
            _per_row_tile, pl.program_id(1), n_prompt_tiles, tm, functools.partial(body, normalise=normalise)))


def _sg_in(xb, w, layer, b, ln_g, ln_b, n_prompt_tiles):
    m = xb.shape[0]
    tm = TM_SG_IN
    vec = pl.BlockSpec((1, SG_WIDTH), lambda j, i: (0, 0))
    return pl.pallas_call(
        functools.partial(_sg_in_kernel, n_prompt_tiles=n_prompt_tiles),
        grid=(2, m // tm),
        in_specs=[pl.BlockSpec((tm, D_MODEL), lambda j, i: (i, 0)),
                  _single_buffered((None, D_MODEL, SG_WIDTH), lambda j, i: (layer, 0, j)),
                  pl.BlockSpec((1, SG_WIDTH), lambda j, i: (0, j)),
                  vec, vec],
        out_specs=pl.BlockSpec((None, tm, SG_WIDTH), lambda j, i: (j, i, 0)),
        out_shape=jax.ShapeDtypeStruct((2, m, SG_WIDTH), F32),
        scratch_shapes=[pltpu.VMEM((D_MODEL, SG_WIDTH), BF16)],
        compiler_params=_cparams("arbitrary", "arbitrary"),
        name="sg_in",
    )(xb, w, b, ln_g, ln_b)


def _sg_mix(u_ref, v_ref, ws_ref, bs_ref, chunk, period):
    tm = u_ref.shape[0]
    r = lax.broadcasted_iota(jnp.int32, (chunk, chunk), 0)
    c = lax.broadcasted_iota(jnp.int32, (chunk, chunk), 1)
    mask = (c <= r) & (r // period == c // period)
    ws = [jnp.where(mask, ws_ref[g], 0.0).astype(BF16) for g in range(SG_GROUPS)]
    rows = []
    for n in range(tm // chunk):
        lo, hi = n * chunk, (n + 1) * chunk
        cols = []
        for g in range(SG_GROUPS):
            vg = v_ref[lo:hi, g * SG_GW:(g + 1) * SG_GW].astype(BF16)
            mixed = _dot(ws[g], vg) + bs_ref[:, g:g + 1]
            cols.append((u_ref[lo:hi, g * SG_GW:(g + 1) * SG_GW] * mixed).astype(BF16))
        rows.append(jnp.concatenate(cols, axis=-1))
    return jnp.concatenate(rows, axis=0)


def _sg_out_kernel(u_ref, v_ref, wsp_ref, bsp_ref, wss_ref, bss_ref, w_ref, bias_ref, res_ref, g_ref, b_ref,
                   of_ref, ob_ref, wb_ref, y_ref, *, n_prompt_tiles, sample_seq):
    def first():
        wb_ref[...] = w_ref[...].astype(BF16)

    def epilogue():
        _residual_ln_store(y_ref[...] + bias_ref[...], res_ref, g_ref, b_ref, of_ref, ob_ref)

    _deferred_projection(pl.program_id(0), n_prompt_tiles, first,
                         lambda: _sg_mix(u_ref, v_ref, wsp_ref, bsp_ref, SG_CHUNK, SG_CHUNK),
                         lambda: _sg_mix(u_ref, v_ref, wss_ref, bss_ref, wss_ref.shape[1], sample_seq),
                         wb_ref, y_ref, epilogue, epilogue, lambda: _zero_outputs(of_ref, ob_ref))


def _sg_out(uv, ws_p, bs_p, ws_s, bs_s, w_out, layer, b_out, res, g, b, n_prompt_tiles, sample_seq):
    m = res.shape[0]
    tm = TM_OUT
    vec = pl.BlockSpec((1, D_MODEL), lambda i: (0, 0))
    full = lambda a: pl.BlockSpec(a.shape, lambda i: (0,) * a.ndim)
    lagged = pl.BlockSpec((tm, D_MODEL), lambda i: (_lag(i), 0))
    return pl.pallas_call(
        functools.partial(_sg_out_kernel, n_prompt_tiles=n_prompt_tiles, sample_seq=sample_seq),
        grid=(m // tm + 1,),
        in_specs=[pl.BlockSpec((None, tm, SG_WIDTH), lambda i: (0, jnp.minimum(i, n_prompt_tiles), 0)),
                  pl.BlockSpec((None, tm, SG_WIDTH), lambda i: (1, jnp.minimum(i, n_prompt_tiles), 0)),
                  full(ws_p), full(bs_p), full(ws_s), full(bs_s),
                  _single_buffered((None, SG_WIDTH, D_MODEL), lambda i: (layer, 0, 0)),
                  vec, lagged, vec, vec],
        out_specs=[lagged, lagged],
        out_shape=[jax.ShapeDtypeStruct((m, D_MODEL), F32),
                   jax.ShapeDtypeStruct((m, D_MODEL), BF16)],
        scratch_shapes=[pltpu.VMEM((SG_WIDTH, D_MODEL), BF16), pltpu.VMEM((tm, D_MODEL), F32)],
        compiler_params=_cparams("arbitrary"),
        name="sg_out",
    )(uv, uv, ws_p, bs_p, ws_s, bs_s, w_out, b_out, res, g, b)


def kernel(x_prompt, x_sample, state_gla, cache_swa_k, cache_swa_v, state_ffn_conv, ln_mix_g, ln_mix_b, ln_ffn_g, ln_ffn_b, gla_w_in, gla_w_g2, gla_b_g, gla_norm_w, gla_w_out, swa_w_qkv, swa_b_qkv, swa_sinks, swa_w_out, swa_b_out, sg_w_in, sg_b_in, sg_ln_g, sg_ln_b, sg_w_s, sg_b_s, sg_w_out, sg_b_out, ffn_w_in, ffn_conv_w, ffn_conv_b, ffn_w_out):
    bp, lp, _ = x_prompt.shape
    bs, ls, _ = x_sample.shape
    mp, ms = bp * lp, bs * ls
    assert lp % TM == 0 and ms <= SAMPLE_ROWS and SAMPLE_ROWS % ls == 0 and ls >= CONV_W - 1
    assert SAMPLE_ROWS == TM_OUT and SAMPLE_PAD % TM_SG_IN == 0

    m_rows = mp + SAMPLE_PAD
    xp2, xs2 = x_prompt.reshape(mp, D_MODEL), x_sample.reshape(ms, D_MODEL)
    xb = jnp.concatenate([xp2.astype(BF16), xs2.astype(BF16), jnp.zeros((SAMPLE_PAD - ms, D_MODEL), BF16)], axis=0)
    residual = (xp2, jnp.pad(xs2, ((0, TM_OUT - ms), (0, 0))), 0)
    gla_w_proj = gla_w_in.astype(BF16)
    gla_wg2p = jnp.pad(gla_w_g2.astype(BF16), ((0, 0), (0, GLA_GLOW_PAD - GLA_RANK), (0, 0)))
    ffn_w_out_b = ffn_w_out.astype(BF16)
    conv_b3 = ffn_conv_b.reshape(DEPTH, 1, D_FF)

    gla_p, gla_s, swk_p, swv_p, swk_s, swv_s, sgv_s, conv_p, conv_s = ([] for _ in range(9))
    for i in range(DEPTH):
        j = i // N_MIXERS
        kind = i % N_MIXERS
        ln_g, ln_b = _row(ln_mix_g[i]), _row(ln_mix_b[i])
        if kind == 0:
            proj = _gla_proj(xb, gla_w_proj, j, mp // TM)
            bg, nw = _row(gla_b_g[j]), _row(gla_norm_w[j])
            og_p, st_p = _gla_prompt(proj, gla_wg2p[j], bg, nw, bp, lp, 256)
            last_gla = j == gla_w_in.shape[0] - 1
            og_s, st_s = _gla_sample(proj, mp, gla_wg2p[j], bg, nw, state_gla, j, bs, ls, 8 if bs % 8 == 0 else 1,
                                     earlier=tuple(gla_s) if last_gla else ())
            gla_p.append(st_p)
            gla_s.append(st_s)
            xf, xb = _mixer_out(og_p, og_s, gla_w_out, j, None, *residual, m_rows, ln_g, ln_b, "gla_out")
        elif kind == 1:
            pos = jnp.concatenate([jnp.arange(lp), PAST_LEN + jnp.arange(TM) % ls])
            qkv = _swa_qkv(xb, swa_w_qkv, j, _row(swa_b_qkv[j]), _rope_tables(pos), mp // TM, lp, 1024)
            sinks = swa_sinks[j].reshape(-1)
            oa_p = _swa_prompt_attn(qkv, sinks, bp, lp)
            kv = jnp.stack([qkv[(b + 1) * lp - WINDOW:(b + 1) * lp, SWA_QD:] for b in range(bp)])
            swk_p.append(kv[..., :SWA_KD].reshape(bp, WINDOW, SWA_KV_HEADS, SWA_HEAD_DIM))
            swv_p.append(kv[..., SWA_KD:].reshape(bp, WINDOW, SWA_KV_HEADS, SWA_HEAD_DIM))
            nbuf = cache_swa_k.shape[2]
            oa_s, ko, vo = _swa_sample_attn(qkv, mp, swa_sinks[j], cache_swa_k[j].reshape(bs, nbuf, SWA_KD),
                                            cache_swa_v[j].reshape(bs, nbuf, SWA_KD), bs, ls,
                                            2 if bs % 2 == 0 else 1)
            swk_s.append(ko.reshape(bs, nbuf, SWA_KV_HEADS, SWA_HEAD_DIM))
            swv_s.append(vo.reshape(bs, nbuf, SWA_KV_HEADS, SWA_HEAD_DIM))
            xf, xb = _mixer_out(oa_p, oa_s, swa_w_out, j, _row(swa_b_out[j]), *residual, m_rows, ln_g, ln_b,
                                "swa_out")
        else:
            uv = _sg_in(xb, sg_w_in, j, _row(sg_b_in[j]), _row(sg_ln_g[j]), _row(sg_ln_b[j]), mp // TM_SG_IN)
            sgv_s.append(uv[1, mp:mp + ms].reshape(bs, ls, SG_WIDTH))
            reps = TM_OUT // ls
            ws_s = jnp.tile(sg_w_s[j][:, :ls, :ls], (1, reps, reps))
            bs_s = jnp.tile(sg_b_s[j][:, :ls].T, (reps, 1))
            xf, xb = _sg_out(uv, sg_w_s[j], sg_b_s[j].T, ws_s, bs_s, sg_w_out, j, _row(sg_b_out[j]),
                             xf, ln_g, ln_b, mp // TM_OUT, ls)
        st = state_ffn_conv[i]
        prev1 = jnp.pad(st[:, 1:2], ((0, 0), (0, ls - 1), (0, 0))).reshape(ms, D_FF)
        prev2 = jnp.pad(st, ((0, 0), (0, ls - 2), (0, 0))).reshape(ms, D_FF)
        prev1 = jnp.pad(prev1, ((0, SAMPLE_ROWS - ms), (0, 0)))
        prev2 = jnp.pad(prev2, ((0, SAMPLE_ROWS - ms), (0, 0)))
        u_p, u_s, gl_p, gl_s = _ffn_in(xb, ffn_w_in, i, ffn_conv_w, conv_b3, prev1, prev2, bp, lp, ls, 512)
        conv_p.append(gl_p.reshape(bp, 8, D_FF)[:, 8 - (CONV_W - 1):])
        conv_s.append(gl_s[:ms].reshape(bs, ls, D_FF)[:, ls - (CONV_W - 1):])
        final = i == DEPTH - 1
        outs = _ffn_out(u_p, u_s, ffn_w_out_b, i, xf, _row(ln_ffn_g[i]), _row(ln_ffn_b[i]), final)
        if final:
            y_prompt_rows, y_sample_tile = outs
        else:
            xf, xb = outs
            residual = (xf, xf, mp // TM_OUT)

    yp = y_prompt_rows.reshape(bp, lp, D_MODEL)
    ys = y_sample_tile[:ms].reshape(bs, ls, D_MODEL)
    gla_s_all = gla_s[-1] if len(gla_s) > 1 else jnp.stack(gla_s)
    return (yp, ys, jnp.stack(gla_p), gla_s_all, jnp.stack(swk_p), jnp.stack(swv_p),
            jnp.stack(swk_s), jnp.stack(swv_s), jnp.stack(sgv_s), jnp.stack(conv_p), jnp.stack(conv_s))
```

```python
import functools

import jax
import jax.numpy as jnp
from jax import lax
from jax.experimental import pallas as pl
from jax.experimental.pallas import tpu as pltpu

F32 = jnp.float32
BF16 = jnp.bfloat16

D_MODEL = 2048
DEPTH = 4
PAST_LEN = 16384
N_MIXERS = 3
ALPHA = (2 * DEPTH) ** 0.25
LN_EPS = 1e-5

GLA_HEADS = 4
GLA_DK = 256
GLA_DV = 512
GLA_RANK = 16
GLA_TAU = 16.0
GLA_CHUNK = 64
GLA_QK = GLA_HEADS * GLA_DK
GLA_VD = GLA_HEADS * GLA_DV
GLA_PROJ = 2 * GLA_QK + 2 * GLA_VD
GLA_GLOW_PAD = 128
GLA_PROJ_TN = 1280
GLA_PROJ_PAD = 5 * GLA_PROJ_TN
GLA_HEADS_PER_STEP = 4

SWA_HEAD_DIM = 64
SWA_Q_HEADS = 32
SWA_KV_HEADS = 8
SWA_GROUP = 4
WINDOW = 128
ROT_DIM = 16
ROPE_THETA = 500000.0
SWA_QD = SWA_Q_HEADS * SWA_HEAD_DIM
SWA_KD = SWA_KV_HEADS * SWA_HEAD_DIM

SG_WIDTH = 2048
SG_GROUPS = 4
SG_GW = SG_WIDTH // SG_GROUPS
SG_CHUNK = 128

D_FF = 5632
CONV_W = 3

TM = 1024
TM_FFN_IN = 1024
TM_SG_IN = 512
TM_OUT = 256
SAMPLE_PAD = TM
SAMPLE_ROWS = 256
MXU_COLS = 256
VMEM_LIMIT_BYTES = 56 * 1024 * 1024


def _cparams(*sem):
    return pltpu.CompilerParams(dimension_semantics=sem, vmem_limit_bytes=VMEM_LIMIT_BYTES)


def _dot(a, b):
    return jnp.dot(a, b, preferred_element_type=F32)


def _dot_nt(a, b):
    return lax.dot_general(a, b, (((1,), (1,)), ((), ())), preferred_element_type=F32)


def _dot_tn(a, b):
    return lax.dot_general(a, b, (((0,), (0,)), ((), ())), preferred_element_type=F32)


def _layer_norm(x, g, b):
    mu = jnp.mean(x, -1, keepdims=True)
    xc = x - mu
    var = jnp.mean(xc * xc, -1, keepdims=True)
    return xc * lax.rsqrt(var + LN_EPS) * g + b


def _gelu(x):
    return 0.5 * x * (1.0 + lax.erf(x * (0.5 ** 0.5)))


def _log_sigmoid(x):
    return jnp.minimum(x, 0.0) - jnp.log1p(jnp.exp(-jnp.abs(x)))


def _cumsum_rows(x, period):
    rowmod = lax.broadcasted_iota(jnp.int32, x.shape, 0) % period
    s = 1
    while s < period:
        x = x + jnp.where(rowmod >= s, pltpu.roll(x, s, axis=0), 0.0)
        s *= 2
    return x


def _single_buffered(shape, index_map):
    return pl.BlockSpec(shape, index_map, pipeline_mode=pl.Buffered(1))


def _row(v):
    return v.reshape(1, -1)


def _per_row_tile(i, n_prompt_tiles, tm, body):
    pl.when(i < n_prompt_tiles)(lambda: body(tm))
    pl.when(i >= n_prompt_tiles)(lambda: body(min(tm, SAMPLE_ROWS)))


def _lag(i):
    return jnp.maximum(i - 1, 0)


def _deferred_row_tiles(i, n_prompt_tiles, n_tiles, tm, first, dots, epilogue):
    s = min(tm, SAMPLE_ROWS)

    @pl.when(i == 0)
    def _():
        first()
        dots(tm)

    @pl.when((i > 0) & (i < n_prompt_tiles))
    def _():
        epilogue(tm)
        dots(tm)

    @pl.when(i == n_prompt_tiles)
    def _():
        epilogue(tm)
        dots(s)

    if n_tiles > n_prompt_tiles + 1:
        @pl.when((i > n_prompt_tiles) & (i < n_tiles))
        def _():
            epilogue(s)
            dots(s)

    pl.when(i == n_tiles)(lambda: epilogue(s))


def _gla_proj_kernel(x_ref, w_ref, o_ref, wz_ref, *, n_prompt_tiles, n_valid):
    tm, tn = o_ref.shape

    @pl.when(pl.program_id(1) == 0)
    def _():
        w = w_ref[...]
        col = lax.broadcasted_iota(jnp.int32, w.shape, 1) + pl.program_id(0) * tn
        wz_ref[...] = jnp.where(col < n_valid, w, jnp.zeros_like(w))

    def body(rows):
        o_ref[0:rows, :] = _dot(x_ref[0:rows, :], wz_ref[...])
        if rows < tm:
            o_ref[rows:tm, :] = jnp.zeros((tm - rows, tn), F32)

    _per_row_tile(pl.program_id(1), n_prompt_tiles, tm, body)


def _gla_proj(xb, w_in_b, layer, n_prompt_tiles):
    m, k = xb.shape
    tn = GLA_PROJ_TN
    return pl.pallas_call(
        functools.partial(_gla_proj_kernel, n_prompt_tiles=n_prompt_tiles, n_valid=w_in_b.shape[2]),
        grid=(GLA_PROJ_PAD // tn, m // TM),
        in_specs=[pl.BlockSpec((TM, k), lambda j, i: (i, 0)),
                  pl.BlockSpec((None, k, tn), lambda j, i: (layer, 0, j))],
        out_specs=pl.BlockSpec((TM, tn), lambda j, i: (i, j)),
        out_shape=jax.ShapeDtypeStruct((m, GLA_PROJ_PAD), F32),
        scratch_shapes=[pltpu.VMEM((k, tn), BF16)],
        compiler_params=_cparams("arbitrary", "arbitrary"),
        name="gla_proj",
    )(xb, w_in_b)


def _gla_gate_and_decay(q, k, glow_bf16, wg2, bg, period):
    ga = _dot(glow_bf16, wg2) + bg
    cum = _cumsum_rows(_log_sigmoid(ga) * (1.0 / GLA_TAU), period)
    q_dec = (q * (GLA_DK ** -0.5) * jnp.exp(cum)).astype(BF16)
    k_inv = (k * jnp.exp(-cum)).astype(BF16)
    return cum, q_dec, k_inv


def _gla_finish(o, r, nw):
    o = o * lax.rsqrt(jnp.mean(o * o, -1, keepdims=True) + LN_EPS) * nw
    return o * (r * jax.nn.sigmoid(r))


def _causal(n):
    r = lax.broadcasted_iota(jnp.int32, (n, n), 0)
    c = lax.broadcasted_iota(jnp.int32, (n, n), 1)
    return c <= r


def _gla_prompt_kernel(q_ref, k_ref, v_ref, r_ref, gl_ref, wg2_ref, bg_ref, nw_ref,
                       o_ref, sout_ref, st_ref, *, t_rows, chunk, heads):
    t = pl.program_id(2)

    @pl.when(t == 0)
    def _():
        st_ref[...] = jnp.zeros_like(st_ref)

    glow = gl_ref[...].astype(BF16)
    causal = _causal(chunk)
    per_head = []
    for h in range(heads):
        dk = slice(h * GLA_DK, (h + 1) * GLA_DK)
        dv = slice(h * GLA_DV, (h + 1) * GLA_DV)
        k = k_ref[:, dk]
        cum, q_dec, k_inv = _gla_gate_and_decay(q_ref[:, dk], k, glow, wg2_ref[:, dk], bg_ref[:, dk], chunk)
        per_head.append((dv, cum, k, q_dec, k_inv, v_ref[:, dv].astype(BF16), []))
    for c in range(t_rows // chunk):
        lo, hi = c * chunk, (c + 1) * chunk
        for h, (dv, cum, k, q_dec, k_inv, v, outs) in enumerate(per_head):
            cum_c = cum[lo:hi]
            last = cum_c[chunk - 1:chunk, :]
            k_last = (k[lo:hi] * jnp.exp(last - cum_c)).astype(BF16)
            qd, ki, vc = q_dec[lo:hi], k_inv[lo:hi], v[lo:hi]
            attn = jnp.where(causal, _dot_nt(qd, ki), 0.0).astype(BF16)
            st = st_ref[h]
            outs.append(_dot(attn, vc) + _dot_nt(qd, st.astype(BF16)))
            st_ref[h] = st * jnp.exp(last) + _dot_tn(vc, k_last)
    for h, (dv, _, _, _, _, _, outs) in enumerate(per_head):
        o = jnp.concatenate(outs, axis=0)
        o_ref[:, dv] = _gla_finish(o, r_ref[:, dv], nw_ref[...]).astype(o_ref.dtype)

    @pl.when(t == pl.num_programs(2) - 1)
    def _():
        for h in range(heads):
            sout_ref[0, h] = st_ref[h].T


def _gla_prompt(proj, wg2p, bg, norm_w, batch, seq, t_rows):
    nt = seq // t_rows
    hs = GLA_HEADS_PER_STEP
    dk, dv = hs * GLA_DK, hs * GLA_DV
    row = lambda b, h, t: b * nt + t
    in_specs = [
        pl.BlockSpec((t_rows, dk), lambda b, h, t: (row(b, h, t), h)),
        pl.BlockSpec((t_rows, dk), lambda b, h, t: (row(b, h, t), GLA_QK // dk + h)),
        pl.BlockSpec((t_rows, dv), lambda b, h, t: (row(b, h, t), 2 * GLA_QK // dv + h)),
        pl.BlockSpec((t_rows, dv), lambda b, h, t: (row(b, h, t), (2 * GLA_QK + GLA_VD) // dv + h)),
        pl.BlockSpec((t_rows, GLA_GLOW_PAD), lambda b, h, t: (row(b, h, t), GLA_PROJ // GLA_GLOW_PAD)),
        pl.BlockSpec((GLA_GLOW_PAD, dk), lambda b, h, t: (0, h)),
        pl.BlockSpec((1, dk), lambda b, h, t: (0, h)),
        pl.BlockSpec((1, GLA_DV), lambda b, h, t: (0, 0)),
    ]
    return pl.pallas_call(
        functools.partial(_gla_prompt_kernel, t_rows=t_rows, chunk=GLA_CHUNK, heads=hs),
        grid=(batch, GLA_HEADS // hs, nt),
        in_specs=in_specs,
        out_specs=[pl.BlockSpec((t_rows, dv), lambda b, h, t: (row(b, h, t), h)),
                   pl.BlockSpec((1, hs, GLA_DK, GLA_DV), lambda b, h, t: (b, h, 0, 0))],
        out_shape=[jax.ShapeDtypeStruct((batch * seq, GLA_VD), BF16),
                   jax.ShapeDtypeStruct((batch, GLA_HEADS, GLA_DK, GLA_DV), F32)],
        scratch_shapes=[pltpu.VMEM((hs, GLA_DV, GLA_DK), F32)],
        compiler_params=_cparams("arbitrary", "arbitrary", "arbitrary"),
        name="gla_prompt",
    )(proj, proj, proj, proj, proj, wg2p, bg, norm_w)


def _gla_sample_kernel(*refs, seq, nb, n_real, n_earlier):
    q_ref, k_ref, v_ref, r_ref, gl_ref, wg2_ref, bg_ref, nw_ref, s0_ref = refs[:9]
    earlier_refs = refs[9:9 + n_earlier]
    o_ref, sout_ref = refs[9 + n_earlier:]
    new_state_ref = sout_ref.at[n_earlier] if n_earlier else sout_ref
    g = pl.program_id(1)

    @pl.when(g < n_real)
    def _():
        k = k_ref[...]
        cum, q_dec, k_inv = _gla_gate_and_decay(q_ref[...], k, gl_ref[...].astype(BF16), wg2_ref[...],
                                                bg_ref[...], seq)
        v = v_ref[...].astype(BF16)
        causal = _causal(seq)
        outs = []
        for b in range(nb):
            lo, hi = b * seq, (b + 1) * seq
            cum_b = cum[lo:hi]
            last = cum_b[seq - 1:seq, :]
            k_last = (k[lo:hi] * jnp.exp(last - cum_b)).astype(BF16)
            qd, ki, vc = q_dec[lo:hi], k_inv[lo:hi], v[lo:hi]
            attn = jnp.where(causal, _dot_nt(qd, ki), 0.0).astype(BF16)
            s = s0_ref[b]
            outs.append(_dot(attn, vc) + _dot(qd, s.astype(BF16)))
            decay = jnp.transpose(jnp.broadcast_to(jnp.exp(last), (128, GLA_DK)))[:, 0:1]
            new_state_ref[b] = s * decay + _dot_tn(k_last, vc)
        for e, earlier_ref in enumerate(earlier_refs):
            sout_ref[e] = earlier_ref[...]
        o = jnp.concatenate(outs, axis=0)
        o_ref[...] = _gla_finish(o, r_ref[...], nw_ref[...])

    @pl.when(g >= n_real)
    def _():
        o_ref[...] = jnp.zeros_like(o_ref)


def _gla_sample(proj, row0, wg2p, bg, norm_w, state, layer, batch, seq, nb, earlier=()):
    rows = nb * seq
    n_real = batch // nb
    blk0 = row0 // rows
    rb = lambda g: blk0 + jnp.minimum(g, n_real - 1)
    sb = lambda g: jnp.minimum(g, n_real - 1)
    in_specs = [
        pl.BlockSpec((rows, GLA_DK), lambda h, g: (rb(g), h)),
        pl.BlockSpec((rows, GLA_DK), lambda h, g: (rb(g), GLA_HEADS + h)),
        pl.BlockSpec((rows, GLA_DV), lambda h, g: (rb(g), 2 * GLA_QK // GLA_DV + h)),
        pl.BlockSpec((rows, GLA_DV), lambda h, g: (rb(g), (2 * GLA_QK + GLA_VD) // GLA_DV + h)),
        pl.BlockSpec((rows, GLA_GLOW_PAD), lambda h, g: (rb(g), GLA_PROJ // GLA_GLOW_PAD)),
        pl.BlockSpec((GLA_GLOW_PAD, GLA_DK), lambda h, g: (0, h)),
        pl.BlockSpec((1, GLA_DK), lambda h, g: (0, h)),
        pl.BlockSpec((1, GLA_DV), lambda h, g: (0, 0)),
        pl.BlockSpec((None, nb, None, GLA_DK, GLA_DV), lambda h, g: (layer, sb(g), h, 0, 0)),
    ]
    state_spec = pl.BlockSpec((nb, None, GLA_DK, GLA_DV), lambda h, g: (sb(g), h, 0, 0))
    state_shape = (batch, GLA_HEADS, GLA_DK, GLA_DV)
    n_earlier = len(earlier)
    if n_earlier:
        in_specs += [state_spec] * n_earlier
        out_state_spec = pl.BlockSpec((n_earlier + 1, nb, None, GLA_DK, GLA_DV), lambda h, g: (0, sb(g), h, 0, 0))
        state_shape = (n_earlier + 1,) + state_shape
    else:
        out_state_spec = state_spec
    return pl.pallas_call(
        functools.partial(_gla_sample_kernel, seq=seq, nb=nb, n_real=n_real, n_earlier=n_earlier),
        grid=(GLA_HEADS, SAMPLE_ROWS // rows),
        in_specs=in_specs,
        out_specs=[pl.BlockSpec((rows, GLA_DV), lambda h, g: (g, h)), out_state_spec],
        out_shape=[jax.ShapeDtypeStruct((SAMPLE_ROWS, GLA_VD), F32),
                   jax.ShapeDtypeStruct(state_shape, F32)],
        compiler_params=_cparams("arbitrary", "arbitrary"),
        name="gla_sample",
    )(proj, proj, proj, proj, proj, wg2p, bg, norm_w, state, *earlier)


def _residual_ln_store(y, res_ref, g_ref, b_ref, of_ref, ob_ref):
    o = _layer_norm(ALPHA * res_ref[...] + y, g_ref[...], b_ref[...])
    of_ref[...] = o
    ob_ref[...] = o.astype(BF16)


def _zero_outputs(of_ref, ob_ref):
    of_ref[...] = jnp.zeros_like(of_ref)
    ob_ref[...] = jnp.zeros_like(ob_ref)


def _deferred_projection(i, n_prompt_tiles, first, a_prompt, a_sample, wb_ref, y_ref,
                         epilogue, sample_epilogue, zero_fill):
    def project(a):
        y_ref[...] = _dot(a(), wb_ref[...])

    @pl.when(i == 0)
    def _():
        first()
        project(a_prompt)

    @pl.when((i > 0) & (i < n_prompt_tiles))
    def _():
        epilogue()
        project(a_prompt)

    @pl.when(i == n_prompt_tiles)
    def _():
        epilogue()
        project(a_sample)

    pl.when(i == n_prompt_tiles + 1)(sample_epilogue)
    if zero_fill is not None:
        pl.when(i > n_prompt_tiles + 1)(zero_fill)


def _mixer_out_kernel(*refs, has_bias, n_prompt_tiles):
    if has_bias:
        ap_ref, as_ref, w_ref, bias_ref, resp_ref, ress_ref, g_ref, b_ref, of_ref, ob_ref, wb_ref, y_ref = refs
    else:
        ap_ref, as_ref, w_ref, resp_ref, ress_ref, g_ref, b_ref, of_ref, ob_ref, wb_ref, y_ref = refs

    def first():
        wb_ref[...] = w_ref[...].astype(BF16)

    def epilogue(res_ref):
        y = y_ref[...] + bias_ref[...] if has_bias else y_ref[...]
        _residual_ln_store(y, res_ref, g_ref, b_ref, of_ref, ob_ref)

    _deferred_projection(pl.program_id(0), n_prompt_tiles, first, lambda: ap_ref[...],
                         lambda: as_ref[...].astype(BF16), wb_ref, y_ref,
                         functools.partial(epilogue, resp_ref), functools.partial(epilogue, ress_ref),
                         lambda: _zero_outputs(of_ref, ob_ref))


def _mixer_out(a_prompt, a_sample, w, layer, bias, res_prompt, res_sample, res_sample_tile, m, g, b, name):
    k = a_prompt.shape[1]
    tm = TM_OUT
    n_prompt_tiles = a_prompt.shape[0] // tm
    assert a_sample.shape[0] == tm
    has_bias = bias is not None
    vec = pl.BlockSpec((1, D_MODEL), lambda i: (0, 0))
    lagged = pl.BlockSpec((tm, D_MODEL), lambda i: (_lag(i), 0))
    in_specs = [pl.BlockSpec((tm, k), lambda i: (jnp.minimum(i, n_prompt_tiles - 1), 0)),
                pl.BlockSpec((tm, k), lambda i: (0, 0)),
                _single_buffered((None, k, D_MODEL), lambda i: (layer, 0, 0))]
    args = [a_prompt, a_sample, w]
    if has_bias:
        in_specs.append(vec)
        args.append(bias)
    in_specs += [pl.BlockSpec((tm, D_MODEL), lambda i: (jnp.minimum(_lag(i), n_prompt_tiles - 1), 0)),
                 pl.BlockSpec((tm, D_MODEL), lambda i: (res_sample_tile, 0)), vec, vec]
    args += [res_prompt, res_sample, g, b]
    return pl.pallas_call(
        functools.partial(_mixer_out_kernel, has_bias=has_bias, n_prompt_tiles=n_prompt_tiles),
        grid=(m // tm + 1,),
        in_specs=in_specs,
        out_specs=[lagged, lagged],
        out_shape=[jax.ShapeDtypeStruct((m, D_MODEL), F32),
                   jax.ShapeDtypeStruct((m, D_MODEL), BF16)],
        scratch_shapes=[pltpu.VMEM((k, D_MODEL), BF16), pltpu.VMEM((tm, D_MODEL), F32)],
        compiler_params=_cparams("arbitrary"),
        name=name,
    )(*args)


FFN_W_CHUNK = 256


def _load_cast_weight(w_hbm, layer, wb_ref, stage_ref, sem):
    n_chunks = wb_ref.shape[0] // FFN_W_CHUNK

    def chunk_copy(c, slot):
        return pltpu.make_async_copy(w_hbm.at[layer, pl.ds(c * FFN_W_CHUNK, FFN_W_CHUNK), :],
                                     stage_ref.at[slot], sem.at[slot])

    chunk_copy(0, 0).start()
    for c in range(n_chunks):
        slot = c % 2
        if c + 1 < n_chunks:
            chunk_copy(c + 1, 1 - slot).start()
        chunk_copy(c, slot).wait()
        wb_ref[c * FFN_W_CHUNK:(c + 1) * FFN_W_CHUNK, :] = stage_ref[slot].astype(BF16)


def _ffn_out_kernel(ap_ref, as_ref, w_hbm, res_ref, g_ref, b_ref, of_ref, ob_ref, y_ref, wb_ref, stage_ref, sem,
                    *, n_prompt_tiles, layer):
    def epilogue():
        _residual_ln_store(y_ref[...], res_ref, g_ref, b_ref, of_ref, ob_ref)

    _deferred_projection(pl.program_id(0), n_prompt_tiles,
                         functools.partial(_load_cast_weight, w_hbm, layer, wb_ref, stage_ref, sem),
                         lambda: ap_ref[...], lambda: as_ref[...],
                         wb_ref, y_ref, epilogue, epilogue, lambda: _zero_outputs(of_ref, ob_ref))


def _ffn_out_final_kernel(ap_ref, as_ref, w_hbm, res_ref, g_ref, b_ref, op_ref, os_ref, y_ref, wb_ref, stage_ref, sem,
                          *, n_prompt_tiles, layer):
    def ln():
        return _layer_norm(ALPHA * res_ref[...] + y_ref[...], g_ref[...], b_ref[...])

    def epilogue():
        op_ref[...] = ln()

    def sample_epilogue():
        os_ref[...] = ln()

    _deferred_projection(pl.program_id(0), n_prompt_tiles,
                         functools.partial(_load_cast_weight, w_hbm, layer, wb_ref, stage_ref, sem),
                         lambda: ap_ref[...], lambda: as_ref[...],
                         wb_ref, y_ref, epilogue, sample_epilogue, None)


def _ffn_out(a_prompt, a_sample, w, layer, res, g, b, final):
    m = res.shape[0]
    k = a_prompt.shape[1]
    tm = TM_OUT
    n_prompt_tiles = a_prompt.shape[0] // tm
    assert a_sample.shape[0] == tm and k % FFN_W_CHUNK == 0
    vec = pl.BlockSpec((1, D_MODEL), lambda i: (0, 0))
    lagged = pl.BlockSpec((tm, D_MODEL), lambda i: (_lag(i), 0))
    in_specs = [pl.BlockSpec((tm, k), lambda i: (jnp.minimum(i, n_prompt_tiles - 1), 0)),
                _single_buffered((tm, k), lambda i: (0, 0)),
                pl.BlockSpec(memory_space=pl.ANY),
                lagged, vec, vec]
    scratch = [pltpu.VMEM((tm, D_MODEL), F32), pltpu.VMEM((k, D_MODEL), BF16),
               pltpu.VMEM((2, FFN_W_CHUNK, D_MODEL), F32), pltpu.SemaphoreType.DMA((2,))]
    if final:
        return pl.pallas_call(
            functools.partial(_ffn_out_final_kernel, n_prompt_tiles=n_prompt_tiles, layer=layer),
            grid=(n_prompt_tiles + 2,),
            in_specs=in_specs,
            out_specs=[pl.BlockSpec((tm, D_MODEL), lambda i: (jnp.minimum(_lag(i), n_prompt_tiles - 1), 0)),
                       pl.BlockSpec((tm, D_MODEL), lambda i: (0, 0))],
            out_shape=[jax.ShapeDtypeStruct((n_prompt_tiles * tm, D_MODEL), F32),
                       jax.ShapeDtypeStruct((tm, D_MODEL), F32)],
            scratch_shapes=scratch,
            compiler_params=_cparams("arbitrary"),
            name="ffn_out_final",
        )(a_prompt, a_sample, w, res, g, b)
    return pl.pallas_call(
        functools.partial(_ffn_out_kernel, n_prompt_tiles=n_prompt_tiles, layer=layer),
        grid=(m // tm + 1,),
        in_specs=in_specs,
        out_specs=[lagged, lagged],
        out_shape=[jax.ShapeDtypeStruct((m, D_MODEL), F32),
                   jax.ShapeDtypeStruct((m, D_MODEL), BF16)],
        scratch_shapes=scratch,
        compiler_params=_cparams("arbitrary"),
        name="ffn_out",
    )(a_prompt, a_sample, w, res, g, b)


def _conv_glu(g, val, g1, g2, cw, cb):
    conv = cb + cw[0:1, :] * g2
    conv = conv + cw[1:2, :] * g1
    conv = conv + cw[2:3, :] * g
    return (_gelu(conv) * val).astype(BF16)


def _ffn_in_kernel(xp_ref, xs_ref, wg_ref, wv_ref, cw_ref, cb_ref, p1_ref, p2_ref, u_ref, us_ref, glp_ref, gls_ref,
                   wgb_ref, wvb_ref, carry_ref, gs_ref, vs_ref, *, seq_tiles, n_prompt_tiles, sample_seq):
    i = pl.program_id(1)
    tm, tn = u_ref.shape
    rows_s = us_ref.shape[0]
    col_slices = [slice(c0, c0 + MXU_COLS) for c0 in range(0, tn, MXU_COLS)]

    def dots(x_ref):
        rows = x_ref.shape[0]
        x = x_ref[...]
        for cs in col_slices:
            gs_ref[0:rows, cs] = _dot(x, wgb_ref[:, cs])
            vs_ref[0:rows, cs] = _dot(x, wvb_ref[:, cs])

    def prompt_epilogue():
        row = lax.broadcasted_iota(jnp.int32, (tm, MXU_COLS), 0)
        first = (i - 1) % seq_tiles == 0
        for cs in col_slices:
            g, val = gs_ref[:, cs], vs_ref[:, cs]
            above = jnp.where(first, 0.0, carry_ref[:, cs])
            g1 = jnp.where(row >= 1, pltpu.roll(g, 1, axis=0), above[7:8, :])
            g2 = jnp.where(row >= 2, pltpu.roll(g, 2, axis=0),
                           jnp.where(row == 0, above[6:7, :], above[7:8, :]))
            u_ref[:, cs] = _conv_glu(g, val, g1, g2, cw_ref[:, cs], cb_ref[:, cs])
            carry_ref[:, cs] = g[tm - 8:tm, :]
            glp_ref[:, cs] = g[tm - 8:tm, :]

    @pl.when(i == 0)
    def _():
        wgb_ref[...] = wg_ref[...].astype(BF16)
        wvb_ref[...] = wv_ref[...].astype(BF16)
        dots(xp_ref)

    @pl.when((i > 0) & (i < n_prompt_tiles))
    def _():
        prompt_epilogue()
        dots(xp_ref)

    @pl.when(i == n_prompt_tiles)
    def _():
        prompt_epilogue()
        dots(xs_ref)

    @pl.when(i == n_prompt_tiles + 1)
    def _():
        pos = lax.broadcasted_iota(jnp.int32, (rows_s, MXU_COLS), 0) % sample_seq
        for cs in col_slices:
            g, val = gs_ref[0:rows_s, cs], vs_ref[0:rows_s, cs]
            g1 = jnp.where(pos >= 1, pltpu.roll(g, 1, axis=0), p1_ref[:, cs])
            g2 = jnp.where(pos >= 2, pltpu.roll(g, 2, axis=0), p2_ref[:, cs])
            us_ref[:, cs] = _conv_glu(g, val, g1, g2, cw_ref[:, cs], cb_ref[:, cs])
            gls_ref[:, cs] = g


def _ffn_in(xb, w_in, layer, conv_w, conv_b, prev1, prev2, n_prompt_seqs, seq, sample_seq, tn):
    tm = TM_FFN_IN
    mp = n_prompt_seqs * seq
    nj = D_FF // tn
    seq_tiles = seq // tm
    n_prompt_tiles = mp // tm
    sample = pl.BlockSpec((SAMPLE_ROWS, tn), lambda j, i: (0, j))
    prompt_tile = lambda i: jnp.minimum(_lag(i), n_prompt_tiles - 1)
    return pl.pallas_call(
        functools.partial(_ffn_in_kernel, seq_tiles=seq_tiles, n_prompt_tiles=n_prompt_tiles,
                          sample_seq=sample_seq),
        grid=(nj, n_prompt_tiles + 2),
        in_specs=[pl.BlockSpec((tm, D_MODEL), lambda j, i: (jnp.minimum(i, n_prompt_tiles - 1), 0)),
                  pl.BlockSpec((SAMPLE_ROWS, D_MODEL), lambda j, i: (mp // SAMPLE_ROWS, 0)),
                  pl.BlockSpec((None, D_MODEL, tn), lambda j, i: (layer, 0, j)),
                  pl.BlockSpec((None, D_MODEL, tn), lambda j, i: (layer, 0, nj + j)),
                  pl.BlockSpec((None, CONV_W, tn), lambda j, i: (layer, 0, j)),
                  pl.BlockSpec((None, 1, tn), lambda j, i: (layer, 0, j)),
                  sample, sample],
        out_specs=[pl.BlockSpec((tm, tn), lambda j, i: (prompt_tile(i), j)),
                   sample,
                   pl.BlockSpec((8, tn), lambda j, i: (prompt_tile(i) // seq_tiles, j)),
                   sample],
        out_shape=[jax.ShapeDtypeStruct((mp, D_FF), BF16),
                   jax.ShapeDtypeStruct((SAMPLE_ROWS, D_FF), BF16),
                   jax.ShapeDtypeStruct((n_prompt_seqs * 8, D_FF), F32),
                   jax.ShapeDtypeStruct((SAMPLE_ROWS, D_FF), F32)],
        scratch_shapes=[pltpu.VMEM((D_MODEL, tn), BF16), pltpu.VMEM((D_MODEL, tn), BF16),
                        pltpu.VMEM((8, tn), F32), pltpu.VMEM((tm, tn), F32), pltpu.VMEM((tm, tn), F32)],
        compiler_params=_cparams("arbitrary", "arbitrary"),
        name="ffn_in",
    )(xb, xb, w_in, w_in, conv_w, conv_b, prev1, prev2)


def _swa_qkv_kernel(x_ref, w_ref, b_ref, ca_ref, cm_ref, cp_ref, o_ref, wb_ref, y_ref,
                    *, n_rot_cols, n_prompt_tiles, n_tiles):
    j = pl.program_id(0)
    tm, tn = o_ref.shape

    def first():
        wb_ref[...] = w_ref[...].astype(BF16)

    def dots(rows):
        x = x_ref[0:rows, :]
        for c0 in range(0, tn, MXU_COLS):
            sl = slice(c0, c0 + MXU_COLS)
            y_ref[0:rows, sl] = _dot(x, wb_ref[:, sl])

    def epilogue(rows):
        reps = MXU_COLS // ca_ref.shape[1]
        ca = jnp.concatenate([ca_ref[0:rows, :]] * reps, axis=1)
        cm = jnp.concatenate([cm_ref[0:rows, :]] * reps, axis=1)
        cp = jnp.concatenate([cp_ref[0:rows, :]] * reps, axis=1)
        lane = lax.broadcasted_iota(jnp.int32, (rows, MXU_COLS), 1) % SWA_HEAD_DIM
        half = ROT_DIM // 2
        for c0 in range(0, tn, MXU_COLS):
            sl = slice(c0, c0 + MXU_COLS)
            rotated = lane < jnp.where(j * tn + c0 < n_rot_cols, ROT_DIM, 0)
            y = y_ref[0:rows, sl] + b_ref[:, sl]
            rot = y * ca + pltpu.roll(y, MXU_COLS - half, axis=1) * cm + pltpu.roll(y, half, axis=1) * cp
            o_ref[0:rows, sl] = jnp.where(rotated, rot, y)
        if rows < tm:
            o_ref[rows:tm, :] = jnp.zeros((tm - rows, tn), F32)

    _deferred_row_tiles(pl.program_id(1), n_prompt_tiles, n_tiles, tm, first, dots, epilogue)


def _rope_tables(pos):
    half = ROT_DIM // 2
    inv = ROPE_THETA ** (-jnp.arange(half, dtype=F32) / half)
    ang = pos.astype(F32)[:, None] * inv
    cos, sin = jnp.cos(ang), jnp.sin(ang)
    n = pos.shape[0]
    rest = SWA_HEAD_DIM - ROT_DIM
    ca = jnp.concatenate([cos, cos, jnp.ones((n, rest), F32)], -1)
    cm = jnp.concatenate([-sin, jnp.zeros((n, half + rest), F32)], -1)
    cp = jnp.concatenate([jnp.zeros((n, half), F32), sin, jnp.zeros((n, rest), F32)], -1)
    return tuple(jnp.tile(t, (1, 128 // SWA_HEAD_DIM)) for t in (ca, cm, cp))


def _swa_qkv(xb, w, layer, b, tables, n_prompt_tiles, seq, tn):
    m = xb.shape[0]
    n = w.shape[2]
    pos_tiles = seq // TM
    n_tiles = m // TM
    tab = pl.BlockSpec((TM, 128), lambda j, i: (jnp.where(_lag(i) < n_prompt_tiles, _lag(i) % pos_tiles, pos_tiles), 0))
    return pl.pallas_call(
        functools.partial(_swa_qkv_kernel, n_rot_cols=SWA_QD + SWA_KD, n_prompt_tiles=n_prompt_tiles,
                          n_tiles=n_tiles),
        grid=(n // tn, n_tiles + 1),
        in_specs=[pl.BlockSpec((TM, D_MODEL), lambda j, i: (jnp.minimum(i, n_tiles - 1), 0)),
                  pl.BlockSpec((None, D_MODEL, tn), lambda j, i: (layer, 0, j)),
                  pl.BlockSpec((1, tn), lambda j, i: (0, j)),
                  tab, tab, tab],
        out_specs=pl.BlockSpec((TM, tn), lambda j, i: (_lag(i), j)),
        out_shape=jax.ShapeDtypeStruct((m, n), F32),
        scratch_shapes=[pltpu.VMEM((D_MODEL, tn), BF16), pltpu.VMEM((TM, tn), F32)],
        compiler_params=_cparams("arbitrary", "arbitrary"),
        name="swa_qkv",
    )(xb, w, b, *tables)


SWA_SCALE = SWA_HEAD_DIM ** -0.5


def _sink_softmax_pv(s, mask, sink, v):
    s = jnp.where(mask, s, -jnp.inf)
    mx = jnp.maximum(jnp.max(s, -1, keepdims=True), sink)
    p = jnp.exp(s - mx)
    denom = jnp.sum(p, -1, keepdims=True) + jnp.exp(sink - mx)
    return _dot((p / denom).astype(BF16), v)


def _swa_prompt_kernel(sink_ref, q_ref, kp_ref, kc_ref, vp_ref, vc_ref, o_ref):
    i = pl.program_id(1)
    q = q_ref[...] * SWA_SCALE
    kb = jnp.concatenate([kp_ref[...], kc_ref[...]], axis=0).astype(BF16)
    vb = jnp.concatenate([vp_ref[...], vc_ref[...]], axis=0).astype(BF16)
    rows = SWA_GROUP * WINDOW
    r = lax.broadcasted_iota(jnp.int32, (rows, 2 * WINDOW), 0) % WINDOW
    c = lax.broadcasted_iota(jnp.int32, (rows, 2 * WINDOW), 1)
    mask = (c > r) & (c <= r + WINDOW) & (c >= jnp.where(i > 0, 0, WINDOW))
    grp = lax.broadcasted_iota(jnp.int32, (rows, 1), 0) // WINDOW
    pieces = []
    for h in range(SWA_KV_HEADS):
        kh = kb[:, h * SWA_HEAD_DIM:(h + 1) * SWA_HEAD_DIM]
        vh = vb[:, h * SWA_HEAD_DIM:(h + 1) * SWA_HEAD_DIM]
        qs = jnp.concatenate(
            [q[:, (h * SWA_GROUP + g) * SWA_HEAD_DIM:(h * SWA_GROUP + g + 1) * SWA_HEAD_DIM]
             for g in range(SWA_GROUP)], axis=0).astype(BF16)
        sink = jnp.zeros((rows, 1), F32)
        for g in range(SWA_GROUP):
            sink = jnp.where(grp == g, sink_ref[h * SWA_GROUP + g], sink)
        o = _sink_softmax_pv(_dot_nt(qs, kh), mask, sink, vh)
        pieces += [o[g * WINDOW:(g + 1) * WINDOW, :] for g in range(SWA_GROUP)]
    o_ref[...] = jnp.concatenate(pieces, axis=-1).astype(o_ref.dtype)


def _swa_prompt_attn(qkv, sinks, batch, seq):
    nb = seq // WINDOW
    kcol = SWA_QD // SWA_KD
    cur = lambda b, i: b * nb + i
    prev = lambda b, i: b * nb + jnp.maximum(i - 1, 0)
    return pl.pallas_call(
        _swa_prompt_kernel,
        grid=(batch, nb),
        in_specs=[pl.BlockSpec(memory_space=pltpu.SMEM),
                  pl.BlockSpec((WINDOW, SWA_QD), lambda b, i: (cur(b, i), 0)),
                  pl.BlockSpec((WINDOW, SWA_KD), lambda b, i: (prev(b, i), kcol)),
                  pl.BlockSpec((WINDOW, SWA_KD), lambda b, i: (cur(b, i), kcol)),
                  pl.BlockSpec((WINDOW, SWA_KD), lambda b, i: (prev(b, i), kcol + 1)),
                  pl.BlockSpec((WINDOW, SWA_KD), lambda b, i: (cur(b, i), kcol + 1))],
        out_specs=pl.BlockSpec((WINDOW, SWA_QD), lambda b, i: (cur(b, i), 0)),
        out_shape=jax.ShapeDtypeStruct((batch * seq, SWA_QD), BF16),
        compiler_params=_cparams("arbitrary", "arbitrary"),
        name="swa_prompt_attn",
    )(sinks, qkv, qkv, qkv, qkv, qkv)


def _swa_sample_kernel(sink_ref, q_ref, kn_ref, vn_ref, kc_ref, vc_ref, o_ref, ko_ref, vo_ref,
                       *, seq, nb, n_real):
    step = pl.program_id(0)
    nbuf = kc_ref.shape[1]

    @pl.when(step < n_real)
    def _():
        rows = SWA_Q_HEADS * seq
        l = lax.broadcasted_iota(jnp.int32, (rows, nbuf + seq), 0) % seq
        c = lax.broadcasted_iota(jnp.int32, (rows, nbuf + seq), 1)
        diff = l + nbuf - c
        mask = (diff >= 0) & (diff < WINDOW)
        row_head = lax.broadcasted_iota(jnp.int32, (rows, SWA_KD), 0) // seq % SWA_KV_HEADS
        lane_head = lax.broadcasted_iota(jnp.int32, (rows, SWA_KD), 1) // SWA_HEAD_DIM
        own = row_head == lane_head
        sink = sink_ref[...]
        for b in range(nb):
            lo, hi = b * seq, (b + 1) * seq
            k_all = jnp.concatenate([kc_ref[b], kn_ref[lo:hi, :]], axis=0)
            v_all = jnp.concatenate([vc_ref[b], vn_ref[lo:hi, :]], axis=0)
            ko_ref[b] = k_all[seq:, :]
            vo_ref[b] = v_all[seq:, :]
            q = q_ref[lo:hi, :] * SWA_SCALE
            blocks = []
            for g in range(SWA_GROUP):
                q_g = jnp.concatenate(
                    [q[:, (h * SWA_GROUP + g) * SWA_HEAD_DIM:(h * SWA_GROUP + g + 1) * SWA_HEAD_DIM]
                     for h in range(SWA_KV_HEADS)], axis=1)
                blocks += [q_g] * SWA_KV_HEADS
            q_own = jnp.where(own, jnp.concatenate(blocks, axis=0), 0.0).astype(BF16)
            o = _sink_softmax_pv(_dot_nt(q_own, k_all.astype(BF16)), mask, sink, v_all.astype(BF16))
            pieces = []
            for h in range(SWA_KV_HEADS):
                for g in range(SWA_GROUP):
                    r0 = (g * SWA_KV_HEADS + h) * seq
                    pieces.append(o[r0:r0 + seq, h * SWA_HEAD_DIM:(h + 1) * SWA_HEAD_DIM])
            o_ref[lo:hi, :] = jnp.concatenate(pieces, axis=-1)

    @pl.when(step >= n_real)
    def _():
        o_ref[...] = jnp.zeros_like(o_ref)


def _swa_sample_attn(qkv, row0, sinks, k_cache, v_cache, batch, seq, nb):
    nbuf = k_cache.shape[1]
    sink_rows = jnp.repeat(sinks.T.reshape(-1), seq)[:, None]
    kcol = SWA_QD // SWA_KD
    rows = nb * seq
    n_real = batch // nb
    blk0 = row0 // rows
    rb = lambda s: blk0 + jnp.minimum(s, n_real - 1)
    cache = pl.BlockSpec((nb, nbuf, SWA_KD), lambda s: (jnp.minimum(s, n_real - 1), 0, 0))
    return pl.pallas_call(
        functools.partial(_swa_sample_kernel, seq=seq, nb=nb, n_real=n_real),
        grid=(SAMPLE_ROWS // rows,),
        in_specs=[pl.BlockSpec(sink_rows.shape, lambda s: (0, 0)),
                  pl.BlockSpec((rows, SWA_QD), lambda s: (rb(s), 0)),
                  pl.BlockSpec((rows, SWA_KD), lambda s: (rb(s), kcol)),
                  pl.BlockSpec((rows, SWA_KD), lambda s: (rb(s), kcol + 1)),
                  cache, cache],
        out_specs=[pl.BlockSpec((rows, SWA_QD), lambda s: (s, 0)), cache, cache],
        out_shape=[jax.ShapeDtypeStruct((SAMPLE_ROWS, SWA_QD), F32),
                   jax.ShapeDtypeStruct((batch, nbuf, SWA_KD), F32),
                   jax.ShapeDtypeStruct((batch, nbuf, SWA_KD), F32)],
        compiler_params=_cparams("arbitrary"),
        name="swa_sample_attn",
    )(sink_rows, qkv, qkv, qkv, k_cache, v_cache)


def _sg_in_kernel(x_ref, w_ref, b_ref, g_ref, beta_ref, u_ref, v_ref, vs_ref, *, n_prompt_tiles):
    i = pl.program_id(0)
    tm = u_ref.shape[0]

    def body(rows):
        x = x_ref[0:rows, :]
        u_ref[0:rows, :] = _gelu(_dot(x, w_ref[:, 0:SG_WIDTH]) + b_ref[:, 0:SG_WIDTH])
        v = _layer_norm(_gelu(_dot(x, w_ref[:, SG_WIDTH:2 * SG_WIDTH]) + b_ref[:, SG_WIDTH:2 * SG_WIDTH]),
                        g_ref[...], beta_ref[...])
        v_ref[0:rows, :] = v.astype(BF16)
        return v

    def zero_rows(start):
        u_ref[start:tm, :] = jnp.zeros((tm - start, SG_WIDTH), F32)
        v_ref[start:tm, :] = jnp.zeros((tm - start, SG_WIDTH), BF16)

    @pl.when(i < n_prompt_tiles)
    def _():
        body(tm)

    @pl.when(i == n_prompt_tiles)
    def _():
        vs_ref[...] = body(SAMPLE_ROWS)
        zero_rows(SAMPLE_ROWS)

    pl.when(i > n_prompt_tiles)(lambda: zero_rows(0))


def _sg_in(xb, w_b, layer, b, ln_g, ln_b, n_prompt_tiles):
    m = xb.shape[0]
    tm = TM_SG_IN
    assert SAMPLE_ROWS <= tm
    vec = pl.BlockSpec((1, SG_WIDTH), lambda i: (0, 0))
    tile = pl.BlockSpec((tm, SG_WIDTH), lambda i: (i, 0))
    return pl.pallas_call(
        functools.partial(_sg_in_kernel, n_prompt_tiles=n_prompt_tiles),
        grid=(m // tm,),
        in_specs=[pl.BlockSpec((tm, D_MODEL), lambda i: (i, 0)),
                  _single_buffered((None, D_MODEL, 2 * SG_WIDTH), lambda i: (layer, 0, 0)),
                  pl.BlockSpec((1, 2 * SG_WIDTH), lambda i: (0, 0)),
                  vec, vec],
        out_specs=[tile, tile, pl.BlockSpec((SAMPLE_ROWS, SG_WIDTH), lambda i: (0, 0))],
        out_shape=[jax.ShapeDtypeStruct((m, SG_WIDTH), F32),
                   jax.ShapeDtypeStruct((m, SG_WIDTH), BF16),
                   jax.ShapeDtypeStruct((SAMPLE_ROWS, SG_WIDTH), F32)],
        compiler_params=_cparams("arbitrary"),
        name="sg_in",
    )(xb, w_b, b, ln_g, ln_b)


def _sg_mix(u_ref, v_ref, ws_ref, bs_ref, chunk, period):
    tm = u_ref.shape[0]
    r = lax.broadcasted_iota(jnp.int32, (chunk, chunk), 0)
    c = lax.broadcasted_iota(jnp.int32, (chunk, chunk), 1)
    mask = (c <= r) & (r // period == c // period)
    ws = [jnp.where(mask, ws_ref[g], 0.0).astype(BF16) for g in range(SG_GROUPS)]
    rows = []
    for n in range(tm // chunk):
        lo, hi = n * chunk, (n + 1) * chunk
        cols = []
        for g in range(SG_GROUPS):
            vg = v_ref[lo:hi, g * SG_GW:(g + 1) * SG_GW].astype(BF16)
            mixed = _dot(ws[g], vg) + bs_ref[:, g:g + 1]
            cols.append((u_ref[lo:hi, g * SG_GW:(g + 1) * SG_GW] * mixed).astype(BF16))
        rows.append(jnp.concatenate(cols, axis=-1))
    return jnp.concatenate(rows, axis=0)


def _sg_out_kernel(u_ref, v_ref, wsp_ref, bsp_ref, wss_ref, bss_ref, w_ref, bias_ref, res_ref, g_ref, b_ref,
                   of_ref, ob_ref, wb_ref, y_ref, *, n_prompt_tiles, sample_seq):
    def first():
        wb_ref[...] = w_ref[...].astype(BF16)

    def epilogue():
        _residual_ln_store(y_ref[...] + bias_ref[...], res_ref, g_ref, b_ref, of_ref, ob_ref)

    _deferred_projection(pl.program_id(0), n_prompt_tiles, first,
                         lambda: _sg_mix(u_ref, v_ref, wsp_ref, bsp_ref, SG_CHUNK, SG_CHUNK),
                         lambda: _sg_mix(u_ref, v_ref, wss_ref, bss_ref, wss_ref.shape[1], sample_seq),
                         wb_ref, y_ref, epilogue, epilogue, lambda: _zero_outputs(of_ref, ob_ref))


def _sg_out(u, v, ws_p, bs_p, ws_s, bs_s, w_out, layer, b_out, res, g, b, n_prompt_tiles, sample_seq):
    m = res.shape[0]
    tm = TM_OUT
    vec = pl.BlockSpec((1, D_MODEL), lambda i: (0, 0))
    full = lambda a: pl.BlockSpec(a.shape, lambda i: (0,) * a.ndim)
    lagged = pl.BlockSpec((tm, D_MODEL), lambda i: (_lag(i), 0))
    mixed_tile = pl.BlockSpec((tm, SG_WIDTH), lambda i: (jnp.minimum(i, n_prompt_tiles), 0))
    return pl.pallas_call(
        functools.partial(_sg_out_kernel, n_prompt_tiles=n_prompt_tiles, sample_seq=sample_seq),
        grid=(m // tm + 1,),
        in_specs=[mixed_tile, mixed_tile,
                  full(ws_p), full(bs_p), full(ws_s), full(bs_s),
                  _single_buffered((None, SG_WIDTH, D_MODEL), lambda i: (layer, 0, 0)),
                  vec, lagged, vec, vec],
        out_specs=[lagged, lagged],
        out_shape=[jax.ShapeDtypeStruct((m, D_MODEL), F32),
                   jax.ShapeDtypeStruct((m, D_MODEL), BF16)],
        scratch_shapes=[pltpu.VMEM((SG_WIDTH, D_MODEL), BF16), pltpu.VMEM((tm, D_MODEL), F32)],
        compiler_params=_cparams("arbitrary"),
        name="sg_out",
    )(u, v, ws_p, bs_p, ws_s, bs_s, w_out, b_out, res, g, b)


def kernel(x_prompt, x_sample, state_gla, cache_swa_k, cache_swa_v, state_ffn_conv, ln_mix_g, ln_mix_b, ln_ffn_g, ln_ffn_b, gla_w_in, gla_w_g2, gla_b_g, gla_norm_w, gla_w_out, swa_w_qkv, swa_b_qkv, swa_sinks, swa_w_out, swa_b_out, sg_w_in, sg_b_in, sg_ln_g, sg_ln_b, sg_w_s, sg_b_s, sg_w_out, sg_b_out, ffn_w_in, ffn_conv_w, ffn_conv_b, ffn_w_out):
    bp, lp, _ = x_prompt.shape
    bs, ls, _ = x_sample.shape
    mp, ms = bp * lp, bs * ls
    assert lp % TM == 0 and ms <= SAMPLE_ROWS and SAMPLE_ROWS % ls == 0 and ls >= CONV_W - 1
    assert SAMPLE_ROWS == TM_OUT and SAMPLE_PAD % TM_SG_IN == 0

    m_rows = mp + SAMPLE_PAD
    xp2, xs2 = x_prompt.reshape(mp, D_MODEL), x_sample.reshape(ms, D_MODEL)
    xb = jnp.concatenate([xp2.astype(BF16), xs2.astype(BF16), jnp.zeros((SAMPLE_PAD - ms, D_MODEL), BF16)], axis=0)
    residual = (xp2, jnp.pad(xs2, ((0, TM_OUT - ms), (0, 0))), 0)
    gla_w_proj = gla_w_in.astype(BF16)
    gla_wg2p = jnp.pad(gla_w_g2.astype(BF16), ((0, 0), (0, GLA_GLOW_PAD - GLA_RANK), (0, 0)))
    sg_w_in_b = sg_w_in.astype(BF16)
    conv_b3 = ffn_conv_b.reshape(DEPTH, 1, D_FF)

    gla_p, gla_s, swk_p, swv_p, swk_s, swv_s, sgv_s, conv_p, conv_s = ([] for _ in range(9))
    for i in range(DEPTH):
        j = i // N_MIXERS
        kind = i % N_MIXERS
        ln_g, ln_b = _row(ln_mix_g[i]), _row(ln_mix_b[i])
        if kind == 0:
            proj = _gla_proj(xb, gla_w_proj, j, mp // TM)
            bg, nw = _row(gla_b_g[j]), _row(gla_norm_w[j])
            og_p, st_p = _gla_prompt(proj, gla_wg2p[j], bg, nw, bp, lp, 256)
            last_gla = j == gla_w_in.shape[0] - 1
            og_s, st_s = _gla_sample(proj, mp, gla_wg2p[j], bg, nw, state_gla, j, bs, ls, 8 if bs % 8 == 0 else 1,
                                     earlier=tuple(gla_s) if last_gla else ())
            gla_p.append(st_p)
            gla_s.append(st_s)
            xf, xb = _mixer_out(og_p, og_s, gla_w_out, j, None, *residual, m_rows, ln_g, ln_b, "gla_out")
        elif kind == 1:
            pos = jnp.concatenate([jnp.arange(lp), PAST_LEN + jnp.arange(TM) % ls])
            qkv = _swa_qkv(xb, swa_w_qkv, j, _row(swa_b_qkv[j]), _rope_tables(pos), mp // TM, lp, 1024)
            sinks = swa_sinks[j].reshape(-1)
            oa_p = _swa_prompt_attn(qkv, sinks, bp, lp)
            kv = jnp.stack([qkv[(b + 1) * lp - WINDOW:(b + 1) * lp, SWA_QD:] for b in range(bp)])
            swk_p.append(kv[..., :SWA_KD].reshape(bp, WINDOW, SWA_KV_HEADS, SWA_HEAD_DIM))
            swv_p.append(kv[..., SWA_KD:].reshape(bp, WINDOW, SWA_KV_HEADS, SWA_HEAD_DIM))
            nbuf = cache_swa_k.shape[2]
            oa_s, ko, vo = _swa_sample_attn(qkv, mp, swa_sinks[j], cache_swa_k[j].reshape(bs, nbuf, SWA_KD),
                                            cache_swa_v[j].reshape(bs, nbuf, SWA_KD), bs, ls,
                                            2 if bs % 2 == 0 else 1)
            swk_s.append(ko.reshape(bs, nbuf, SWA_KV_HEADS, SWA_HEAD_DIM))
            swv_s.append(vo.reshape(bs, nbuf, SWA_KV_HEADS, SWA_HEAD_DIM))
            xf, xb = _mixer_out(oa_p, oa_s, swa_w_out, j, _row(swa_b_out[j]), *residual, m_rows, ln_g, ln_b,
                                "swa_out")
        else:
            sg_u, sg_v, sg_v_sample = _sg_in(xb, sg_w_in_b, j, _row(sg_b_in[j]), _row(sg_ln_g[j]),
                                             _row(sg_ln_b[j]), mp // TM_SG_IN)
            sgv_s.append(sg_v_sample[:ms].reshape(bs, ls, SG_WIDTH))
            reps = TM_OUT // ls
            ws_s = jnp.tile(sg_w_s[j][:, :ls, :ls], (1, reps, reps))
            bs_s = jnp.tile(sg_b_s[j][:, :ls].T, (reps, 1))
            xf, xb = _sg_out(sg_u, sg_v, sg_w_s[j], sg_b_s[j].T, ws_s, bs_s, sg_w_out, j, _row(sg_b_out[j]),
                             xf, ln_g, ln_b, mp // TM_OUT, ls)
        st = state_ffn_conv[i]
        prev1 = jnp.pad(st[:, 1:2], ((0, 0), (0, ls - 1), (0, 0))).reshape(ms, D_FF)
        prev2 = jnp.pad(st, ((0, 0), (0, ls - 2), (0, 0))).reshape(ms, D_FF)
        prev1 = jnp.pad(prev1, ((0, SAMPLE_ROWS - ms), (0, 0)))
        prev2 = jnp.pad(prev2, ((0, SAMPLE_ROWS - ms), (0, 0)))
        u_p, u_s, gl_p, gl_s = _ffn_in(xb, ffn_w_in, i, ffn_conv_w, conv_b3, prev1, prev2, bp, lp, ls, 512)
        conv_p.append(gl_p.reshape(bp, 8, D_FF)[:, 8 - (CONV_W - 1):])
        conv_s.append(gl_s[:ms].reshape(bs, ls, D_FF)[:, ls - (CONV_W - 1):])
        final = i == DEPTH - 1
        outs = _ffn_out(u_p, u_s, ffn_w_out, i, xf, _row(ln_ffn_g[i]), _row(ln_ffn_b[i]), final)
        if final:
            y_prompt_rows, y_sample_tile = outs
        else:
            xf, xb = outs
            residual = (xf, xf, mp // TM_OUT)

    yp = y_prompt_rows.reshape(bp, lp, D_MODEL)
    ys = y_sample_tile[:ms].reshape(bs, ls, D_MODEL)
    gla_s_all = gla_s[-1] if len(gla_s) > 1 else jnp.stack(gla_s)
    return (yp, ys, jnp.stack(gla_p), gla_s_all, jnp.stack(swk_p), jnp.stack(swv_p),
            jnp.stack(swk_s), jnp.stack(swv_s), jnp.stack(sgv_s), jnp.stack(conv_p), jnp.stack(conv_s))
```

```python
import functools

import jax
import jax.numpy as jnp
from jax import lax
from jax.experimental import pallas as pl
from jax.experimental.pallas import tpu as pltpu

F32 = jnp.float32
BF16 = jnp.bfloat16

D_MODEL = 2048
DEPTH = 4
PAST_LEN = 16384
N_MIXERS = 3
ALPHA = (2 * DEPTH) ** 0.25
LN_EPS = 1e-5

GLA_HEADS = 4
GLA_DK = 256
GLA_DV = 512
GLA_RANK = 16
GLA_TAU = 16.0
GLA_CHUNK = 64
GLA_QK = GLA_HEADS * GLA_DK
GLA_VD = GLA_HEADS * GLA_DV
GLA_PROJ = 2 * GLA_QK + 2 * GLA_VD
GLA_GLOW_PAD = 128
GLA_PROJ_TN = 1280
GLA_PROJ_PAD = 5 * GLA_PROJ_TN
GLA_HEADS_PER_STEP = 4

SWA_HEAD_DIM = 64
SWA_Q_HEADS = 32
SWA_KV_HEADS = 8
SWA_GROUP = 4
WINDOW = 128
ROT_DIM = 16
ROPE_THETA = 500000.0
SWA_QD = SWA_Q_HEADS * SWA_HEAD_DIM
SWA_KD = SWA_KV_HEADS * SWA_HEAD_DIM

SG_WIDTH = 2048
SG_GROUPS = 4
SG_GW = SG_WIDTH // SG_GROUPS
SG_CHUNK = 128

D_FF = 5632
CONV_W = 3

TM = 1024
TM_FFN_IN = 1024
TM_SG_IN = 512
TM_OUT = 256
SAMPLE_PAD = TM
SAMPLE_ROWS = 256
MXU_COLS = 256
RING = 3
VMEM_LIMIT_BYTES = 56 * 1024 * 1024


def _cparams(*sem):
    return pltpu.CompilerParams(dimension_semantics=sem, vmem_limit_bytes=VMEM_LIMIT_BYTES)


def _dot(a, b):
    return jnp.dot(a, b, preferred_element_type=F32)


def _dot_nt(a, b):
    return lax.dot_general(a, b, (((1,), (1,)), ((), ())), preferred_element_type=F32)


def _dot_tn(a, b):
    return lax.dot_general(a, b, (((0,), (0,)), ((), ())), preferred_element_type=F32)


def _layer_norm(x, g, b):
    mu = jnp.mean(x, -1, keepdims=True)
    xc = x - mu
    var = jnp.mean(xc * xc, -1, keepdims=True)
    return xc * lax.rsqrt(var + LN_EPS) * g + b


def _gelu(x):
    return 0.5 * x * (1.0 + lax.erf(x * (0.5 ** 0.5)))


def _log_sigmoid(x):
    return jnp.minimum(x, 0.0) - jnp.log1p(jnp.exp(-jnp.abs(x)))


def _cumsum_rows(x, period):
    rowmod = lax.broadcasted_iota(jnp.int32, x.shape, 0) % period
    s = 1
    while s < period:
        x = x + jnp.where(rowmod >= s, pltpu.roll(x, s, axis=0), 0.0)
        s *= 2
    return x


def _single_buffered(shape, index_map):
    return pl.BlockSpec(shape, index_map, pipeline_mode=pl.Buffered(1))


def _row(v):
    return v.reshape(1, -1)


def _per_row_tile(i, n_prompt_tiles, tm, body):
    pl.when(i < n_prompt_tiles)(lambda: body(tm))
    pl.when(i >= n_prompt_tiles)(lambda: body(min(tm, SAMPLE_ROWS)))


def _lag(i):
    return jnp.maximum(i - 1, 0)


def _deferred_row_tiles(i, n_prompt_tiles, n_tiles, tm, first, dots, epilogue):
    s = min(tm, SAMPLE_ROWS)

    @pl.when(i == 0)
    def _():
        first()
        dots(tm)

    @pl.when((i > 0) & (i < n_prompt_tiles))
    def _():
        epilogue(tm)
        dots(tm)

    @pl.when(i == n_prompt_tiles)
    def _():
        epilogue(tm)
        dots(s)

    if n_tiles > n_prompt_tiles + 1:
        @pl.when((i > n_prompt_tiles) & (i < n_tiles))
        def _():
            epilogue(s)
            dots(s)

    pl.when(i == n_tiles)(lambda: epilogue(s))


def _gla_proj_kernel(x_ref, w_ref, o_ref, wz_ref, *, n_prompt_tiles, n_valid):
    tm, tn = o_ref.shape

    @pl.when(pl.program_id(1) == 0)
    def _():
        w = w_ref[...]
        col = lax.broadcasted_iota(jnp.int32, w.shape, 1) + pl.program_id(0) * tn
        wz_ref[...] = jnp.where(col < n_valid, w, jnp.zeros_like(w))

    def body(rows):
        o_ref[0:rows, :] = _dot(x_ref[0:rows, :], wz_ref[...])
        if rows < tm:
            o_ref[rows:tm, :] = jnp.zeros((tm - rows, tn), F32)

    _per_row_tile(pl.program_id(1), n_prompt_tiles, tm, body)


def _gla_proj(xb, w_in_b, layer, n_prompt_tiles):
    m, k = xb.shape
    tn = GLA_PROJ_TN
    return pl.pallas_call(
        functools.partial(_gla_proj_kernel, n_prompt_tiles=n_prompt_tiles, n_valid=w_in_b.shape[2]),
        grid=(GLA_PROJ_PAD // tn, m // TM),
        in_specs=[pl.BlockSpec((TM, k), lambda j, i: (i, 0)),
                  pl.BlockSpec((None, k, tn), lambda j, i: (layer, 0, j))],
        out_specs=pl.BlockSpec((TM, tn), lambda j, i: (i, j)),
        out_shape=jax.ShapeDtypeStruct((m, GLA_PROJ_PAD), F32),
        scratch_shapes=[pltpu.VMEM((k, tn), BF16)],
        compiler_params=_cparams("arbitrary", "arbitrary"),
        name="gla_proj",
    )(xb, w_in_b)


def _gla_gate_and_decay(q, k, glow_bf16, wg2, bg, period):
    ga = _dot(glow_bf16, wg2) + bg
    cum = _cumsum_rows(_log_sigmoid(ga) * (1.0 / GLA_TAU), period)
    q_dec = (q * (GLA_DK ** -0.5) * jnp.exp(cum)).astype(BF16)
    k_inv = (k * jnp.exp(-cum)).astype(BF16)
    return cum, q_dec, k_inv


def _gla_finish(o, r, nw):
    o = o * lax.rsqrt(jnp.mean(o * o, -1, keepdims=True) + LN_EPS) * nw
    return o * (r * jax.nn.sigmoid(r))


def _causal(n):
    r = lax.broadcasted_iota(jnp.int32, (n, n), 0)
    c = lax.broadcasted_iota(jnp.int32, (n, n), 1)
    return c <= r


def _gla_prompt_kernel(q_ref, k_ref, v_ref, r_ref, gl_ref, wg2_ref, bg_ref, nw_ref,
                       o_ref, sout_ref, st_ref, *, t_rows, chunk, heads):
    t = pl.program_id(2)

    @pl.when(t == 0)
    def _():
        st_ref[...] = jnp.zeros_like(st_ref)

    glow = gl_ref[...].astype(BF16)
    causal = _causal(chunk)
    per_head = []
    for h in range(heads):
        dk = slice(h * GLA_DK, (h + 1) * GLA_DK)
        dv = slice(h * GLA_DV, (h + 1) * GLA_DV)
        k = k_ref[:, dk]
        cum, q_dec, k_inv = _gla_gate_and_decay(q_ref[:, dk], k, glow, wg2_ref[:, dk], bg_ref[:, dk], chunk)
        per_head.append((dv, cum, k, q_dec, k_inv, v_ref[:, dv].astype(BF16), []))
    for c in range(t_rows // chunk):
        lo, hi = c * chunk, (c + 1) * chunk
        for h, (dv, cum, k, q_dec, k_inv, v, outs) in enumerate(per_head):
            cum_c = cum[lo:hi]
            last = cum_c[chunk - 1:chunk, :]
            k_last = (k[lo:hi] * jnp.exp(last - cum_c)).astype(BF16)
            qd, ki, vc = q_dec[lo:hi], k_inv[lo:hi], v[lo:hi]
            attn = jnp.where(causal, _dot_nt(qd, ki), 0.0).astype(BF16)
            st = st_ref[h]
            outs.append(_dot(attn, vc) + _dot_nt(qd, st.astype(BF16)))
            st_ref[h] = st * jnp.exp(last) + _dot_tn(vc, k_last)
    for h, (dv, _, _, _, _, _, outs) in enumerate(per_head):
        o = jnp.concatenate(outs, axis=0)
        o_ref[:, dv] = _gla_finish(o, r_ref[:, dv], nw_ref[...]).astype(o_ref.dtype)

    @pl.when(t == pl.num_programs(2) - 1)
    def _():
        for h in range(heads):
            sout_ref[0, h] = st_ref[h].T


def _gla_prompt(proj, wg2p, bg, norm_w, batch, seq, t_rows):
    nt = seq // t_rows
    hs = GLA_HEADS_PER_STEP
    dk, dv = hs * GLA_DK, hs * GLA_DV
    row = lambda b, h, t: b * nt + t
    in_specs = [
        pl.BlockSpec((t_rows, dk), lambda b, h, t: (row(b, h, t), h)),
        pl.BlockSpec((t_rows, dk), lambda b, h, t: (row(b, h, t), GLA_QK // dk + h)),
        pl.BlockSpec((t_rows, dv), lambda b, h, t: (row(b, h, t), 2 * GLA_QK // dv + h)),
        pl.BlockSpec((t_rows, dv), lambda b, h, t: (row(b, h, t), (2 * GLA_QK + GLA_VD) // dv + h)),
        pl.BlockSpec((t_rows, GLA_GLOW_PAD), lambda b, h, t: (row(b, h, t), GLA_PROJ // GLA_GLOW_PAD)),
        pl.BlockSpec((GLA_GLOW_PAD, dk), lambda b, h, t: (0, h)),
        pl.BlockSpec((1, dk), lambda b, h, t: (0, h)),
        pl.BlockSpec((1, GLA_DV), lambda b, h, t: (0, 0)),
    ]
    return pl.pallas_call(
        functools.partial(_gla_prompt_kernel, t_rows=t_rows, chunk=GLA_CHUNK, heads=hs),
        grid=(batch, GLA_HEADS // hs, nt),
        in_specs=in_specs,
        out_specs=[pl.BlockSpec((t_rows, dv), lambda b, h, t: (row(b, h, t), h)),
                   pl.BlockSpec((1, hs, GLA_DK, GLA_DV), lambda b, h, t: (b, h, 0, 0))],
        out_shape=[jax.ShapeDtypeStruct((batch * seq, GLA_VD), BF16),
                   jax.ShapeDtypeStruct((batch, GLA_HEADS, GLA_DK, GLA_DV), F32)],
        scratch_shapes=[pltpu.VMEM((hs, GLA_DV, GLA_DK), F32)],
        compiler_params=_cparams("arbitrary", "arbitrary", "arbitrary"),
        name="gla_prompt",
    )(proj, proj, proj, proj, proj, wg2p, bg, norm_w)


def _gla_sample_kernel(*refs, seq, nb, n_real, n_earlier):
    q_ref, k_ref, v_ref, r_ref, gl_ref, wg2_ref, bg_ref, nw_ref, s0_ref = refs[:9]
    earlier_refs = refs[9:9 + n_earlier]
    o_ref, sout_ref = refs[9 + n_earlier:]
    new_state_ref = sout_ref.at[n_earlier] if n_earlier else sout_ref
    g = pl.program_id(1)

    @pl.when(g < n_real)
    def _():
        k = k_ref[...]
        cum, q_dec, k_inv = _gla_gate_and_decay(q_ref[...], k, gl_ref[...].astype(BF16), wg2_ref[...],
                                                bg_ref[...], seq)
        v = v_ref[...].astype(BF16)
        causal = _causal(seq)
        outs = []
        for b in range(nb):
            lo, hi = b * seq, (b + 1) * seq
            cum_b = cum[lo:hi]
            last = cum_b[seq - 1:seq, :]
            k_last = (k[lo:hi] * jnp.exp(last - cum_b)).astype(BF16)
            qd, ki, vc = q_dec[lo:hi], k_inv[lo:hi], v[lo:hi]
            attn = jnp.where(causal, _dot_nt(qd, ki), 0.0).astype(BF16)
            s = s0_ref[b]
            outs.append(_dot(attn, vc) + _dot(qd, s.astype(BF16)))
            decay = jnp.transpose(jnp.broadcast_to(jnp.exp(last), (128, GLA_DK)))[:, 0:1]
            new_state_ref[b] = s * decay + _dot_tn(k_last, vc)
        for e, earlier_ref in enumerate(earlier_refs):
            sout_ref[e] = earlier_ref[...]
        o = jnp.concatenate(outs, axis=0)
        o_ref[...] = _gla_finish(o, r_ref[...], nw_ref[...])

    @pl.when(g >= n_real)
    def _():
        o_ref[...] = jnp.zeros_like(o_ref)


def _gla_sample(proj, row0, wg2p, bg, norm_w, state, layer, batch, seq, nb, earlier=()):
    rows = nb * seq
    n_real = batch // nb
    blk0 = row0 // rows
    rb = lambda g: blk0 + jnp.minimum(g, n_real - 1)
    sb = lambda g: jnp.minimum(g, n_real - 1)
    in_specs = [
        pl.BlockSpec((rows, GLA_DK), lambda h, g: (rb(g), h)),
        pl.BlockSpec((rows, GLA_DK), lambda h, g: (rb(g), GLA_HEADS + h)),
        pl.BlockSpec((rows, GLA_DV), lambda h, g: (rb(g), 2 * GLA_QK // GLA_DV + h)),
        pl.BlockSpec((rows, GLA_DV), lambda h, g: (rb(g), (2 * GLA_QK + GLA_VD) // GLA_DV + h)),
        pl.BlockSpec((rows, GLA_GLOW_PAD), lambda h, g: (rb(g), GLA_PROJ // GLA_GLOW_PAD)),
        pl.BlockSpec((GLA_GLOW_PAD, GLA_DK), lambda h, g: (0, h)),
        pl.BlockSpec((1, GLA_DK), lambda h, g: (0, h)),
        pl.BlockSpec((1, GLA_DV), lambda h, g: (0, 0)),
        pl.BlockSpec((None, nb, None, GLA_DK, GLA_DV), lambda h, g: (layer, sb(g), h, 0, 0)),
    ]
    state_spec = pl.BlockSpec((nb, None, GLA_DK, GLA_DV), lambda h, g: (sb(g), h, 0, 0))
    state_shape = (batch, GLA_HEADS, GLA_DK, GLA_DV)
    n_earlier = len(earlier)
    if n_earlier:
        in_specs += [state_spec] * n_earlier
        out_state_spec = pl.BlockSpec((n_earlier + 1, nb, None, GLA_DK, GLA_DV), lambda h, g: (0, sb(g), h, 0, 0))
        state_shape = (n_earlier + 1,) + state_shape
    else:
        out_state_spec = state_spec
    return pl.pallas_call(
        functools.partial(_gla_sample_kernel, seq=seq, nb=nb, n_real=n_real, n_earlier=n_earlier),
        grid=(GLA_HEADS, SAMPLE_ROWS // rows),
        in_specs=in_specs,
        out_specs=[pl.BlockSpec((rows, GLA_DV), lambda h, g: (g, h)), out_state_spec],
        out_shape=[jax.ShapeDtypeStruct((SAMPLE_ROWS, GLA_VD), F32),
                   jax.ShapeDtypeStruct(state_shape, F32)],
        compiler_params=_cparams("arbitrary", "arbitrary"),
        name="gla_sample",
    )(proj, proj, proj, proj, proj, wg2p, bg, norm_w, state, *earlier)


def _residual_ln_store(y, res_ref, g_ref, b_ref, of_ref, ob_ref):
    o = _layer_norm(ALPHA * res_ref[...] + y, g_ref[...], b_ref[...])
    of_ref[...] = o
    ob_ref[...] = o.astype(BF16)


def _zero_outputs(of_ref, ob_ref):
    of_ref[...] = jnp.zeros_like(of_ref)
    ob_ref[...] = jnp.zeros_like(ob_ref)


def _deferred_projection(i, n_prompt_tiles, first, a_prompt, a_sample, wb_ref, y_ref,
                         epilogue, sample_epilogue, zero_fill):
    def project(a):
        y_ref[...] = _dot(a(), wb_ref[...])

    @pl.when(i == 0)
    def _():
        first()
        project(a_prompt)

    @pl.when((i > 0) & (i < n_prompt_tiles))
    def _():
        epilogue()
        project(a_prompt)

    @pl.when(i == n_prompt_tiles)
    def _():
        epilogue()
        project(a_sample)

    pl.when(i == n_prompt_tiles + 1)(sample_epilogue)
    if zero_fill is not None:
        pl.when(i > n_prompt_tiles + 1)(zero_fill)


def _mixer_out_kernel(*refs, has_bias, n_prompt_tiles):
    if has_bias:
        ap_hbm, as_ref, w_ref, bias_ref, resp_hbm, ress_ref, g_ref, b_ref, of_ref, ob_ref = refs[:10]
    else:
        ap_hbm, as_ref, w_ref, resp_hbm, ress_ref, g_ref, b_ref, of_ref, ob_ref = refs[:9]
    wb_ref, y_ref, a_buf, r_buf, a_sem, r_sem = refs[-6:]
    i = pl.program_id(0)
    tm = y_ref.shape[0]

    def a_copy(t):
        return pltpu.make_async_copy(ap_hbm.at[pl.ds(pl.multiple_of(t * tm, tm), tm), :],
                                     a_buf.at[t % RING], a_sem.at[t % RING])

    def r_copy(t):
        return pltpu.make_async_copy(resp_hbm.at[pl.ds(pl.multiple_of(t * tm, tm), tm), :],
                                     r_buf.at[t % RING], r_sem.at[t % RING])

    @pl.when(i == 0)
    def _():
        a_copy(0).start()
        a_copy(1).start()
        r_copy(0).start()

    @pl.when(i + 2 < n_prompt_tiles)
    def _():
        a_copy(i + 2).start()

    @pl.when(i + 1 < n_prompt_tiles)
    def _():
        r_copy(i + 1).start()

    @pl.when(i < n_prompt_tiles)
    def _():
        a_copy(i).wait()

    @pl.when((i >= 1) & (i <= n_prompt_tiles))
    def _():
        r_copy(i - 1).wait()

    def first():
        wb_ref[...] = w_ref[...].astype(BF16)

    def epilogue(res):
        y = y_ref[...] + bias_ref[...] if has_bias else y_ref[...]
        _residual_ln_store(y, res, g_ref, b_ref, of_ref, ob_ref)

    _deferred_projection(i, n_prompt_tiles, first, lambda: a_buf[i % RING],
                         lambda: as_ref[...].astype(BF16), wb_ref, y_ref,
                         lambda: epilogue(r_buf.at[(i - 1) % RING]), functools.partial(epilogue, ress_ref),
                         lambda: _zero_outputs(of_ref, ob_ref))


def _mixer_out(a_prompt, a_sample, w, layer, bias, res_prompt, res_sample, res_sample_tile, m, g, b, name):
    k = a_prompt.shape[1]
    tm = TM_OUT
    n_prompt_tiles = a_prompt.shape[0] // tm
    assert a_sample.shape[0] == tm and n_prompt_tiles >= 2
    has_bias = bias is not None
    vec = pl.BlockSpec((1, D_MODEL), lambda i: (0, 0))
    lagged = pl.BlockSpec((tm, D_MODEL), lambda i: (_lag(i), 0))
    in_hbm = pl.BlockSpec(memory_space=pl.ANY)
    in_specs = [in_hbm,
                pl.BlockSpec((tm, k), lambda i: (0, 0)),
                _single_buffered((None, k, D_MODEL), lambda i: (layer, 0, 0))]
    args = [a_prompt, a_sample, w]
    if has_bias:
        in_specs.append(vec)
        args.append(bias)
    in_specs += [in_hbm, pl.BlockSpec((tm, D_MODEL), lambda i: (res_sample_tile, 0)), vec, vec]
    args += [res_prompt, res_sample, g, b]
    return pl.pallas_call(
        functools.partial(_mixer_out_kernel, has_bias=has_bias, n_prompt_tiles=n_prompt_tiles),
        grid=(m // tm + 1,),
        in_specs=in_specs,
        out_specs=[lagged, lagged],
        out_shape=[jax.ShapeDtypeStruct((m, D_MODEL), F32),
                   jax.ShapeDtypeStruct((m, D_MODEL), BF16)],
        scratch_shapes=[pltpu.VMEM((k, D_MODEL), BF16), pltpu.VMEM((tm, D_MODEL), F32),
                        pltpu.VMEM((RING, tm, k), BF16), pltpu.VMEM((RING, tm, D_MODEL), F32),
                        pltpu.SemaphoreType.DMA((RING,)), pltpu.SemaphoreType.DMA((RING,))],
        compiler_params=_cparams("arbitrary"),
        name=name,
    )(*args)


FFN_W_CHUNK = 256


def _load_cast_weight(w_hbm, layer, wb_ref, stage_ref, sem):
    n_chunks = wb_ref.shape[0] // FFN_W_CHUNK

    def chunk_copy(c, slot):
        return pltpu.make_async_copy(w_hbm.at[layer, pl.ds(c * FFN_W_CHUNK, FFN_W_CHUNK), :],
                                     stage_ref.at[slot], sem.at[slot])

    chunk_copy(0, 0).start()
    for c in range(n_chunks):
        slot = c % 2
        if c + 1 < n_chunks:
            chunk_copy(c + 1, 1 - slot).start()
        chunk_copy(c, slot).wait()
        wb_ref[c * FFN_W_CHUNK:(c + 1) * FFN_W_CHUNK, :] = stage_ref[slot].astype(BF16)


def _ffn_out_kernel(ap_ref, as_ref, w_hbm, res_ref, g_ref, b_ref, of_ref, ob_ref, y_ref, wb_ref, stage_ref, sem,
                    *, n_prompt_tiles, layer):
    def epilogue():
        _residual_ln_store(y_ref[...], res_ref, g_ref, b_ref, of_ref, ob_ref)

    _deferred_projection(pl.program_id(0), n_prompt_tiles,
                         functools.partial(_load_cast_weight, w_hbm, layer, wb_ref, stage_ref, sem),
                         lambda: ap_ref[...], lambda: as_ref[...],
                         wb_ref, y_ref, epilogue, epilogue, lambda: _zero_outputs(of_ref, ob_ref))


def _ffn_out_final_kernel(ap_ref, as_ref, w_hbm, res_ref, g_ref, b_ref, op_ref, os_ref, y_ref, wb_ref, stage_ref, sem,
                          *, n_prompt_tiles, layer):
    def ln():
        return _layer_norm(ALPHA * res_ref[...] + y_ref[...], g_ref[...], b_ref[...])

    def epilogue():
        op_ref[...] = ln()

    def sample_epilogue():
        os_ref[...] = ln()

    _deferred_projection(pl.program_id(0), n_prompt_tiles,
                         functools.partial(_load_cast_weight, w_hbm, layer, wb_ref, stage_ref, sem),
                         lambda: ap_ref[...], lambda: as_ref[...],
                         wb_ref, y_ref, epilogue, sample_epilogue, None)


def _ffn_out(a_prompt, a_sample, w, layer, res, g, b, final):
    m = res.shape[0]
    k = a_prompt.shape[1]
    tm = TM_OUT
    n_prompt_tiles = a_prompt.shape[0] // tm
    assert a_sample.shape[0] == tm and k % FFN_W_CHUNK == 0
    vec = pl.BlockSpec((1, D_MODEL), lambda i: (0, 0))
    lagged = pl.BlockSpec((tm, D_MODEL), lambda i: (_lag(i), 0))
    in_specs = [pl.BlockSpec((tm, k), lambda i: (jnp.minimum(i, n_prompt_tiles - 1), 0)),
                _single_buffered((tm, k), lambda i: (0, 0)),
                pl.BlockSpec(memory_space=pl.ANY),
                lagged, vec, vec]
    scratch = [pltpu.VMEM((tm, D_MODEL), F32), pltpu.VMEM((k, D_MODEL), BF16),
               pltpu.VMEM((2, FFN_W_CHUNK, D_MODEL), F32), pltpu.SemaphoreType.DMA((2,))]
    if final:
        return pl.pallas_call(
            functools.partial(_ffn_out_final_kernel, n_prompt_tiles=n_prompt_tiles, layer=layer),
            grid=(n_prompt_tiles + 2,),
            in_specs=in_specs,
            out_specs=[pl.BlockSpec((tm, D_MODEL), lambda i: (jnp.minimum(_lag(i), n_prompt_tiles - 1), 0)),
                       pl.BlockSpec((tm, D_MODEL), lambda i: (0, 0))],
            out_shape=[jax.ShapeDtypeStruct((n_prompt_tiles * tm, D_MODEL), F32),
                       jax.ShapeDtypeStruct((tm, D_MODEL), F32)],
            scratch_shapes=scratch,
            compiler_params=_cparams("arbitrary"),
            name="ffn_out_final",
        )(a_prompt, a_sample, w, res, g, b)
    return pl.pallas_call(
        functools.partial(_ffn_out_kernel, n_prompt_tiles=n_prompt_tiles, layer=layer),
        grid=(m // tm + 1,),
        in_specs=in_specs,
        out_specs=[lagged, lagged],
        out_shape=[jax.ShapeDtypeStruct((m, D_MODEL), F32),
                   jax.ShapeDtypeStruct((m, D_MODEL), BF16)],
        scratch_shapes=scratch,
        compiler_params=_cparams("arbitrary"),
        name="ffn_out",
    )(a_prompt, a_sample, w, res, g, b)


def _conv_glu(g, val, g1, g2, cw, cb):
    conv = cb + cw[0:1, :] * g2
    conv = conv + cw[1:2, :] * g1
    conv = conv + cw[2:3, :] * g
    return (_gelu(conv) * val).astype(BF16)


def _ffn_in_kernel(xp_ref, xs_ref, wg_ref, wv_ref, cw_ref, cb_ref, p1_ref, p2_ref, u_ref, us_ref, glp_ref, gls_ref,
                   wgb_ref, wvb_ref, carry_ref, gs_ref, vs_ref, *, seq_tiles, n_prompt_tiles, sample_seq):
    i = pl.program_id(1)
    tm, tn = u_ref.shape
    rows_s = us_ref.shape[0]
    col_slices = [slice(c0, c0 + MXU_COLS) for c0 in range(0, tn, MXU_COLS)]

    def dots(x_ref):
        rows = x_ref.shape[0]
        x = x_ref[...]
        for cs in col_slices:
            gs_ref[0:rows, cs] = _dot(x, wgb_ref[:, cs])
            vs_ref[0:rows, cs] = _dot(x, wvb_ref[:, cs])

    def prompt_epilogue():
        row = lax.broadcasted_iota(jnp.int32, (tm, MXU_COLS), 0)
        first = (i - 1) % seq_tiles == 0
        for cs in col_slices:
            g, val = gs_ref[:, cs], vs_ref[:, cs]
            above = jnp.where(first, 0.0, carry_ref[:, cs])
            g1 = jnp.where(row >= 1, pltpu.roll(g, 1, axis=0), above[7:8, :])
            g2 = jnp.where(row >= 2, pltpu.roll(g, 2, axis=0),
                           jnp.where(row == 0, above[6:7, :], above[7:8, :]))
            u_ref[:, cs] = _conv_glu(g, val, g1, g2, cw_ref[:, cs], cb_ref[:, cs])
            carry_ref[:, cs] = g[tm - 8:tm, :]
            glp_ref[:, cs] = g[tm - 8:tm, :]

    @pl.when(i == 0)
    def _():
        wgb_ref[...] = wg_ref[...].astype(BF16)
        wvb_ref[...] = wv_ref[...].astype(BF16)
        dots(xp_ref)

    @pl.when((i > 0) & (i < n_prompt_tiles))
    def _():
        prompt_epilogue()
        dots(xp_ref)

    @pl.when(i == n_prompt_tiles)
    def _():
        prompt_epilogue()
        dots(xs_ref)

    @pl.when(i == n_prompt_tiles + 1)
    def _():
        pos = lax.broadcasted_iota(jnp.int32, (rows_s, MXU_COLS), 0) % sample_seq
        for cs in col_slices:
            g, val = gs_ref[0:rows_s, cs], vs_ref[0:rows_s, cs]
            g1 = jnp.where(pos >= 1, pltpu.roll(g, 1, axis=0), p1_ref[:, cs])
            g2 = jnp.where(pos >= 2, pltpu.roll(g, 2, axis=0), p2_ref[:, cs])
            us_ref[:, cs] = _conv_glu(g, val, g1, g2, cw_ref[:, cs], cb_ref[:, cs])
            gls_ref[:, cs] = g


def _ffn_in(xb, w_in, layer, conv_w, conv_b, prev1, prev2, n_prompt_seqs, seq, sample_seq, tn):
    tm = TM_FFN_IN
    mp = n_prompt_seqs * seq
    nj = D_FF // tn
    seq_tiles = seq // tm
    n_prompt_tiles = mp // tm
    sample = pl.BlockSpec((SAMPLE_ROWS, tn), lambda j, i: (0, j))
    prompt_tile = lambda i: jnp.minimum(_lag(i), n_prompt_tiles - 1)
    return pl.pallas_call(
        functools.partial(_ffn_in_kernel, seq_tiles=seq_tiles, n_prompt_tiles=n_prompt_tiles,
                          sample_seq=sample_seq),
        grid=(nj, n_prompt_tiles + 2),
        in_specs=[pl.BlockSpec((tm, D_MODEL), lambda j, i: (jnp.minimum(i, n_prompt_tiles - 1), 0)),
                  pl.BlockSpec((SAMPLE_ROWS, D_MODEL), lambda j, i: (mp // SAMPLE_ROWS, 0)),
                  pl.BlockSpec((None, D_MODEL, tn), lambda j, i: (layer, 0, j)),
                  pl.BlockSpec((None, D_MODEL, tn), lambda j, i: (layer, 0, nj + j)),
                  pl.BlockSpec((None, CONV_W, tn), lambda j, i: (layer, 0, j)),
                  pl.BlockSpec((None, 1, tn), lambda j, i: (layer, 0, j)),
                  sample, sample],
        out_specs=[pl.BlockSpec((tm, tn), lambda j, i: (prompt_tile(i), j)),
                   sample,
                   pl.BlockSpec((8, tn), lambda j, i: (prompt_tile(i) // seq_tiles, j)),
                   sample],
        out_shape=[jax.ShapeDtypeStruct((mp, D_FF), BF16),
                   jax.ShapeDtypeStruct((SAMPLE_ROWS, D_FF), BF16),
                   jax.ShapeDtypeStruct((n_prompt_seqs * 8, D_FF), F32),
                   jax.ShapeDtypeStruct((SAMPLE_ROWS, D_FF), F32)],
        scratch_shapes=[pltpu.VMEM((D_MODEL, tn), BF16), pltpu.VMEM((D_MODEL, tn), BF16),
                        pltpu.VMEM((8, tn), F32), pltpu.VMEM((tm, tn), F32), pltpu.VMEM((tm, tn), F32)],
        compiler_params=_cparams("arbitrary", "arbitrary"),
        name="ffn_in",
    )(xb, xb, w_in, w_in, conv_w, conv_b, prev1, prev2)


def _swa_qkv_kernel(x_ref, w_ref, b_ref, ca_ref, cm_ref, cp_ref, o_ref, wb_ref, y_ref,
                    *, n_rot_cols, n_prompt_tiles, n_tiles):
    j = pl.program_id(0)
    tm, tn = o_ref.shape

    def first():
        wb_ref[...] = w_ref[...].astype(BF16)

    def dots(rows):
        x = x_ref[0:rows, :]
        for c0 in range(0, tn, MXU_COLS):
            sl = slice(c0, c0 + MXU_COLS)
            y_ref[0:rows, sl] = _dot(x, wb_ref[:, sl])

    def epilogue(rows):
        reps = MXU_COLS // ca_ref.shape[1]
        ca = jnp.concatenate([ca_ref[0:rows, :]] * reps, axis=1)
        cm = jnp.concatenate([cm_ref[0:rows, :]] * reps, axis=1)
        cp = jnp.concatenate([cp_ref[0:rows, :]] * reps, axis=1)
        lane = lax.broadcasted_iota(jnp.int32, (rows, MXU_COLS), 1) % SWA_HEAD_DIM
        half = ROT_DIM // 2
        for c0 in range(0, tn, MXU_COLS):
            sl = slice(c0, c0 + MXU_COLS)
            rotated = lane < jnp.where(j * tn + c0 < n_rot_cols, ROT_DIM, 0)
            y = y_ref[0:rows, sl] + b_ref[:, sl]
            rot = y * ca + pltpu.roll(y, MXU_COLS - half, axis=1) * cm + pltpu.roll(y, half, axis=1) * cp
            o_ref[0:rows, sl] = jnp.where(rotated, rot, y)
        if rows < tm:
            o_ref[rows:tm, :] = jnp.zeros((tm - rows, tn), F32)

    _deferred_row_tiles(pl.program_id(1), n_prompt_tiles, n_tiles, tm, first, dots, epilogue)


def _rope_tables(pos):
    half = ROT_DIM // 2
    inv = ROPE_THETA ** (-jnp.arange(half, dtype=F32) / half)
    ang = pos.astype(F32)[:, None] * inv
    cos, sin = jnp.cos(ang), jnp.sin(ang)
    n = pos.shape[0]
    rest = SWA_HEAD_DIM - ROT_DIM
    ca = jnp.concatenate([cos, cos, jnp.ones((n, rest), F32)], -1)
    cm = jnp.concatenate([-sin, jnp.zeros((n, half + rest), F32)], -1)
    cp = jnp.concatenate([jnp.zeros((n, half), F32), sin, jnp.zeros((n, rest), F32)], -1)
    return tuple(jnp.tile(t, (1, 128 // SWA_HEAD_DIM)) for t in (ca, cm, cp))


def _swa_qkv(xb, w, layer, b, tables, n_prompt_tiles, seq, tn):
    m = xb.shape[0]
    n = w.shape[2]
    pos_tiles = seq // TM
    n_tiles = m // TM
    tab = pl.BlockSpec((TM, 128), lambda j, i: (jnp.where(_lag(i) < n_prompt_tiles, _lag(i) % pos_tiles, pos_tiles), 0))
    return pl.pallas_call(
        functools.partial(_swa_qkv_kernel, n_rot_cols=SWA_QD + SWA_KD, n_prompt_tiles=n_prompt_tiles,
                          n_tiles=n_tiles),
        grid=(n // tn, n_tiles + 1),
        in_specs=[pl.BlockSpec((TM, D_MODEL), lambda j, i: (jnp.minimum(i, n_tiles - 1), 0)),
                  pl.BlockSpec((None, D_MODEL, tn), lambda j, i: (layer, 0, j)),
                  pl.BlockSpec((1, tn), lambda j, i: (0, j)),
                  tab, tab, tab],
        out_specs=pl.BlockSpec((TM, tn), lambda j, i: (_lag(i), j)),
        out_shape=jax.ShapeDtypeStruct((m, n), F32),
        scratch_shapes=[pltpu.VMEM((D_MODEL, tn), BF16), pltpu.VMEM((TM, tn), F32)],
        compiler_params=_cparams("arbitrary", "arbitrary"),
        name="swa_qkv",
    )(xb, w, b, *tables)


SWA_SCALE = SWA_HEAD_DIM ** -0.5


def _sink_softmax_pv(s, mask, sink, v):
    s = jnp.where(mask, s, -jnp.inf)
    mx = jnp.maximum(jnp.max(s, -1, keepdims=True), sink)
    p = jnp.exp(s - mx)
    denom = jnp.sum(p, -1, keepdims=True) + jnp.exp(sink - mx)
    return _dot((p / denom).astype(BF16), v)


def _swa_prompt_kernel(sink_ref, q_ref, kp_ref, kc_ref, vp_ref, vc_ref, o_ref):
    i = pl.program_id(1)
    q = q_ref[...] * SWA_SCALE
    kb = jnp.concatenate([kp_ref[...], kc_ref[...]], axis=0).astype(BF16)
    vb = jnp.concatenate([vp_ref[...], vc_ref[...]], axis=0).astype(BF16)
    rows = SWA_GROUP * WINDOW
    r = lax.broadcasted_iota(jnp.int32, (rows, 2 * WINDOW), 0) % WINDOW
    c = lax.broadcasted_iota(jnp.int32, (rows, 2 * WINDOW), 1)
    mask = (c > r) & (c <= r + WINDOW) & (c >= jnp.where(i > 0, 0, WINDOW))
    grp = lax.broadcasted_iota(jnp.int32, (rows, 1), 0) // WINDOW
    pieces = []
    for h in range(SWA_KV_HEADS):
        kh = kb[:, h * SWA_HEAD_DIM:(h + 1) * SWA_HEAD_DIM]
        vh = vb[:, h * SWA_HEAD_DIM:(h + 1) * SWA_HEAD_DIM]
        qs = jnp.concatenate(
            [q[:, (h * SWA_GROUP + g) * SWA_HEAD_DIM:(h * SWA_GROUP + g + 1) * SWA_HEAD_DIM]
             for g in range(SWA_GROUP)], axis=0).astype(BF16)
        sink = jnp.zeros((rows, 1), F32)
        for g in range(SWA_GROUP):
            sink = jnp.where(grp == g, sink_ref[h * SWA_GROUP + g], sink)
        o = _sink_softmax_pv(_dot_nt(qs, kh), mask, sink, vh)
        pieces += [o[g * WINDOW:(g + 1) * WINDOW, :] for g in range(SWA_GROUP)]
    o_ref[...] = jnp.concatenate(pieces, axis=-1).astype(o_ref.dtype)


def _swa_prompt_attn(qkv, sinks, batch, seq):
    nb = seq // WINDOW
    kcol = SWA_QD // SWA_KD
    cur = lambda b, i: b * nb + i
    prev = lambda b, i: b * nb + jnp.maximum(i - 1, 0)
    return pl.pallas_call(
        _swa_prompt_kernel,
        grid=(batch, nb),
        in_specs=[pl.BlockSpec(memory_space=pltpu.SMEM),
                  pl.BlockSpec((WINDOW, SWA_QD), lambda b, i: (cur(b, i), 0)),
                  pl.BlockSpec((WINDOW, SWA_KD), lambda b, i: (prev(b, i), kcol)),
                  pl.BlockSpec((WINDOW, SWA_KD), lambda b, i: (cur(b, i), kcol)),
                  pl.BlockSpec((WINDOW, SWA_KD), lambda b, i: (prev(b, i), kcol + 1)),
                  pl.BlockSpec((WINDOW, SWA_KD), lambda b, i: (cur(b, i), kcol + 1))],
        out_specs=pl.BlockSpec((WINDOW, SWA_QD), lambda b, i: (cur(b, i), 0)),
        out_shape=jax.ShapeDtypeStruct((batch * seq, SWA_QD), BF16),
        compiler_params=_cparams("arbitrary", "arbitrary"),
        name="swa_prompt_attn",
    )(sinks, qkv, qkv, qkv, qkv, qkv)


def _swa_sample_kernel(sink_ref, q_ref, kn_ref, vn_ref, kc_ref, vc_ref, o_ref, ko_ref, vo_ref,
                       *, seq, nb, n_real):
    step = pl.program_id(0)
    nbuf = kc_ref.shape[1]

    @pl.when(step < n_real)
    def _():
        rows = SWA_Q_HEADS * seq
        l = lax.broadcasted_iota(jnp.int32, (rows, nbuf + seq), 0) % seq
        c = lax.broadcasted_iota(jnp.int32, (rows, nbuf + seq), 1)
        diff = l + nbuf - c
        mask = (diff >= 0) & (diff < WINDOW)
        row_head = lax.broadcasted_iota(jnp.int32, (rows, SWA_KD), 0) // seq % SWA_KV_HEADS
        lane_head = lax.broadcasted_iota(jnp.int32, (rows, SWA_KD), 1) // SWA_HEAD_DIM
        own = row_head == lane_head
        sink = sink_ref[...]
        for b in range(nb):
            lo, hi = b * seq, (b + 1) * seq
            k_all = jnp.concatenate([kc_ref[b], kn_ref[lo:hi, :]], axis=0)
            v_all = jnp.concatenate([vc_ref[b], vn_ref[lo:hi, :]], axis=0)
            ko_ref[b] = k_all[seq:, :]
            vo_ref[b] = v_all[seq:, :]
            q = q_ref[lo:hi, :] * SWA_SCALE
            blocks = []
            for g in range(SWA_GROUP):
                q_g = jnp.concatenate(
                    [q[:, (h * SWA_GROUP + g) * SWA_HEAD_DIM:(h * SWA_GROUP + g + 1) * SWA_HEAD_DIM]
                     for h in range(SWA_KV_HEADS)], axis=1)
                blocks += [q_g] * SWA_KV_HEADS
            q_own = jnp.where(own, jnp.concatenate(blocks, axis=0), 0.0).astype(BF16)
            o = _sink_softmax_pv(_dot_nt(q_own, k_all.astype(BF16)), mask, sink, v_all.astype(BF16))
            pieces = []
            for h in range(SWA_KV_HEADS):
                for g in range(SWA_GROUP):
                    r0 = (g * SWA_KV_HEADS + h) * seq
                    pieces.append(o[r0:r0 + seq, h * SWA_HEAD_DIM:(h + 1) * SWA_HEAD_DIM])
            o_ref[lo:hi, :] = jnp.concatenate(pieces, axis=-1)

    @pl.when(step >= n_real)
    def _():
        o_ref[...] = jnp.zeros_like(o_ref)


def _swa_sample_attn(qkv, row0, sinks, k_cache, v_cache, batch, seq, nb):
    nbuf = k_cache.shape[1]
    sink_rows = jnp.repeat(sinks.T.reshape(-1), seq)[:, None]
    kcol = SWA_QD // SWA_KD
    rows = nb * seq
    n_real = batch // nb
    blk0 = row0 // rows
    rb = lambda s: blk0 + jnp.minimum(s, n_real - 1)
    cache = pl.BlockSpec((nb, nbuf, SWA_KD), lambda s: (jnp.minimum(s, n_real - 1), 0, 0))
    return pl.pallas_call(
        functools.partial(_swa_sample_kernel, seq=seq, nb=nb, n_real=n_real),
        grid=(SAMPLE_ROWS // rows,),
        in_specs=[pl.BlockSpec(sink_rows.shape, lambda s: (0, 0)),
                  pl.BlockSpec((rows, SWA_QD), lambda s: (rb(s), 0)),
                  pl.BlockSpec((rows, SWA_KD), lambda s: (rb(s), kcol)),
                  pl.BlockSpec((rows, SWA_KD), lambda s: (rb(s), kcol + 1)),
                  cache, cache],
        out_specs=[pl.BlockSpec((rows, SWA_QD), lambda s: (s, 0)), cache, cache],
        out_shape=[jax.ShapeDtypeStruct((SAMPLE_ROWS, SWA_QD), F32),
                   jax.ShapeDtypeStruct((batch, nbuf, SWA_KD), F32),
                   jax.ShapeDtypeStruct((batch, nbuf, SWA_KD), F32)],
        compiler_params=_cparams("arbitrary"),
        name="swa_sample_attn",
    )(sink_rows, qkv, qkv, qkv, k_cache, v_cache)


def _sg_in_kernel(x_ref, w_ref, b_ref, g_ref, beta_ref, u_ref, v_ref, vs_ref, *, n_prompt_tiles):
    i = pl.program_id(0)
    tm = u_ref.shape[0]

    def body(rows):
        x = x_ref[0:rows, :]
        u_ref[0:rows, :] = _gelu(_dot(x, w_ref[:, 0:SG_WIDTH]) + b_ref[:, 0:SG_WIDTH])
        v = _layer_norm(_gelu(_dot(x, w_ref[:, SG_WIDTH:2 * SG_WIDTH]) + b_ref[:, SG_WIDTH:2 * SG_WIDTH]),
                        g_ref[...], beta_ref[...])
        v_ref[0:rows, :] = v.astype(BF16)
        return v

    def zero_rows(start):
        u_ref[start:tm, :] = jnp.zeros((tm - start, SG_WIDTH), F32)
        v_ref[start:tm, :] = jnp.zeros((tm - start, SG_WIDTH), BF16)

    @pl.when(i < n_prompt_tiles)
    def _():
        body(tm)

    @pl.when(i == n_prompt_tiles)
    def _():
        vs_ref[...] = body(SAMPLE_ROWS)
        zero_rows(SAMPLE_ROWS)

    pl.when(i > n_prompt_tiles)(lambda: zero_rows(0))


def _sg_in(xb, w_b, layer, b, ln_g, ln_b, n_prompt_tiles):
    m = xb.shape[0]
    tm = TM_SG_IN
    assert SAMPLE_ROWS <= tm
    vec = pl.BlockSpec((1, SG_WIDTH), lambda i: (0, 0))
    tile = pl.BlockSpec((tm, SG_WIDTH), lambda i: (i, 0))
    return pl.pallas_call(
        functools.partial(_sg_in_kernel, n_prompt_tiles=n_prompt_tiles),
        grid=(m // tm,),
        in_specs=[pl.BlockSpec((tm, D_MODEL), lambda i: (i, 0)),
                  _single_buffered((None, D_MODEL, 2 * SG_WIDTH), lambda i: (layer, 0, 0)),
                  pl.BlockSpec((1, 2 * SG_WIDTH), lambda i: (0, 0)),
                  vec, vec],
        out_specs=[tile, tile, pl.BlockSpec((SAMPLE_ROWS, SG_WIDTH), lambda i: (0, 0))],
        out_shape=[jax.ShapeDtypeStruct((m, SG_WIDTH), F32),
                   jax.ShapeDtypeStruct((m, SG_WIDTH), BF16),
                   jax.ShapeDtypeStruct((SAMPLE_ROWS, SG_WIDTH), F32)],
        compiler_params=_cparams("arbitrary"),
        name="sg_in",
    )(xb, w_b, b, ln_g, ln_b)


def _sg_mix(u_ref, v_ref, ws_ref, bs_ref, chunk, period):
    tm = u_ref.shape[0]
    r = lax.broadcasted_iota(jnp.int32, (chunk, chunk), 0)
    c = lax.broadcasted_iota(jnp.int32, (chunk, chunk), 1)
    mask = (c <= r) & (r // period == c // period)
    ws = [jnp.where(mask, ws_ref[g], 0.0).astype(BF16) for g in range(SG_GROUPS)]
    rows = []
    for n in range(tm // chunk):
        lo, hi = n * chunk, (n + 1) * chunk
        cols = []
        for g in range(SG_GROUPS):
            vg = v_ref[lo:hi, g * SG_GW:(g + 1) * SG_GW].astype(BF16)
            mixed = _dot(ws[g], vg) + bs_ref[:, g:g + 1]
            cols.append((u_ref[lo:hi, g * SG_GW:(g + 1) * SG_GW] * mixed).astype(BF16))
        rows.append(jnp.concatenate(cols, axis=-1))
    return jnp.concatenate(rows, axis=0)


def _sg_out_kernel(u_ref, v_ref, wsp_ref, bsp_ref, wss_ref, bss_ref, w_ref, bias_ref, res_ref, g_ref, b_ref,
                   of_ref, ob_ref, wb_ref, y_ref, *, n_prompt_tiles, sample_seq):
    def first():
        wb_ref[...] = w_ref[...].astype(BF16)

    def epilogue():
        _residual_ln_store(y_ref[...] + bias_ref[...], res_ref, g_ref, b_ref, of_ref, ob_ref)

    _deferred_projection(pl.program_id(0), n_prompt_tiles, first,
                         lambda: _sg_mix(u_ref, v_ref, wsp_ref, bsp_ref, SG_CHUNK, SG_CHUNK),
                         lambda: _sg_mix(u_ref, v_ref, wss_ref, bss_ref, wss_ref.shape[1], sample_seq),
                         wb_ref, y_ref, epilogue, epilogue, lambda: _zero_outputs(of_ref, ob_ref))


def _sg_out(u, v, ws_p, bs_p, ws_s, bs_s, w_out, layer, b_out, res, g, b, n_prompt_tiles, sample_seq):
    m = res.shape[0]
    tm = TM_OUT
    vec = pl.BlockSpec((1, D_MODEL), lambda i: (0, 0))
    full = lambda a: pl.BlockSpec(a.shape, lambda i: (0,) * a.ndim)
    lagged = pl.BlockSpec((tm, D_MODEL), lambda i: (_lag(i), 0))
    mixed_tile = pl.BlockSpec((tm, SG_WIDTH), lambda i: (jnp.minimum(i, n_prompt_tiles), 0))
    return pl.pallas_call(
        functools.partial(_sg_out_kernel, n_prompt_tiles=n_prompt_tiles, sample_seq=sample_seq),
        grid=(m // tm + 1,),
        in_specs=[mixed_tile, mixed_tile,
                  full(ws_p), full(bs_p), full(ws_s), full(bs_s),
                  _single_buffered((None, SG_WIDTH, D_MODEL), lambda i: (layer, 0, 0)),
                  vec, lagged, vec, vec],
        out_specs=[lagged, lagged],
        out_shape=[jax.ShapeDtypeStruct((m, D_MODEL), F32),
                   jax.ShapeDtypeStruct((m, D_MODEL), BF16)],
        scratch_shapes=[pltpu.VMEM((SG_WIDTH, D_MODEL), BF16), pltpu.VMEM((tm, D_MODEL), F32)],
        compiler_params=_cparams("arbitrary"),
        name="sg_out",
    )(u, v, ws_p, bs_p, ws_s, bs_s, w_out, b_out, res, g, b)


def kernel(x_prompt, x_sample, state_gla, cache_swa_k, cache_swa_v, state_ffn_conv, ln_mix_g, ln_mix_b, ln_ffn_g, ln_ffn_b, gla_w_in, gla_w_g2, gla_b_g, gla_norm_w, gla_w_out, swa_w_qkv, swa_b_qkv, swa_sinks, swa_w_out, swa_b_out, sg_w_in, sg_b_in, sg_ln_g, sg_ln_b, sg_w_s, sg_b_s, sg_w_out, sg_b_out, ffn_w_in, ffn_conv_w, ffn_conv_b, ffn_w_out):
    bp, lp, _ = x_prompt.shape
    bs, ls, _ = x_sample.shape
    mp, ms = bp * lp, bs * ls
    assert lp % TM == 0 and ms <= SAMPLE_ROWS and SAMPLE_ROWS % ls == 0 and ls >= CONV_W - 1
    assert SAMPLE_ROWS == TM_OUT and SAMPLE_PAD % TM_SG_IN == 0

    m_rows = mp + SAMPLE_PAD
    xp2, xs2 = x_prompt.reshape(mp, D_MODEL), x_sample.reshape(ms, D_MODEL)
    xb = jnp.concatenate([xp2.astype(BF16), xs2.astype(BF16), jnp.zeros((SAMPLE_PAD - ms, D_MODEL), BF16)], axis=0)
    residual = (xp2, jnp.pad(xs2, ((0, TM_OUT - ms), (0, 0))), 0)
    gla_w_proj = gla_w_in.astype(BF16)
    gla_wg2p = jnp.pad(gla_w_g2.astype(BF16), ((0, 0), (0, GLA_GLOW_PAD - GLA_RANK), (0, 0)))
    sg_w_in_b = sg_w_in.astype(BF16)
    conv_b3 = ffn_conv_b.reshape(DEPTH, 1, D_FF)

    gla_p, gla_s, swk_p, swv_p, swk_s, swv_s, sgv_s, conv_p, conv_s = ([] for _ in range(9))
    for i in range(DEPTH):
        j = i // N_MIXERS
        kind = i % N_MIXERS
        ln_g, ln_b = _row(ln_mix_g[i]), _row(ln_mix_b[i])
        if kind == 0:
            proj = _gla_proj(xb, gla_w_proj, j, mp // TM)
            bg, nw = _row(gla_b_g[j]), _row(gla_norm_w[j])
            og_p, st_p = _gla_prompt(proj, gla_wg2p[j], bg, nw, bp, lp, 256)
            last_gla = j == gla_w_in.shape[0] - 1
            og_s, st_s = _gla_sample(proj, mp, gla_wg2p[j], bg, nw, state_gla, j, bs, ls, 8 if bs % 8 == 0 else 1,
                                     earlier=tuple(gla_s) if last_gla else ())
            gla_p.append(st_p)
            gla_s.append(st_s)
            xf, xb = _mixer_out(og_p, og_s, gla_w_out, j, None, *residual, m_rows, ln_g, ln_b, "gla_out")
        elif kind == 1:
            pos = jnp.concatenate([jnp.arange(lp), PAST_LEN + jnp.arange(TM) % ls])
            qkv = _swa_qkv(xb, swa_w_qkv, j, _row(swa_b_qkv[j]), _rope_tables(pos), mp // TM, lp, 1024)
            sinks = swa_sinks[j].reshape(-1)
            oa_p = _swa_prompt_attn(qkv, sinks, bp, lp)
            kv = jnp.stack([qkv[(b + 1) * lp - WINDOW:(b + 1) * lp, SWA_QD:] for b in range(bp)])
            swk_p.append(kv[..., :SWA_KD].reshape(bp, WINDOW, SWA_KV_HEADS, SWA_HEAD_DIM))
            swv_p.append(kv[..., SWA_KD:].reshape(bp, WINDOW, SWA_KV_HEADS, SWA_HEAD_DIM))
            nbuf = cache_swa_k.shape[2]
            oa_s, ko, vo = _swa_sample_attn(qkv, mp, swa_sinks[j], cache_swa_k[j].reshape(bs, nbuf, SWA_KD),
                                            cache_swa_v[j].reshape(bs, nbuf, SWA_KD), bs, ls,
                                            2 if bs % 2 == 0 else 1)
            swk_s.append(ko.reshape(bs, nbuf, SWA_KV_HEADS, SWA_HEAD_DIM))
            swv_s.append(vo.reshape(bs, nbuf, SWA_KV_HEADS, SWA_HEAD_DIM))
            xf, xb = _mixer_out(oa_p, oa_s, swa_w_out, j, _row(swa_b_out[j]), *residual, m_rows, ln_g, ln_b,
                                "swa_out")
        else:
            sg_u, sg_v, sg_v_sample = _sg_in(xb, sg_w_in_b, j, _row(sg_b_in[j]), _row(sg_ln_g[j]),
                                             _row(sg_ln_b[j]), mp // TM_SG_IN)
            sgv_s.append(sg_v_sample[:ms].reshape(bs, ls, SG_WIDTH))
            reps = TM_OUT // ls
            ws_s = jnp.tile(sg_w_s[j][:, :ls, :ls], (1, reps, reps))
            bs_s = jnp.tile(sg_b_s[j][:, :ls].T, (reps, 1))
            xf, xb = _sg_out(sg_u, sg_v, sg_w_s[j], sg_b_s[j].T, ws_s, bs_s, sg_w_out, j, _row(sg_b_out[j]),
                             xf, ln_g, ln_b, mp // TM_OUT, ls)
        st = state_ffn_conv[i]
        prev1 = jnp.pad(st[:, 1:2], ((0, 0), (0, ls - 1), (0, 0))).reshape(ms, D_FF)
        prev2 = jnp.pad(st, ((0, 0), (0, ls - 2), (0, 0))).reshape(ms, D_FF)
        prev1 = jnp.pad(prev1, ((0, SAMPLE_ROWS - ms), (0, 0)))
        prev2 = jnp.pad(prev2, ((0, SAMPLE_ROWS - ms), (0, 0)))
        u_p, u_s, gl_p, gl_s = _ffn_in(xb, ffn_w_in, i, ffn_conv_w, conv_b3, prev1, prev2, bp, lp, ls, 512)
        conv_p.append(gl_p.reshape(bp, 8, D_FF)[:, 8 - (CONV_W - 1):])
        conv_s.append(gl_s[:ms].reshape(bs, ls, D_FF)[:, ls - (CONV_W - 1):])
        final = i == DEPTH - 1
        outs = _ffn_out(u_p, u_s, ffn_w_out, i, xf, _row(ln_ffn_g[i]), _row(ln_ffn_b[i]), final)
        if final:
            y_prompt_rows, y_sample_tile = outs
        else:
            xf, xb = outs
            residual = (xf, xf, mp // TM_OUT)

    yp = y_prompt_rows.reshape(bp, lp, D_MODEL)
    ys = y_sample_tile[:ms].reshape(bs, ls, D_MODEL)
    gla_s_all = gla_s[-1] if len(gla_s) > 1 else jnp.stack(gla_s)
    return (yp, ys, jnp.stack(gla_p), gla_s_all, jnp.stack(swk_p), jnp.stack(swv_p),
            jnp.stack(swk_s), jnp.stack(swv_s), jnp.stack(sgv_s), jnp.stack(conv_p), jnp.stack(conv_s))
```
